```python
import jax
import jax.numpy as jnp
from jax import lax
import numpy as np

D_MODEL = 1024
BATCH = 8
SEQ = 2048
DEPTH = 2

CHUNK = 128
A_HEADS = 4
A_HEAD_DIM = 128
D_A = A_HEADS * A_HEAD_DIM
B_GROUPS = 4
D_B = 512
B_CONV = 3
D_IN_HYB = 2 * D_A + 3 * D_B
D_CAT = D_A + D_B
D_C = D_MODEL
C_CONV = 31
N_EXPERTS = 16
N_GROUPS = 4
EXPERTS_PER_GROUP = N_EXPERTS // N_GROUPS
TOP_K = 2
D_EXPERT = 256
N_EVEN = (DEPTH + 1) // 2
N_ODD = DEPTH // 2
N_MOD = 6
EPS = 1e-6

kernel_name = 'hybrid_sgu_shortconv_conformer_grouped_moe_adaln'


def rms_norm(x, g):
    xf = x.astype(jnp.float32)
    y = xf * lax.rsqrt(jnp.mean(xf * xf, axis=-1, keepdims=True) + EPS)
    return (y * g.astype(jnp.float32)).astype(x.dtype)


def layer_norm(x, g, b):
    xf = x.astype(jnp.float32)
    mu = jnp.mean(xf, axis=-1, keepdims=True)
    var = jnp.mean(jnp.square(xf - mu), axis=-1, keepdims=True)
    y = (xf - mu) * lax.rsqrt(var + EPS)
    return (y * g.astype(jnp.float32) + b.astype(jnp.float32)).astype(x.dtype)


def modulate(h, shift, scale):
    return h * (1.0 + scale[:, None, :]) + shift[:, None, :]


def causal_dw_conv(x, w):
    k = w.shape[0]
    return lax.conv_general_dilated(
        x, w[:, None, :].astype(x.dtype), window_strides=(1,),
        padding=[(k - 1, 0)], dimension_numbers=('NWC', 'WIO', 'NWC'),
        feature_group_count=x.shape[-1])


def chunked_sgu(u, v, ln_g, ln_b, w_s, b_s):
    bsz, s_len, _ = v.shape
    v = layer_norm(v, ln_g, ln_b)
    v = v.reshape(bsz, s_len // CHUNK, CHUNK, A_HEADS, A_HEAD_DIM)
    mask = jnp.tril(jnp.ones((CHUNK, CHUNK), dtype=bool))
    w = jnp.where(mask[None], w_s, jnp.zeros_like(w_s))
    s = jnp.einsum('hts,bnshd->bnthd', w, v) + b_s.T[None, None, :, :, None]
    return u * s.reshape(bsz, s_len, D_A)


def hybrid_ab(h, w_in, ln_g, ln_b, w_s, b_s, conv_w, w_out):
    z = h @ w_in
    u, v, bg, cg, hb = jnp.split(
        z, [D_A, 2 * D_A, 2 * D_A + D_B, 2 * D_A + 2 * D_B], axis=-1)
    y_a = chunked_sgu(jax.nn.gelu(u, approximate=False),
                      jax.nn.gelu(v, approximate=False), ln_g, ln_b, w_s, b_s)
    y_b = bg * causal_dw_conv(cg * hb, conv_w)
    return jnp.concatenate([y_a, y_b], axis=-1) @ w_out


def conformer_conv(h, w1, b1, dw_w, dw_b, ln_g, ln_b, w2, b2):
    z = jax.nn.glu(h @ w1 + b1, axis=-1)
    z = causal_dw_conv(z, dw_w) + dw_b
    z = jax.nn.silu(layer_norm(z, ln_g, ln_b))
    return z @ w2 + b2


def grouped_moe(h, router_w, router_bias, w_gate, w_up, w_down):
    bsz, s_len, d = h.shape
    t = h.reshape(-1, d)
    scores = jax.nn.sigmoid(t.astype(jnp.float32) @ router_w.astype(jnp.float32))
    biased = scores + router_bias.astype(jnp.float32)
    grp = biased.reshape(-1, N_GROUPS, EXPERTS_PER_GROUP)
    grp_score = jnp.sum(lax.top_k(grp, TOP_K)[0], axis=-1)
    best = jnp.argmax(grp_score, axis=-1)
    in_group = (jnp.arange(N_EXPERTS) // EXPERTS_PER_GROUP)[None, :] == best[:, None]
    masked = jnp.where(in_group, biased, -jnp.inf)
    _, idx = lax.top_k(masked, TOP_K)
    wsel = jnp.take_along_axis(scores, idx, axis=-1)
    wsel = wsel / jnp.sum(wsel, axis=-1, keepdims=True)
    gates = jnp.sum(jax.nn.one_hot(idx, N_EXPERTS, dtype=jnp.float32) * wsel[..., None],
                    axis=1).astype(h.dtype)
    a = jax.nn.silu(jnp.einsum('td,edf->tef', t, w_gate)) * jnp.einsum('td,edf->tef', t, w_up)
    y = jnp.einsum('tef,efd->td', a * gates[:, :, None], w_down)
    return y.reshape(bsz, s_len, d)


def setup_inputs(seed: int = 0) -> dict:
    key = jax.random.key(seed)
    ks = jax.random.split(key, 32)
    f32 = jnp.float32

    def nrm(k, shape, scale):
        return jax.random.normal(k, shape, f32) * scale

    d = D_MODEL
    return {
        'x': nrm(ks[0], (BATCH, SEQ, d), 1.0),
        'c': nrm(ks[1], (BATCH, d), 1.0),
        'ada_w': nrm(ks[2], (DEPTH, d, N_MOD * d), 0.5 * d ** -0.5),
        'ada_b': nrm(ks[3], (DEPTH, N_MOD * d), 0.02),
        'norm_mix_g': 1.0 + nrm(ks[4], (DEPTH, d), 0.02),
        'norm_ffn_g': 1.0 + nrm(ks[5], (DEPTH, d), 0.02),
        'hyb_w_in': nrm(ks[6], (N_EVEN, d, D_IN_HYB), d ** -0.5),
        'sgu_ln_g': 1.0 + nrm(ks[7], (N_EVEN, D_A), 0.02),
        'sgu_ln_b': nrm(ks[8], (N_EVEN, D_A), 0.02),
        'sgu_w': nrm(ks[9], (N_EVEN, A_HEADS, CHUNK, CHUNK), CHUNK ** -0.5),
        'sgu_b': 1.0 + nrm(ks[10], (N_EVEN, A_HEADS, CHUNK), 0.02),
        'sconv_w': nrm(ks[11], (N_EVEN, B_CONV, D_B), B_CONV ** -0.5),
        'hyb_w_out': nrm(ks[12], (N_EVEN, D_CAT, d), D_CAT ** -0.5),
        'conf_w1': nrm(ks[13], (N_ODD, d, 2 * D_C), d ** -0.5),
        'conf_b1': nrm(ks[14], (N_ODD, 2 * D_C), 0.02),
        'conf_dw_w': nrm(ks[15], (N_ODD, C_CONV, D_C), C_CONV ** -0.5),
        'conf_dw_b': nrm(ks[16], (N_ODD, D_C), 0.02),
        'conf_ln_g': 1.0 + nrm(ks[17], (N_ODD, D_C), 0.02),
        'conf_ln_b': nrm(ks[18], (N_ODD, D_C), 0.02),
        'conf_w2': nrm(ks[19], (N_ODD, D_C, d), D_C ** -0.5),
        'conf_b2': nrm(ks[20], (N_ODD, d), 0.02),
        'router_w': nrm(ks[21], (d, N_EXPERTS), d ** -0.5),
        'router_bias': nrm(ks[22], (N_EXPERTS,), 0.01),
        'moe_w_gate': nrm(ks[23], (DEPTH, N_EXPERTS, d, D_EXPERT), d ** -0.5),
        'moe_w_up': nrm(ks[24], (DEPTH, N_EXPERTS, d, D_EXPERT), d ** -0.5),
        'moe_w_down': nrm(ks[25], (DEPTH, N_EXPERTS, D_EXPERT, d), D_EXPERT ** -0.5),
        'final_norm_g': 1.0 + nrm(ks[26], (d,), 0.02),
    }


def reference(x, c, ada_w, ada_b, norm_mix_g, norm_ffn_g, hyb_w_in, sgu_ln_g, sgu_ln_b,
              sgu_w, sgu_b, sconv_w, hyb_w_out, conf_w1, conf_b1, conf_dw_w, conf_dw_b,
              conf_ln_g, conf_ln_b, conf_w2, conf_b2, router_w, router_bias, moe_w_gate,
              moe_w_up, moe_w_down, final_norm_g):
    c_act = jax.nn.silu(c)
    for l in range(DEPTH):
        mod = c_act @ ada_w[l] + ada_b[l]
        sh1, sc1, g1, sh2, sc2, g2 = jnp.split(mod, N_MOD, axis=-1)
        h = modulate(rms_norm(x, norm_mix_g[l]), sh1, sc1)
        i = l // 2
        if l % 2 == 0:
            y = hybrid_ab(h, hyb_w_in[i], sgu_ln_g[i], sgu_ln_b[i], sgu_w[i], sgu_b[i],
                          sconv_w[i], hyb_w_out[i])
        else:
            y = conformer_conv(h, conf_w1[i], conf_b1[i], conf_dw_w[i], conf_dw_b[i],
                               conf_ln_g[i], conf_ln_b[i], conf_w2[i], conf_b2[i])
        x = x + g1[:, None, :] * y
        h = modulate(rms_norm(x, norm_ffn_g[l]), sh2, sc2)
        x = x + g2[:, None, :] * grouped_moe(h, router_w, router_bias, moe_w_gate[l],
                                             moe_w_up[l], moe_w_down[l])
    return rms_norm(x, final_norm_g)
```

```python
import functools

import jax
import jax.numpy as jnp
from jax import lax
from jax.experimental import pallas as pl
from jax.experimental.pallas import tpu as pltpu

D_MODEL = 1024
CHUNK = 128
A_HEADS = 4
A_HEAD_DIM = 128
D_A = A_HEADS * A_HEAD_DIM
D_B = 512
B_CONV = 3
D_IN_HYB = 2 * D_A + 3 * D_B
D_C = D_MODEL
C_CONV = 31
N_EXPERTS = 16
N_GROUPS = 4
EXPERTS_PER_GROUP = N_EXPERTS // N_GROUPS
D_EXPERT = 256
N_MOD = 6
EPS = 1e-6

SUBLANES = 8
LANES = 128

SEQ_TILE = 512
MOE_TILE = 1024
MOD_TILE = 1024
C_HALO = 32
B_HALO = 8
VMEM_LIMIT = 56 * 1024 * 1024

F32 = jnp.float32
BF16 = jnp.bfloat16


def _split_bf16(a):
    hi = a.astype(BF16)
    lo = (a - hi.astype(F32)).astype(BF16)
    return hi, lo


def _dot3(a, b):
    ah, al = _split_bf16(a)
    bh, bl = _split_bf16(b)
    d = functools.partial(jnp.dot, preferred_element_type=F32)
    return d(ah, bh) + (d(ah, bl) + d(al, bh))


def _rms(x, g):
    return x * lax.rsqrt(jnp.mean(x * x, axis=-1, keepdims=True) + EPS) * g


def _gelu(x):
    return 0.5 * x * (1.0 + lax.erf(x * (2.0 ** -0.5)))


def _layer_norm(x, g, b):
    mu = jnp.mean(x, axis=-1, keepdims=True)
    xc = x - mu
    var = jnp.mean(xc * xc, axis=-1, keepdims=True)
    return xc * lax.rsqrt(var + EPS) * g + b


def _mod_kernel(c_ref, w_ref, b_ref, o_ref):
    c = c_ref[...]
    c_act = c * jax.nn.sigmoid(c)
    o_ref[0] = _dot3(c_act, w_ref[0]) + b_ref[0]


def _ada_mod(c, ada_w, ada_b):
    depth, d, n = ada_w.shape
    bsz = c.shape[0]
    return pl.pallas_call(
        _mod_kernel,
        grid=(depth, n // MOD_TILE),
        in_specs=[
            pl.BlockSpec((bsz, d), lambda l, j: (0, 0)),
            pl.BlockSpec((1, d, MOD_TILE), lambda l, j: (l, 0, j)),
            pl.BlockSpec((1, 1, MOD_TILE), lambda l, j: (l, 0, j)),
        ],
        out_specs=pl.BlockSpec((1, bsz, MOD_TILE), lambda l, j: (l, 0, j)),
        out_shape=jax.ShapeDtypeStruct((depth, bsz, n), F32),
        compiler_params=pltpu.CompilerParams(
            dimension_semantics=("arbitrary", "arbitrary"), vmem_limit_bytes=VMEM_LIMIT),
        name="ada_mod",
    )(c, ada_w, ada_b.reshape(depth, 1, n))


def _route(h2, rw_ref, rb_ref):
    logits = lax.dot_general(rw_ref[...], h2, (((1,), (1,)), ((), ())),
                             precision=lax.Precision.HIGHEST,
                             preferred_element_type=F32)
    scores = jax.nn.sigmoid(logits)
    biased = scores + rb_ref[...]
    ng, ne = N_GROUPS, EXPERTS_PER_GROUP
    b = [biased[j * ng:(j + 1) * ng] for j in range(ne)]
    s = [scores[j * ng:(j + 1) * ng] for j in range(ne)]
    gscore = None
    for i in range(ne):
        for j in range(i + 1, ne):
            pair = b[i] + b[j]
            gscore = pair if gscore is None else jnp.maximum(gscore, pair)
    gid = lax.broadcasted_iota(jnp.int32, gscore.shape, 0)
    gmax = jnp.max(gscore, axis=0, keepdims=True)
    best = jnp.min(jnp.where(gscore == gmax, gid, ng), axis=0, keepdims=True)
    onb = gid == best
    cb = [jnp.sum(jnp.where(onb, b[j], 0.0), axis=0, keepdims=True) for j in range(ne)]
    cs = [jnp.sum(jnp.where(onb, s[j], 0.0), axis=0, keepdims=True) for j in range(ne)]
    sel = []
    for j in range(ne):
        rank = jnp.zeros_like(best)
        for i in range(ne):
            if i == j:
                continue
            ahead = (cb[i] >= cb[j]) if i < j else (cb[i] > cb[j])
            rank = rank + ahead.astype(jnp.int32)
        sel.append(rank < 2)
    wsum = sum(jnp.where(sel[j], cs[j], 0.0) for j in range(ne))
    rows = [jnp.where(sel[j], cs[j], 0.0) / wsum for j in range(ne)]
    rows.append(best.astype(F32))
    rows.extend([jnp.zeros_like(rows[0])] * (SUBLANES - len(rows)))
    return jnp.concatenate(rows, axis=0)


def _rows_to_token_major(meta_t):
    ts = meta_t.shape[1]
    padded = jnp.concatenate([meta_t, jnp.zeros((LANES - SUBLANES, ts), F32)], axis=0)
    return padded.T


def _ffn_prenorm_and_route(x1, mod, gffn_ref, rw_ref, rb_ref, h2_ref, meta_ref):
    sh2, sc2 = mod[3:4], mod[4:5]
    h2 = _rms(x1, gffn_ref[0]) * (1.0 + sc2) + sh2
    h2_ref[...] = h2.astype(BF16)
    meta_ref[...] = _rows_to_token_major(_route(h2, rw_ref, rb_ref))


def _hybrid_kernel(x_ref, mod_ref, gmix_ref, gffn_ref, win_ref, lng_ref, lnb_ref, sw_ref, sb_ref,
                   cw_ref, wout_ref, rw_ref, rb_ref, x1_ref, h2_ref, meta_ref, pbuf):
    ts = x_ref.shape[1]
    x = x_ref[0]
    mod = mod_ref[0]
    sh1, sc1, g1 = mod[0:1], mod[1:2], mod[2:3]
    h = _rms(x, gmix_ref[0]) * (1.0 + sc1) + sh1
    z = jnp.dot(h.astype(BF16), win_ref[...], preferred_element_type=F32)

    u = _gelu(z[:, :D_A])
    v = _gelu(z[:, D_A:2 * D_A])
    v = _layer_norm(v, lng_ref[...], lnb_ref[...]).astype(BF16)
    tril = (lax.broadcasted_iota(jnp.int32, (CHUNK, CHUNK), 0)
            >= lax.broadcasted_iota(jnp.int32, (CHUNK, CHUNK), 1))
    chunks = []
    for n in range(ts // CHUNK):
        heads = []
        for hd in range(A_HEADS):
            w = jnp.where(tril, sw_ref[hd], 0.0).astype(BF16)
            vb = v[n * CHUNK:(n + 1) * CHUNK, hd * A_HEAD_DIM:(hd + 1) * A_HEAD_DIM]
            heads.append(jnp.dot(w, vb, preferred_element_type=F32) + sb_ref[hd])
        chunks.append(jnp.concatenate(heads, axis=1))
    y_a = u * jnp.concatenate(chunks, axis=0)

    o = 2 * D_A
    bg, cg, hb = z[:, o:o + D_B], z[:, o + D_B:o + 2 * D_B], z[:, o + 2 * D_B:o + 3 * D_B]
    p = cg * hb

    @pl.when(pl.program_id(1) == 0)
    def _():
        pbuf[0:B_HALO, :] = jnp.zeros((B_HALO, D_B), F32)

    pbuf[B_HALO:B_HALO + ts, :] = p
    cw = cw_ref[...]
    conv = (cw[0:1] * pbuf[B_HALO - 2:B_HALO - 2 + ts, :]
            + cw[1:2] * pbuf[B_HALO - 1:B_HALO - 1 + ts, :]
            + cw[2:3] * p)
    pbuf[0:B_HALO, :] = p[ts - B_HALO:ts, :]
    y_b = bg * conv

    ycat = jnp.concatenate([y_a, y_b], axis=1).astype(BF16)
    y = jnp.dot(ycat, wout_ref[...], preferred_element_type=F32)
    x1 = x + g1 * y
    x1_ref[0] = x1
    _ffn_prenorm_and_route(x1, mod, gffn_ref, rw_ref, rb_ref, h2_ref, meta_ref)


def _conformer_kernel(x_ref, mod_ref, gmix_ref, gffn_ref, w1_ref, b1_ref, dww_ref, dwb_ref,
                      lng_ref, lnb_ref, w2_ref, b2_ref, rw_ref, rb_ref,
                      x1_ref, h2_ref, meta_ref, cbuf):
    ts = x_ref.shape[1]
    x = x_ref[0]
    mod = mod_ref[0]
    sh1, sc1, g1 = mod[0:1], mod[1:2], mod[2:3]
    h = _rms(x, gmix_ref[0]) * (1.0 + sc1) + sh1
    zz = jnp.dot(h.astype(BF16), w1_ref[...], preferred_element_type=F32) + b1_ref[...]
    z = zz[:, :D_C] * jax.nn.sigmoid(zz[:, D_C:])

    @pl.when(pl.program_id(1) == 0)
    def _():
        cbuf[0:C_HALO, :] = jnp.zeros((C_HALO, D_C), F32)

    cbuf[C_HALO:C_HALO + ts, :] = z
    base = C_HALO - (C_CONV - 1)
    acc = dww_ref[C_CONV - 1:C_CONV, :] * z
    for k in range(C_CONV - 1):
        acc = acc + dww_ref[k:k + 1, :] * cbuf[base + k:base + k + ts, :]
    cbuf[0:C_HALO, :] = z[ts - C_HALO:ts, :]

    zc = _layer_norm(acc + dwb_ref[...], lng_ref[...], lnb_ref[...])
    zc = zc * jax.nn.sigmoid(zc)
    y = jnp.dot(zc.astype(BF16), w2_ref[...], preferred_element_type=F32) + b2_ref[...]
    x1 = x + g1 * y
    x1_ref[0] = x1
    _ffn_prenorm_and_route(x1, mod, gffn_ref, rw_ref, rb_ref, h2_ref, meta_ref)


def _const_spec(shape):
    nd = len(shape)
    return pl.BlockSpec(shape, lambda b, s: (0,) * nd)


def _mixer_call(kernel, name, x, mod_l, consts, scratch):
    bsz, seq, d = x.shape
    ts = SEQ_TILE
    nseq = seq // ts
    tok_spec = lambda w: pl.BlockSpec((ts, w), lambda b, s: (b * nseq + s, 0))
    return pl.pallas_call(
        kernel,
        grid=(bsz, nseq),
        in_specs=[pl.BlockSpec((1, ts, d), lambda b, s: (b, s, 0)),
                  pl.BlockSpec((1, N_MOD, d), lambda b, s: (b, 0, 0))]
                 + [_const_spec(a.shape) for a in consts],
        out_specs=[pl.BlockSpec((1, ts, d), lambda b, s: (b, s, 0)), tok_spec(d), tok_spec(LANES)],
        out_shape=[jax.ShapeDtypeStruct((bsz, seq, d), F32),
                   jax.ShapeDtypeStruct((bsz * seq, d), BF16),
                   jax.ShapeDtypeStruct((bsz * seq, LANES), F32)],
        scratch_shapes=[scratch],
        compiler_params=pltpu.CompilerParams(
            dimension_semantics=("arbitrary", "arbitrary"), vmem_limit_bytes=VMEM_LIMIT),
        name=name,
    )(x, mod_l, *consts)


def _moe_kernel(x1_ref, h2_ref, meta_ref, mod_ref, wg_ref, wu_ref, wd_ref, fg_ref, o_ref, acc_ref,
                *, final):
    g = pl.program_id(1)

    @pl.when(g == 0)
    def _():
        acc_ref[...] = jnp.zeros_like(acc_ref)

    h = h2_ref[...]
    meta = meta_ref[...]
    in_group = meta[:, EXPERTS_PER_GROUP:EXPERTS_PER_GROUP + 1] == g.astype(F32)
    y = acc_ref[...]
    for j in range(EXPERTS_PER_GROUP):
        gate = jnp.where(in_group, meta[:, j:j + 1], 0.0)
        a = jnp.dot(h, wg_ref[j], preferred_element_type=F32)
        a = a * jax.nn.sigmoid(a) * jnp.dot(h, wu_ref[j], preferred_element_type=F32)
        y = y + jnp.dot((a * gate).astype(BF16), wd_ref[j], preferred_element_type=F32)
    acc_ref[...] = y

    @pl.when(g == N_GROUPS - 1)
    def _():
        x2 = x1_ref[...] + mod_ref[0][5:6] * y
        o_ref[...] = _rms(x2, fg_ref[...]) if final else x2


def _moe_call(x1, h2, meta, mod_l, wg, wu, wd, final_g, final):
    bsz, seq, d = x1.shape
    t = bsz * seq
    tm = MOE_TILE
    per_b = seq // tm
    ne = EXPERTS_PER_GROUP
    out = pl.pallas_call(
        functools.partial(_moe_kernel, final=final),
        grid=(t // tm, N_GROUPS),
        in_specs=[
            pl.BlockSpec((tm, d), lambda i, g: (i, 0)),
            pl.BlockSpec((tm, d), lambda i, g: (i, 0)),
            pl.BlockSpec((tm, LANES), lambda i, g: (i, 0)),
            pl.BlockSpec((1, N_MOD, d), lambda i, g: (i // per_b, 0, 0)),
            pl.BlockSpec((ne, d, D_EXPERT), lambda i, g: (g, 0, 0)),
            pl.BlockSpec((ne, d, D_EXPERT), lambda i, g: (g, 0, 0)),
            pl.BlockSpec((ne, D_EXPERT, d), lambda i, g: (g, 0, 0)),
            pl.BlockSpec((1, d), lambda i, g: (0, 0)),
        ],
        out_specs=pl.BlockSpec((tm, d), lambda i, g: (i, 0)),
        out_shape=jax.ShapeDtypeStruct((t, d), F32),
        scratch_shapes=[pltpu.VMEM((tm, d), F32)],
        compiler_params=pltpu.CompilerParams(
            dimension_semantics=("arbitrary", "arbitrary"), vmem_limit_bytes=VMEM_LIMIT),
        name="moe_final" if final else "moe",
    )(x1.reshape(t, d), h2, meta, mod_l, wg, wu, wd, final_g.reshape(1, d))
    return out.reshape(bsz, seq, d)


def kernel(x, c, ada_w, ada_b, norm_mix_g, norm_ffn_g, hyb_w_in, sgu_ln_g, sgu_ln_b, sgu_w, sgu_b,
           sconv_w, hyb_w_out, conf_w1, conf_b1, conf_dw_w, conf_dw_b, conf_ln_g, conf_ln_b,
           conf_w2, conf_b2, router_w, router_bias, moe_w_gate, moe_w_up, moe_w_down, final_norm_g):
    depth = ada_w.shape[0]
    bsz, seq, d = x.shape
    mod = _ada_mod(c, ada_w, ada_b).reshape(depth, bsz, N_MOD, d)

    perm = jnp.arange(N_EXPERTS).reshape(N_GROUPS, EXPERTS_PER_GROUP).T.reshape(-1)
    rw_t = router_w.T[perm]
    rb_t = router_bias[perm].reshape(N_EXPERTS, 1)

    row = lambda a: a.reshape(1, -1)
    for l in range(depth):
        i = l // 2
        if l % 2 == 0:
            consts = [row(norm_mix_g[l]), row(norm_ffn_g[l]), hyb_w_in[i].astype(BF16),
                      row(sgu_ln_g[i]), row(sgu_ln_b[i]), sgu_w[i],
                      sgu_b[i].reshape(A_HEADS, CHUNK, 1), sconv_w[i],
                      hyb_w_out[i].astype(BF16), rw_t, rb_t]
            x1, h2, meta = _mixer_call(_hybrid_kernel, "mixer_hybrid", x, mod[l], consts,
                                       pltpu.VMEM((SEQ_TILE + B_HALO, D_B), F32))
        else:
            consts = [row(norm_mix_g[l]), row(norm_ffn_g[l]), conf_w1[i].astype(BF16),
                      row(conf_b1[i]), conf_dw_w[i], row(conf_dw_b[i]), row(conf_ln_g[i]),
                      row(conf_ln_b[i]), conf_w2[i].astype(BF16), row(conf_b2[i]), rw_t, rb_t]
            x1, h2, meta = _mixer_call(_conformer_kernel, "mixer_conformer", x, mod[l], consts,
                                       pltpu.VMEM((SEQ_TILE + C_HALO, D_C), F32))
        x = _moe_call(x1, h2, meta, mod[l], moe_w_gate[l].astype(BF16), moe_w_up[l].astype(BF16),
                      moe_w_down[l].astype(BF16), final_norm_g, final=(l == depth - 1))
    return x
```

```python
import functools

import jax
import jax.numpy as jnp
from jax import lax
from jax.experimental import pallas as pl
from jax.experimental.pallas import tpu as pltpu

D_MODEL = 1024
CHUNK = 128
A_HEADS = 4
A_HEAD_DIM = 128
D_A = A_HEADS * A_HEAD_DIM
D_B = 512
B_CONV = 3
D_IN_HYB = 2 * D_A + 3 * D_B
D_C = D_MODEL
C_CONV = 31
N_EXPERTS = 16
N_GROUPS = 4
EXPERTS_PER_GROUP = N_EXPERTS // N_GROUPS
D_EXPERT = 256
N_MOD = 6
EPS = 1e-6

SUBLANES = 8
LANES = 128

SEQ_TILE = 512
MOE_TILE = 1024
MOD_TILE = 1024
C_HALO = 32
B_HALO = 8
VMEM_LIMIT = 56 * 1024 * 1024

F32 = jnp.float32
BF16 = jnp.bfloat16


def _split_bf16(a):
    hi = a.astype(BF16)
    lo = (a - hi.astype(F32)).astype(BF16)
    return hi, lo


def _dot3(a, b):
    ah, al = _split_bf16(a)
    bh, bl = _split_bf16(b)
    d = functools.partial(jnp.dot, preferred_element_type=F32)
    return d(ah, bh) + (d(ah, bl) + d(al, bh))


def _rms(x, g):
    return x * lax.rsqrt(jnp.mean(x * x, axis=-1, keepdims=True) + EPS) * g


def _gelu(x):
    return 0.5 * x * (1.0 + lax.erf(x * (2.0 ** -0.5)))


def _layer_norm(x, g, b):
    mu = jnp.mean(x, axis=-1, keepdims=True)
    xc = x - mu
    var = jnp.mean(xc * xc, axis=-1, keepdims=True)
    return xc * lax.rsqrt(var + EPS) * g + b


def _mod_kernel(c_ref, w_ref, b_ref, o_ref):
    c = c_ref[...]
    c_act = c * jax.nn.sigmoid(c)
    o_ref[0] = _dot3(c_act, w_ref[0]) + b_ref[0]


def _ada_mod(c, ada_w, ada_b):
    depth, d, n = ada_w.shape
    bsz = c.shape[0]
    return pl.pallas_call(
        _mod_kernel,
        grid=(depth, n // MOD_TILE),
        in_specs=[
            pl.BlockSpec((bsz, d), lambda l, j: (0, 0)),
            pl.BlockSpec((1, d, MOD_TILE), lambda l, j: (l, 0, j)),
            pl.BlockSpec((1, 1, MOD_TILE), lambda l, j: (l, 0, j)),
        ],
        out_specs=pl.BlockSpec((1, bsz, MOD_TILE), lambda l, j: (l, 0, j)),
        out_shape=jax.ShapeDtypeStruct((depth, bsz, n), F32),
        compiler_params=pltpu.CompilerParams(
            dimension_semantics=("arbitrary", "arbitrary"), vmem_limit_bytes=VMEM_LIMIT),
        name="ada_mod",
    )(c, ada_w, ada_b.reshape(depth, 1, n))


def _route(h2, rw_ref, rb_ref):
    logits = lax.dot_general(rw_ref[...], h2, (((1,), (1,)), ((), ())),
                             precision=lax.Precision.HIGHEST,
                             preferred_element_type=F32)
    scores = jax.nn.sigmoid(logits)
    biased = scores + rb_ref[...]
    ng, ne = N_GROUPS, EXPERTS_PER_GROUP
    b = [biased[j * ng:(j + 1) * ng] for j in range(ne)]
    s = [scores[j * ng:(j + 1) * ng] for j in range(ne)]
    gscore = None
    for i in range(ne):
        for j in range(i + 1, ne):
            pair = b[i] + b[j]
            gscore = pair if gscore is None else jnp.maximum(gscore, pair)
    gid = lax.broadcasted_iota(jnp.int32, gscore.shape, 0)
    gmax = jnp.max(gscore, axis=0, keepdims=True)
    best = jnp.min(jnp.where(gscore == gmax, gid, ng), axis=0, keepdims=True)
    onb = gid == best
    cb = [jnp.sum(jnp.where(onb, b[j], 0.0), axis=0, keepdims=True) for j in range(ne)]
    cs = [jnp.sum(jnp.where(onb, s[j], 0.0), axis=0, keepdims=True) for j in range(ne)]
    sel = []
    for j in range(ne):
        rank = jnp.zeros_like(best)
        for i in range(ne):
            if i == j:
                continue
            ahead = (cb[i] >= cb[j]) if i < j else (cb[i] > cb[j])
            rank = rank + ahead.astype(jnp.int32)
        sel.append(rank < 2)
    wsum = sum(jnp.where(sel[j], cs[j], 0.0) for j in range(ne))
    rows = [jnp.where(sel[j], cs[j], 0.0) / wsum for j in range(ne)]
    rows.append(best.astype(F32))
    rows.extend([jnp.zeros_like(rows[0])] * (SUBLANES - len(rows)))
    return jnp.concatenate(rows, axis=0)


def _rows_to_token_major(meta_t):
    ts = meta_t.shape[1]
    padded = jnp.concatenate([meta_t, jnp.zeros((LANES - SUBLANES, ts), F32)], axis=0)
    return padded.T


def _ffn_prenorm_and_route(x1, mod, gffn_ref, rw_ref, rb_ref, h2_ref, meta_ref):
    sh2, sc2 = mod[3:4], mod[4:5]
    h2 = _rms(x1, gffn_ref[0]) * (1.0 + sc2) + sh2
    h2_ref[...] = h2.astype(BF16)
    meta_ref[...] = _rows_to_token_major(_route(h2, rw_ref, rb_ref))


def _hybrid_kernel(x_ref, mod_ref, gmix_ref, gffn_ref, win_ref, lng_ref, lnb_ref, sw_ref, sb_ref,
                   cw_ref, wout_ref, rw_ref, rb_ref, x1_ref, h2_ref, meta_ref, pbuf):
    ts = x_ref.shape[1]
    x = x_ref[0]
    mod = mod_ref[0]
    sh1, sc1, g1 = mod[0:1], mod[1:2], mod[2:3]
    h = _rms(x, gmix_ref[0]) * (1.0 + sc1) + sh1
    z = jnp.dot(h.astype(BF16), win_ref[...], preferred_element_type=F32)

    u = _gelu(z[:, :D_A])
    v = _gelu(z[:, D_A:2 * D_A])
    v = _layer_norm(v, lng_ref[...], lnb_ref[...]).astype(BF16)
    tril = (lax.broadcasted_iota(jnp.int32, (CHUNK, CHUNK), 0)
            >= lax.broadcasted_iota(jnp.int32, (CHUNK, CHUNK), 1))
    chunks = []
    for n in range(ts // CHUNK):
        heads = []
        for hd in range(A_HEADS):
            w = jnp.where(tril, sw_ref[hd], 0.0).astype(BF16)
            vb = v[n * CHUNK:(n + 1) * CHUNK, hd * A_HEAD_DIM:(hd + 1) * A_HEAD_DIM]
            heads.append(jnp.dot(w, vb, preferred_element_type=F32) + sb_ref[hd])
        chunks.append(jnp.concatenate(heads, axis=1))
    y_a = u * jnp.concatenate(chunks, axis=0)

    o = 2 * D_A
    bg, cg, hb = z[:, o:o + D_B], z[:, o + D_B:o + 2 * D_B], z[:, o + 2 * D_B:o + 3 * D_B]
    p = cg * hb

    @pl.when(pl.program_id(1) == 0)
    def _():
        pbuf[0:B_HALO, :] = jnp.zeros((B_HALO, D_B), F32)

    pbuf[B_HALO:B_HALO + ts, :] = p
    cw = cw_ref[...]
    conv = (cw[0:1] * pbuf[B_HALO - 2:B_HALO - 2 + ts, :]
            + cw[1:2] * pbuf[B_HALO - 1:B_HALO - 1 + ts, :]
            + cw[2:3] * p)
    pbuf[0:B_HALO, :] = p[ts - B_HALO:ts, :]
    y_b = bg * conv

    ycat = jnp.concatenate([y_a, y_b], axis=1).astype(BF16)
    y = jnp.dot(ycat, wout_ref[...], preferred_element_type=F32)
    x1 = x + g1 * y
    x1_ref[0] = x1
    _ffn_prenorm_and_route(x1, mod, gffn_ref, rw_ref, rb_ref, h2_ref, meta_ref)


CONV_ROWS = 128


def _causal_dw_conv(cbuf, w_ref, obuf, ts):
    n_a = -(-C_CONV // SUBLANES)

    def chunk(i, carry):
        t0 = pl.multiple_of(i * CONV_ROWS, CONV_ROWS)
        for cb in range(D_C // LANES):
            cols = slice(cb * LANES, (cb + 1) * LANES)
            out = None
            for r in range(SUBLANES):
                part = None
                for a in range(n_a):
                    k = C_CONV - 1 - SUBLANES * a - r
                    if k < 0:
                        continue
                    rows = pl.ds(t0 + C_HALO - SUBLANES * (a + 1), CONV_ROWS + SUBLANES)
                    term = w_ref[k:k + 1, cols] * cbuf[rows, cols]
                    part = term if part is None else part + term
                piece = part[SUBLANES - r:SUBLANES - r + CONV_ROWS]
                out = piece if out is None else out + piece
            obuf[pl.ds(t0, CONV_ROWS), cols] = out
        return carry

    lax.fori_loop(0, ts // CONV_ROWS, chunk, 0)

def _conformer_kernel(x_ref, mod_ref, gmix_ref, gffn_ref, w1_ref, b1_ref, dww_ref, dwb_ref,
                      lng_ref, lnb_ref, w2_ref, b2_ref, rw_ref, rb_ref,
                      x1_ref, h2_ref, meta_ref, cbuf, obuf):
    ts = x_ref.shape[1]
    x = x_ref[0]
    mod = mod_ref[0]
    sh1, sc1, g1 = mod[0:1], mod[1:2], mod[2:3]
    h = _rms(x, gmix_ref[0]) * (1.0 + sc1) + sh1
    zz = jnp.dot(h.astype(BF16), w1_ref[...], preferred_element_type=F32) + b1_ref[...]
    z = zz[:, :D_C] * jax.nn.sigmoid(zz[:, D_C:])

    @pl.when(pl.program_id(1) == 0)
    def _():
        cbuf[0:C_HALO, :] = jnp.zeros((C_HALO, D_C), F32)

    cbuf[C_HALO:C_HALO + ts, :] = z
    _causal_dw_conv(cbuf, dww_ref, obuf, ts)
    cbuf[0:C_HALO, :] = z[ts - C_HALO:ts, :]

    zc = _layer_norm(obuf[...] + dwb_ref[...], lng_ref[...], lnb_ref[...])
    zc = zc * jax.nn.sigmoid(zc)
    y = jnp.dot(zc.astype(BF16), w2_ref[...], preferred_element_type=F32) + b2_ref[...]
    x1 = x + g1 * y
    x1_ref[0] = x1
    _ffn_prenorm_and_route(x1, mod, gffn_ref, rw_ref, rb_ref, h2_ref, meta_ref)


def _const_spec(shape):
    nd = len(shape)
    return pl.BlockSpec(shape, lambda b, s: (0,) * nd)


def _mixer_call(kernel, name, x, mod_l, consts, scratch):
    bsz, seq, d = x.shape
    ts = SEQ_TILE
    nseq = seq // ts
    tok_spec = lambda w: pl.BlockSpec((ts, w), lambda b, s: (b * nseq + s, 0))
    return pl.pallas_call(
        kernel,
        grid=(bsz, nseq),
        in_specs=[pl.BlockSpec((1, ts, d), lambda b, s: (b, s, 0)),
                  pl.BlockSpec((1, N_MOD, d), lambda b, s: (b, 0, 0))]
                 + [_const_spec(a.shape) for a in consts],
        out_specs=[pl.BlockSpec((1, ts, d), lambda b, s: (b, s, 0)), tok_spec(d), tok_spec(LANES)],
        out_shape=[jax.ShapeDtypeStruct((bsz, seq, d), F32),
                   jax.ShapeDtypeStruct((bsz * seq, d), BF16),
                   jax.ShapeDtypeStruct((bsz * seq, LANES), F32)],
        scratch_shapes=scratch,
        compiler_params=pltpu.CompilerParams(
            dimension_semantics=("arbitrary", "arbitrary"), vmem_limit_bytes=VMEM_LIMIT),
        name=name,
    )(x, mod_l, *consts)


def _moe_kernel(x1_ref, h2_ref, meta_ref, mod_ref, wg_ref, wu_ref, wd_ref, fg_ref, o_ref, acc_ref,
                *, final):
    g = pl.program_id(1)

    @pl.when(g == 0)
    def _():
        acc_ref[...] = jnp.zeros_like(acc_ref)

    h = h2_ref[...]
    meta = meta_ref[...]
    in_group = meta[:, EXPERTS_PER_GROUP:EXPERTS_PER_GROUP + 1] == g.astype(F32)
    y = acc_ref[...]
    for j in range(EXPERTS_PER_GROUP):
        gate = jnp.where(in_group, meta[:, j:j + 1], 0.0)
        a = jnp.dot(h, wg_ref[j], preferred_element_type=F32)
        a = a * jax.nn.sigmoid(a) * jnp.dot(h, wu_ref[j], preferred_element_type=F32)
        y = y + jnp.dot((a * gate).astype(BF16), wd_ref[j], preferred_element_type=F32)
    acc_ref[...] = y

    @pl.when(g == N_GROUPS - 1)
    def _():
        x2 = x1_ref[...] + mod_ref[0][5:6] * y
        o_ref[...] = _rms(x2, fg_ref[...]) if final else x2


def _moe_call(x1, h2, meta, mod_l, wg, wu, wd, final_g, final):
    bsz, seq, d = x1.shape
    t = bsz * seq
    tm = MOE_TILE
    per_b = seq // tm
    ne = EXPERTS_PER_GROUP
    out = pl.pallas_call(
        functools.partial(_moe_kernel, final=final),
        grid=(t // tm, N_GROUPS),
        in_specs=[
            pl.BlockSpec((tm, d), lambda i, g: (i, 0)),
            pl.BlockSpec((tm, d), lambda i, g: (i, 0)),
            pl.BlockSpec((tm, LANES), lambda i, g: (i, 0)),
            pl.BlockSpec((1, N_MOD, d), lambda i, g: (i // per_b, 0, 0)),
            pl.BlockSpec((ne, d, D_EXPERT), lambda i, g: (g, 0, 0)),
            pl.BlockSpec((ne, d, D_EXPERT), lambda i, g: (g, 0, 0)),
            pl.BlockSpec((ne, D_EXPERT, d), lambda i, g: (g, 0, 0)),
            pl.BlockSpec((1, d), lambda i, g: (0, 0)),
        ],
        out_specs=pl.BlockSpec((tm, d), lambda i, g: (i, 0)),
        out_shape=jax.ShapeDtypeStruct((t, d), F32),
        scratch_shapes=[pltpu.VMEM((tm, d), F32)],
        compiler_params=pltpu.CompilerParams(
            dimension_semantics=("arbitrary", "arbitrary"), vmem_limit_bytes=VMEM_LIMIT),
        name="moe_final" if final else "moe",
    )(x1.reshape(t, d), h2, meta, mod_l, wg, wu, wd, final_g.reshape(1, d))
    return out.reshape(bsz, seq, d)


def kernel(x, c, ada_w, ada_b, norm_mix_g, norm_ffn_g, hyb_w_in, sgu_ln_g, sgu_ln_b, sgu_w, sgu_b,
           sconv_w, hyb_w_out, conf_w1, conf_b1, conf_dw_w, conf_dw_b, conf_ln_g, conf_ln_b,
           conf_w2, conf_b2, router_w, router_bias, moe_w_gate, moe_w_up, moe_w_down, final_norm_g):
    depth = ada_w.shape[0]
    bsz, seq, d = x.shape
    mod = _ada_mod(c, ada_w, ada_b).reshape(depth, bsz, N_MOD, d)

    perm = jnp.arange(N_EXPERTS).reshape(N_GROUPS, EXPERTS_PER_GROUP).T.reshape(-1)
    rw_t = router_w.T[perm]
    rb_t = router_bias[perm].reshape(N_EXPERTS, 1)

    row = lambda a: a.reshape(1, -1)
    for l in range(depth):
        i = l // 2
        if l % 2 == 0:
            consts = [row(norm_mix_g[l]), row(norm_ffn_g[l]), hyb_w_in[i].astype(BF16),
                      row(sgu_ln_g[i]), row(sgu_ln_b[i]), sgu_w[i],
                      sgu_b[i].reshape(A_HEADS, CHUNK, 1), sconv_w[i],
                      hyb_w_out[i].astype(BF16), rw_t, rb_t]
            x1, h2, meta = _mixer_call(_hybrid_kernel, "mixer_hybrid", x, mod[l], consts,
                                       [pltpu.VMEM((SEQ_TILE + B_HALO, D_B), F32)])
        else:
            consts = [row(norm_mix_g[l]), row(norm_ffn_g[l]), conf_w1[i].astype(BF16),
                      row(conf_b1[i]), conf_dw_w[i], row(conf_dw_b[i]), row(conf_ln_g[i]),
                      row(conf_ln_b[i]), conf_w2[i].astype(BF16), row(conf_b2[i]), rw_t, rb_t]
            x1, h2, meta = _mixer_call(_conformer_kernel, "mixer_conformer", x, mod[l], consts,
                                       [pltpu.VMEM((SEQ_TILE + C_HALO, D_C), F32),
                                        pltpu.VMEM((SEQ_TILE, D_C), F32)])
        x = _moe_call(x1, h2, meta, mod[l], moe_w_gate[l].astype(BF16), moe_w_up[l].astype(BF16),
                      moe_w_down[l].astype(BF16), final_norm_g, final=(l == depth - 1))
    return x
```

```python
import functools

import jax
import jax.numpy as jnp
from jax import lax
from jax.experimental import pallas as pl
from jax.experimental.pallas import tpu as pltpu

D_MODEL = 1024
CHUNK = 128
A_HEADS = 4
A_HEAD_DIM = 128
D_A = A_HEADS * A_HEAD_DIM
D_B = 512
B_CONV = 3
D_IN_HYB = 2 * D_A + 3 * D_B
D_C = D_MODEL
C_CONV = 31
N_EXPERTS = 16
N_GROUPS = 4
EXPERTS_PER_GROUP = N_EXPERTS // N_GROUPS
D_EXPERT = 256
N_MOD = 6
EPS = 1e-6

SUBLANES = 8
LANES = 128
ROW_ALIGN = 16

SEQ_TILE = 512
MOE_TILE = 512
MOD_TILE = 1024
C_HALO = 32
B_HALO = 8
CONV_ROWS = 128
SORT_ROWS = SEQ_TILE + N_GROUPS * ROW_ALIGN
ROW_W = D_MODEL + LANES
VMEM_LIMIT = 56 * 1024 * 1024

F32 = jnp.float32
BF16 = jnp.bfloat16
I32 = jnp.int32


def _round_up(n, m):
    return -(-n // m) * m


def _region_rows(t):
    return _round_up(t + (t // SEQ_TILE) * ROW_ALIGN, MOE_TILE)


def _split_bf16(a):
    hi = a.astype(BF16)
    lo = (a - hi.astype(F32)).astype(BF16)
    return hi, lo


def _dot3(a, b):
    ah, al = _split_bf16(a)
    bh, bl = _split_bf16(b)
    d = functools.partial(jnp.dot, preferred_element_type=F32)
    return d(ah, bh) + (d(ah, bl) + d(al, bh))


def _rms(x, g):
    return x * lax.rsqrt(jnp.mean(x * x, axis=-1, keepdims=True) + EPS) * g


def _gelu(x):
    return 0.5 * x * (1.0 + lax.erf(x * (2.0 ** -0.5)))


def _layer_norm(x, g, b):
    mu = jnp.mean(x, axis=-1, keepdims=True)
    xc = x - mu
    var = jnp.mean(xc * xc, axis=-1, keepdims=True)
    return xc * lax.rsqrt(var + EPS) * g + b


def _mod_kernel(c_ref, w_ref, b_ref, o_ref):
    c = c_ref[...]
    c_act = c * jax.nn.sigmoid(c)
    o_ref[0] = _dot3(c_act, w_ref[0]) + b_ref[0]


def _ada_mod(c, ada_w, ada_b):
    depth, d, n = ada_w.shape
    bsz = c.shape[0]
    return pl.pallas_call(
        _mod_kernel,
        grid=(depth, n // MOD_TILE),
        in_specs=[
            pl.BlockSpec((bsz, d), lambda l, j: (0, 0)),
            pl.BlockSpec((1, d, MOD_TILE), lambda l, j: (l, 0, j)),
            pl.BlockSpec((1, 1, MOD_TILE), lambda l, j: (l, 0, j)),
        ],
        out_specs=pl.BlockSpec((1, bsz, MOD_TILE), lambda l, j: (l, 0, j)),
        out_shape=jax.ShapeDtypeStruct((depth, bsz, n), F32),
        compiler_params=pltpu.CompilerParams(
            dimension_semantics=("arbitrary", "arbitrary"), vmem_limit_bytes=VMEM_LIMIT),
        name="ada_mod",
    )(c, ada_w, ada_b.reshape(depth, 1, n))


def _route(h2, rw_ref, rb_ref):
    logits = lax.dot_general(rw_ref[...], h2, (((1,), (1,)), ((), ())),
                             precision=lax.Precision.HIGHEST,
                             preferred_element_type=F32)
    scores = jax.nn.sigmoid(logits)
    biased = scores + rb_ref[...]
    ng, ne = N_GROUPS, EXPERTS_PER_GROUP
    b = [biased[j * ng:(j + 1) * ng] for j in range(ne)]
    s = [scores[j * ng:(j + 1) * ng] for j in range(ne)]
    gscore = None
    for i in range(ne):
        for j in range(i + 1, ne):
            pair = b[i] + b[j]
            gscore = pair if gscore is None else jnp.maximum(gscore, pair)
    gid = lax.broadcasted_iota(I32, gscore.shape, 0)
    gmax = jnp.max(gscore, axis=0, keepdims=True)
    best = jnp.min(jnp.where(gscore == gmax, gid, ng), axis=0, keepdims=True)
    onb = gid == best
    cb = [jnp.sum(jnp.where(onb, b[j], 0.0), axis=0, keepdims=True) for j in range(ne)]
    cs = [jnp.sum(jnp.where(onb, s[j], 0.0), axis=0, keepdims=True) for j in range(ne)]
    sel = []
    for j in range(ne):
        rank = jnp.zeros_like(best)
        for i in range(ne):
            if i == j:
                continue
            ahead = (cb[i] >= cb[j]) if i < j else (cb[i] > cb[j])
            rank = rank + ahead.astype(I32)
        sel.append(rank < 2)
    wsum = sum(jnp.where(sel[j], cs[j], 0.0) for j in range(ne))
    gates = jnp.concatenate([jnp.where(sel[j], cs[j], 0.0) / wsum for j in range(ne)], axis=0)
    return gates, jnp.where(onb, 1.0, 0.0)


def _for_each_chunk(nrows, max_rows, fn):
    size = max_rows
    while size >= ROW_ALIGN:
        off = nrows & ~(2 * size - 1)

        @pl.when((nrows & size) != 0)
        def _(off=off, size=size):
            fn(pl.multiple_of(off, ROW_ALIGN), size)

        size //= 2


def _sort_matrix(q):
    rows = lax.broadcasted_iota(I32, (SORT_ROWS, q.shape[1]), 0)
    return jnp.where(rows == q, 1.0, 0.0).astype(BF16)


def _run_copies(op, src_of, dst_of, sem, nrows, bases, region_rows):
    tile_off = 0
    for g in range(N_GROUPS):
        def fn(off, size, g=g, tile_off=tile_off):
            a = pl.multiple_of(tile_off + off, ROW_ALIGN)
            b = pl.multiple_of(g * region_rows + bases[g] + off, ROW_ALIGN)
            cp = pltpu.make_async_copy(src_of(a, b, size), dst_of(a, b, size), sem)
            getattr(cp, op)()
        _for_each_chunk(nrows[g], SEQ_TILE, fn)
        tile_off = tile_off + nrows[g]


def _prenorm_route_scatter(x1, mod, tile, n_tiles, gffn_ref, rw_ref, rb_ref,
                           hs_ref, q_ref, segb_ref, segn_ref, fill_ref,
                           rowbuf, tri, zbuf, fill_s, pend_s, ssem, zsem):
    ts = x1.shape[0]
    region_rows = hs_ref.shape[0] // N_GROUPS
    sh2, sc2 = mod[3:4], mod[4:5]
    h2 = _rms(x1, gffn_ref[...]) * (1.0 + sc2) + sh2
    gates, onehot = _route(h2, rw_ref, rb_ref)

    @pl.when(tile == 0)
    def _():
        for g in range(N_GROUPS):
            fill_s[g] = 0
        k = lax.broadcasted_iota(I32, (ts, ts), 0)
        t = lax.broadcasted_iota(I32, (ts, ts), 1)
        tri[...] = jnp.where(k <= t, 1.0, 0.0).astype(BF16)

    oh8 = jnp.concatenate([onehot, jnp.zeros((SUBLANES - N_GROUPS, ts), F32)], axis=0).astype(BF16)
    csum = jnp.dot(oh8, tri[...], preferred_element_type=F32)[:N_GROUPS]
    counts = [csum[g, ts - 1].astype(I32) for g in range(N_GROUPS)]
    nrows = [(c + (ROW_ALIGN - 1)) & ~(ROW_ALIGN - 1) for c in counts]
    q = jnp.zeros((1, ts), F32)
    tile_off = 0
    for g in range(N_GROUPS):
        q = q + onehot[g:g + 1] * (csum[g:g + 1] - 1.0 + tile_off.astype(F32)
                                   if g else csum[g:g + 1] - 1.0)
        tile_off = tile_off + nrows[g]
    q = q.astype(I32)
    q_ref[0] = q

    hi = gates.astype(BF16).astype(F32)
    mid = (gates - hi).astype(BF16).astype(F32)
    lo = (gates - hi - mid).astype(BF16).astype(F32)
    meta_t = jnp.concatenate(
        [hi, mid, lo, jnp.zeros((LANES - 3 * EXPERTS_PER_GROUP, ts), F32)], axis=0)
    rows = jnp.concatenate([h2.astype(BF16), meta_t.T.astype(BF16)], axis=1)
    sorted_rows = jnp.dot(_sort_matrix(q), rows, preferred_element_type=F32).astype(BF16)

    src_of = lambda a, b, size: rowbuf.at[pl.ds(a, size)]
    dst_of = lambda a, b, size: hs_ref.at[pl.ds(b, size)]

    @pl.when(tile > 0)
    def _():
        _run_copies("wait", src_of, dst_of, ssem, [pend_s[g] for g in range(N_GROUPS)],
                    [pend_s[N_GROUPS + g] for g in range(N_GROUPS)], region_rows)

    rowbuf[...] = sorted_rows
    bases = [fill_s[g] for g in range(N_GROUPS)]
    for g in range(N_GROUPS):
        pend_s[g] = nrows[g]
        pend_s[N_GROUPS + g] = bases[g]
        segn_ref[tile * N_GROUPS + g] = nrows[g]
        segb_ref[tile * N_GROUPS + g] = bases[g]
        fill_s[g] = bases[g] + nrows[g]
    _run_copies("start", src_of, dst_of, ssem, nrows, bases, region_rows)

    @pl.when(tile == n_tiles - 1)
    def _():
        _run_copies("wait", src_of, dst_of, ssem, nrows, bases, region_rows)
        zbuf[...] = jnp.zeros_like(zbuf)
        for op in ("start", "wait"):
            for g in range(N_GROUPS):
                fill = fill_s[g]
                def fn(off, size, g=g, fill=fill, op=op):
                    b = pl.multiple_of(g * region_rows + fill + off, ROW_ALIGN)
                    cp = pltpu.make_async_copy(zbuf.at[pl.ds(0, size)], hs_ref.at[pl.ds(b, size)], zsem)
                    getattr(cp, op)()
                _for_each_chunk((-fill) & (MOE_TILE - 1), MOE_TILE // 2, fn)
        for g in range(N_GROUPS):
            fill_ref[g] = fill_s[g]


def _gather_prev_ffn(tile, n_tiles, segb_ref, segn_ref, ys_ref, q_ref, ybuf, gsem):
    region_rows = ys_ref.shape[0] // N_GROUPS

    def copies(op, t, slot):
        nrows = [segn_ref[t * N_GROUPS + g] for g in range(N_GROUPS)]
        bases = [segb_ref[t * N_GROUPS + g] for g in range(N_GROUPS)]
        src_of = lambda a, b, size: ys_ref.at[pl.ds(b, size)]
        dst_of = lambda a, b, size: ybuf.at[slot, pl.ds(a, size)]
        _run_copies(op, src_of, dst_of, gsem.at[slot], nrows, bases, region_rows)

    def issue(t, slot):
        ybuf[slot, SEQ_TILE:SORT_ROWS, :] = jnp.zeros((SORT_ROWS - SEQ_TILE, D_MODEL), BF16)
        copies("start", t, slot)

    @pl.when(tile == 0)
    def _():
        issue(tile, 0)

    @pl.when(tile + 1 < n_tiles)
    def _():
        issue(tile + 1, (tile + 1) % 2)

    slot = tile % 2
    copies("wait", tile, slot)
    p = _sort_matrix(q_ref[0])
    return lax.dot_general(p, ybuf[slot], (((0,), (0,)), ((), ())), preferred_element_type=F32)


def _hybrid_kernel(x_ref, mod_ref, gmix_ref, gffn_ref, win_ref, lng_ref, lnb_ref, sw_ref, sb_ref,
                   cw_ref, wout_ref, rw_ref, rb_ref,
                   x1_ref, hs_ref, q_ref, segb_ref, segn_ref, fill_ref,
                   pbuf, rowbuf, tri, zbuf, fill_s, pend_s, ssem, zsem):
    ts = x_ref.shape[1]
    tile = pl.program_id(0) * pl.num_programs(1) + pl.program_id(1)
    n_tiles = pl.num_programs(0) * pl.num_programs(1)
    x = x_ref[0]
    mod = mod_ref[0]
    sh1, sc1, g1 = mod[0:1], mod[1:2], mod[2:3]
    h = _rms(x, gmix_ref[...]) * (1.0 + sc1) + sh1
    z = jnp.dot(h.astype(BF16), win_ref[...], preferred_element_type=F32)

    u = _gelu(z[:, :D_A])
    v = _gelu(z[:, D_A:2 * D_A])
    v = _layer_norm(v, lng_ref[...], lnb_ref[...]).astype(BF16)
    tril = (lax.broadcasted_iota(I32, (CHUNK, CHUNK), 0)
            >= lax.broadcasted_iota(I32, (CHUNK, CHUNK), 1))
    chunks = []
    for n in range(ts // CHUNK):
        heads = []
        for hd in range(A_HEADS):
            w = jnp.where(tril, sw_ref[hd], 0.0).astype(BF16)
            vb = v[n * CHUNK:(n + 1) * CHUNK, hd * A_HEAD_DIM:(hd + 1) * A_HEAD_DIM]
            heads.append(jnp.dot(w, vb, preferred_element_type=F32) + sb_ref[hd])
        chunks.append(jnp.concatenate(heads, axis=1))
    y_a = u * jnp.concatenate(chunks, axis=0)

    o = 2 * D_A
    bg, cg, hb = z[:, o:o + D_B], z[:, o + D_B:o + 2 * D_B], z[:, o + 2 * D_B:o + 3 * D_B]
    p = cg * hb

    @pl.when(pl.program_id(1) == 0)
    def _():
        pbuf[0:B_HALO, :] = jnp.zeros((B_HALO, D_B), F32)

    pbuf[B_HALO:B_HALO + ts, :] = p
    cw = cw_ref[...]
    conv = (cw[0:1] * pbuf[B_HALO - 2:B_HALO - 2 + ts, :]
            + cw[1:2] * pbuf[B_HALO - 1:B_HALO - 1 + ts, :]
            + cw[2:3] * p)
    pbuf[0:B_HALO, :] = p[ts - B_HALO:ts, :]
    y_b = bg * conv

    ycat = jnp.concatenate([y_a, y_b], axis=1).astype(BF16)
    y = jnp.dot(ycat, wout_ref[...], preferred_element_type=F32)
    x1 = x + g1 * y
    x1_ref[0] = x1
    _prenorm_route_scatter(x1, mod, tile, n_tiles, gffn_ref, rw_ref, rb_ref,
                           hs_ref, q_ref, segb_ref, segn_ref, fill_ref,
                           rowbuf, tri, zbuf, fill_s, pend_s, ssem, zsem)


def _causal_dw_conv(cbuf, w_ref, obuf, ts):
    n_a = -(-C_CONV // SUBLANES)

    def chunk(i, carry):
        t0 = pl.multiple_of(i * CONV_ROWS, CONV_ROWS)
        for cb in range(D_C // LANES):
            cols = slice(cb * LANES, (cb + 1) * LANES)
            out = None
            for r in range(SUBLANES):
                part = None
                for a in range(n_a):
                    k = C_CONV - 1 - SUBLANES * a - r
                    if k < 0:
                        continue
                    rows = pl.ds(t0 + C_HALO - SUBLANES * (a + 1), CONV_ROWS + SUBLANES)
                    term = w_ref[k:k + 1, cols] * cbuf[rows, cols]
                    part = term if part is None else part + term
                piece = part[SUBLANES - r:SUBLANES - r + CONV_ROWS]
                out = piece if out is None else out + piece
            obuf[pl.ds(t0, CONV_ROWS), cols] = out
        return carry

    lax.fori_loop(0, ts // CONV_ROWS, chunk, 0)


def _conformer_kernel(psegb_ref, psegn_ref,
                      xp_ref, ys_ref, pq_ref, pmod_ref, mod_ref, gmix_ref, gffn_ref, w1_ref, b1_ref,
                      dww_ref, dwb_ref, lng_ref, lnb_ref, w2_ref, b2_ref, rw_ref, rb_ref,
                      x1_ref, hs_ref, q_ref, segb_ref, segn_ref, fill_ref,
                      cbuf, obuf, ybuf, gsem, rowbuf, tri, zbuf, fill_s, pend_s, ssem, zsem):
    ts = xp_ref.shape[1]
    tile = pl.program_id(0) * pl.num_programs(1) + pl.program_id(1)
    n_tiles = pl.num_programs(0) * pl.num_programs(1)
    y_prev = _gather_prev_ffn(tile, n_tiles, psegb_ref, psegn_ref, ys_ref, pq_ref, ybuf, gsem)
    x = xp_ref[0] + pmod_ref[0][5:6] * y_prev
    mod = mod_ref[0]
    sh1, sc1, g1 = mod[0:1], mod[1:2], mod[2:3]
    h = _rms(x, gmix_ref[...]) * (1.0 + sc1) + sh1
    zz = jnp.dot(h.astype(BF16), w1_ref[...], preferred_element_type=F32) + b1_ref[...]
    z = zz[:, :D_C] * jax.nn.sigmoid(zz[:, D_C:])

    @pl.when(pl.program_id(1) == 0)
    def _():
        cbuf[0:C_HALO, :] = jnp.zeros((C_HALO, D_C), F32)

    cbuf[C_HALO:C_HALO + ts, :] = z
    _causal_dw_conv(cbuf, dww_ref, obuf, ts)
    cbuf[0:C_HALO, :] = z[ts - C_HALO:ts, :]

    zc = _layer_norm(obuf[...] + dwb_ref[...], lng_ref[...], lnb_ref[...])
    zc = zc * jax.nn.sigmoid(zc)
    y = jnp.dot(zc.astype(BF16), w2_ref[...], preferred_element_type=F32) + b2_ref[...]
    x1 = x + g1 * y
    x1_ref[0] = x1
    _prenorm_route_scatter(x1, mod, tile, n_tiles, gffn_ref, rw_ref, rb_ref,
                           hs_ref, q_ref, segb_ref, segn_ref, fill_ref,
                           rowbuf, tri, zbuf, fill_s, pend_s, ssem, zsem)


def _final_kernel(psegb_ref, psegn_ref, xp_ref, ys_ref, pq_ref, pmod_ref, fg_ref, o_ref, ybuf, gsem):
    tile = pl.program_id(0) * pl.num_programs(1) + pl.program_id(1)
    n_tiles = pl.num_programs(0) * pl.num_programs(1)
    y_prev = _gather_prev_ffn(tile, n_tiles, psegb_ref, psegn_ref, ys_ref, pq_ref, ybuf, gsem)
    x = xp_ref[0] + pmod_ref[0][5:6] * y_prev
    o_ref[0] = _rms(x, fg_ref[...])


def _scatter_outputs(bsz, seq, d):
    t = bsz * seq
    nseq = seq // SEQ_TILE
    n_tiles = t // SEQ_TILE
    smem = pl.BlockSpec(memory_space=pltpu.SMEM)
    specs = [pl.BlockSpec((1, SEQ_TILE, d), lambda b, s, *_: (b, s, 0)),
             pl.BlockSpec(memory_space=pl.ANY),
             pl.BlockSpec((1, 1, SEQ_TILE), lambda b, s, *_: (b * nseq + s, 0, 0)),
             smem, smem, smem]
    shapes = [jax.ShapeDtypeStruct((bsz, seq, d), F32),
              jax.ShapeDtypeStruct((N_GROUPS * _region_rows(t), ROW_W), BF16),
              jax.ShapeDtypeStruct((n_tiles, 1, SEQ_TILE), I32),
              jax.ShapeDtypeStruct((n_tiles * N_GROUPS,), I32),
              jax.ShapeDtypeStruct((n_tiles * N_GROUPS,), I32),
              jax.ShapeDtypeStruct((N_GROUPS,), I32)]
    scratch = [pltpu.VMEM((SORT_ROWS, ROW_W), BF16),
               pltpu.VMEM((SEQ_TILE, SEQ_TILE), BF16),
               pltpu.VMEM((MOE_TILE // 2, ROW_W), BF16),
               pltpu.SMEM((N_GROUPS,), I32),
               pltpu.SMEM((2 * N_GROUPS,), I32),
               pltpu.SemaphoreType.DMA(()),
               pltpu.SemaphoreType.DMA(())]
    return specs, shapes, scratch


def _gather_inputs(bsz, seq, d):
    nseq = seq // SEQ_TILE
    specs = [pl.BlockSpec((1, SEQ_TILE, d), lambda b, s, *_: (b, s, 0)),
             pl.BlockSpec(memory_space=pl.ANY),
             pl.BlockSpec((1, 1, SEQ_TILE), lambda b, s, *_: (b * nseq + s, 0, 0)),
             pl.BlockSpec((1, N_MOD, d), lambda b, s, *_: (b, 0, 0))]
    scratch = [pltpu.VMEM((2, SORT_ROWS, d), BF16), pltpu.SemaphoreType.DMA((2,))]
    return specs, scratch


def _const_specs(consts):
    return [pl.BlockSpec(a.shape, lambda b, s, *_, nd=a.ndim: (0,) * nd) for a in consts]


_PARAMS = pltpu.CompilerParams(dimension_semantics=("arbitrary", "arbitrary"),
                               vmem_limit_bytes=VMEM_LIMIT)


def _hybrid_call(x, mod_l, consts):
    bsz, seq, d = x.shape
    out_specs, out_shapes, sc_scratch = _scatter_outputs(bsz, seq, d)
    return pl.pallas_call(
        _hybrid_kernel,
        grid=(bsz, seq // SEQ_TILE),
        in_specs=[pl.BlockSpec((1, SEQ_TILE, d), lambda b, s: (b, s, 0)),
                  pl.BlockSpec((1, N_MOD, d), lambda b, s: (b, 0, 0))] + _const_specs(consts),
        out_specs=out_specs,
        out_shape=out_shapes,
        scratch_shapes=[pltpu.VMEM((SEQ_TILE + B_HALO, D_B), F32)] + sc_scratch,
        compiler_params=_PARAMS,
        name="mixer_hybrid",
    )(x, mod_l, *consts)


def _conformer_call(prev, mod_prev, mod_l, consts):
    x1p, ysp, qp, segbp, segnp = prev
    bsz, seq, d = x1p.shape
    g_specs, g_scratch = _gather_inputs(bsz, seq, d)
    out_specs, out_shapes, sc_scratch = _scatter_outputs(bsz, seq, d)
    grid_spec = pltpu.PrefetchScalarGridSpec(
        num_scalar_prefetch=2,
        grid=(bsz, seq // SEQ_TILE),
        in_specs=g_specs + [pl.BlockSpec((1, N_MOD, d), lambda b, s, *_: (b, 0, 0))]
                 + _const_specs(consts),
        out_specs=out_specs,
        scratch_shapes=[pltpu.VMEM((SEQ_TILE + C_HALO, D_C), F32), pltpu.VMEM((SEQ_TILE, D_C), F32)]
                       + g_scratch + sc_scratch,
    )
    return pl.pallas_call(
        _conformer_kernel, grid_spec=grid_spec, out_shape=out_shapes,
        compiler_params=_PARAMS, name="mixer_conformer",
    )(segbp, segnp, x1p, ysp, qp, mod_prev, mod_l, *consts)


def _final_call(prev, mod_prev, final_g):
    x1p, ysp, qp, segbp, segnp = prev
    bsz, seq, d = x1p.shape
    g_specs, g_scratch = _gather_inputs(bsz, seq, d)
    grid_spec = pltpu.PrefetchScalarGridSpec(
        num_scalar_prefetch=2,
        grid=(bsz, seq // SEQ_TILE),
        in_specs=g_specs + [pl.BlockSpec((1, d), lambda b, s, *_: (0, 0))],
        out_specs=pl.BlockSpec((1, SEQ_TILE, d), lambda b, s, *_: (b, s, 0)),
        scratch_shapes=g_scratch,
    )
    return pl.pallas_call(
        _final_kernel, grid_spec=grid_spec,
        out_shape=jax.ShapeDtypeStruct((bsz, seq, d), F32),
        compiler_params=_PARAMS, name="final_norm",
    )(segbp, segnp, x1p, ysp, qp, mod_prev, final_g.reshape(1, d))


def _moe_kernel(blk_ref, grp_ref, valid_ref, hs_ref, wg_ref, wu_ref, wd_ref, ys_ref):
    @pl.when(valid_ref[pl.program_id(0)] != 0)
    def _():
        rows = hs_ref[...]
        h = rows[:, :D_MODEL]
        m = rows[:, D_MODEL:].astype(F32)
        ne = EXPERTS_PER_GROUP
        y = None
        for j in range(ne):
            gate = m[:, j:j + 1] + (m[:, ne + j:ne + j + 1] + m[:, 2 * ne + j:2 * ne + j + 1])
            a = jnp.dot(h, wg_ref[0, j], preferred_element_type=F32)
            a = a * jax.nn.sigmoid(a) * jnp.dot(h, wu_ref[0, j], preferred_element_type=F32)
            yj = jnp.dot((a * gate).astype(BF16), wd_ref[0, j], preferred_element_type=F32)
            y = yj if y is None else y + yj
        ys_ref[...] = y.astype(BF16)


def _moe_items(fill, region_rows, n_items):
    tiles = (fill + (MOE_TILE - 1)) // MOE_TILE
    ends = jnp.cumsum(tiles)
    i = jnp.minimum(jnp.arange(n_items, dtype=I32), ends[-1] - 1)
    grp = jnp.sum((i[:, None] >= ends[None, :]).astype(I32), axis=1)
    blk = grp * (region_rows // MOE_TILE) + i - (ends - tiles)[grp]
    valid = (jnp.arange(n_items, dtype=I32) < ends[-1]).astype(I32)
    return blk.astype(I32), grp.astype(I32), valid


def _moe_call(hs, fill, layer, wg, wu, wd, t):
    region_rows = hs.shape[0] // N_GROUPS
    d = D_MODEL
    ne = EXPERTS_PER_GROUP
    n_items = -(-(t + (t // SEQ_TILE) * N_GROUPS * (ROW_ALIGN - 1)) // MOE_TILE) + N_GROUPS
    blk, grp, valid = _moe_items(fill, region_rows, n_items)
    w_idx = lambda i, blk, grp, valid: (layer * N_GROUPS + grp[i], 0, 0, 0)
    grid_spec = pltpu.PrefetchScalarGridSpec(
        num_scalar_prefetch=3,
        grid=(n_items,),
        in_specs=[pl.BlockSpec((MOE_TILE, ROW_W), lambda i, blk, grp, valid: (blk[i], 0)),
                  pl.BlockSpec((1, ne, d, D_EXPERT), w_idx),
                  pl.BlockSpec((1, ne, d, D_EXPERT), w_idx),
                  pl.BlockSpec((1, ne, D_EXPERT, d), w_idx)],
        out_specs=pl.BlockSpec((MOE_TILE, d), lambda i, blk, grp, valid: (blk[i], 0)),
    )
    return pl.pallas_call(
        _moe_kernel, grid_spec=grid_spec,
        out_shape=jax.ShapeDtypeStruct((hs.shape[0], d), BF16),
        compiler_params=pltpu.CompilerParams(dimension_semantics=("arbitrary",),
                                             vmem_limit_bytes=VMEM_LIMIT),
        name="moe",
    )(blk, grp, valid, hs, wg, wu, wd)


def kernel(x, c, ada_w, ada_b, norm_mix_g, norm_ffn_g, hyb_w_in, sgu_ln_g, sgu_ln_b, sgu_w, sgu_b,
           sconv_w, hyb_w_out, conf_w1, conf_b1, conf_dw_w, conf_dw_b, conf_ln_g, conf_ln_b,
           conf_w2, conf_b2, router_w, router_bias, moe_w_gate, moe_w_up, moe_w_down, final_norm_g):
    depth = ada_w.shape[0]
    bsz, seq, d = x.shape
    t = bsz * seq
    assert depth == 2 and seq % SEQ_TILE == 0
    mod = _ada_mod(c, ada_w, ada_b).reshape(depth, bsz, N_MOD, d)

    perm = jnp.arange(N_EXPERTS).reshape(N_GROUPS, EXPERTS_PER_GROUP).T.reshape(-1)
    rw_t = router_w.T[perm]
    rb_t = router_bias[perm].reshape(N_EXPERTS, 1)

    ne = EXPERTS_PER_GROUP
    wg = moe_w_gate.astype(BF16).reshape(depth * N_GROUPS, ne, d, D_EXPERT)
    wu = moe_w_up.astype(BF16).reshape(depth * N_GROUPS, ne, d, D_EXPERT)
    wd = moe_w_down.astype(BF16).reshape(depth * N_GROUPS, ne, D_EXPERT, d)

    row = lambda a: a.reshape(1, -1)
    consts0 = [row(norm_mix_g[0]), row(norm_ffn_g[0]), hyb_w_in[0].astype(BF16),
               row(sgu_ln_g[0]), row(sgu_ln_b[0]), sgu_w[0],
               sgu_b[0].reshape(A_HEADS, CHUNK, 1), sconv_w[0],
               hyb_w_out[0].astype(BF16), rw_t, rb_t]
    x1, hs, q, segb, segn, fill = _hybrid_call(x, mod[0], consts0)
    ys = _moe_call(hs, fill, 0, wg, wu, wd, t)

    consts1 = [row(norm_mix_g[1]), row(norm_ffn_g[1]), conf_w1[0].astype(BF16),
               row(conf_b1[0]), conf_dw_w[0], row(conf_dw_b[0]), row(conf_ln_g[0]),
               row(conf_ln_b[0]), conf_w2[0].astype(BF16), row(conf_b2[0]), rw_t, rb_t]
    x1, hs, q1, segb1, segn1, fill1 = _conformer_call((x1, ys, q, segb, segn), mod[0], mod[1], consts1)
    ys = _moe_call(hs, fill1, 1, wg, wu, wd, t)
    return _final_call((x1, ys, q1, segb1, segn1), mod[1], final_norm_g)
```

```python
import functools

import jax
import jax.numpy as jnp
from jax import lax
from jax.experimental import pallas as pl
from jax.experimental.pallas import tpu as pltpu

D_MODEL = 1024
CHUNK = 128
A_HEADS = 4
A_HEAD_DIM = 128
D_A = A_HEADS * A_HEAD_DIM
D_B = 512
B_CONV = 3
D_IN_HYB = 2 * D_A + 3 * D_B
D_C = D_MODEL
C_CONV = 31
N_EXPERTS = 16
N_GROUPS = 4
EXPERTS_PER_GROUP = N_EXPERTS // N_GROUPS
D_EXPERT = 256
N_MOD = 6
EPS = 1e-6

SUBLANES = 8
LANES = 128
ROW_ALIGN = 16

SEQ_TILE = 512
MOE_TILE = 512
MOD_TILE = 1024
W_STAGE_ROWS = 128
C_HALO = 32
B_HALO = 8
CONV_ROWS = 128
GLU_COLS = 256
SORT_ROWS = SEQ_TILE + N_GROUPS * ROW_ALIGN
ROW_W = D_MODEL + LANES
VMEM_LIMIT = 56 * 1024 * 1024

F32 = jnp.float32
BF16 = jnp.bfloat16
I32 = jnp.int32


def _round_up(n, m):
    return -(-n // m) * m


def _region_rows(t):
    return _round_up(t + (t // SEQ_TILE) * ROW_ALIGN, MOE_TILE)


def _split_bf16(a):
    hi = a.astype(BF16)
    lo = (a - hi.astype(F32)).astype(BF16)
    return hi, lo


def _dot3(a, b):
    ah, al = _split_bf16(a)
    bh, bl = _split_bf16(b)
    d = functools.partial(jnp.dot, preferred_element_type=F32)
    return d(ah, bh) + (d(ah, bl) + d(al, bh))


def _rms(x, g):
    return x * lax.rsqrt(jnp.mean(x * x, axis=-1, keepdims=True) + EPS) * g


def _gelu(x):
    return 0.5 * x * (1.0 + lax.erf(x * (2.0 ** -0.5)))


def _layer_norm(x, g, b):
    mu = jnp.mean(x, axis=-1, keepdims=True)
    xc = x - mu
    var = jnp.mean(xc * xc, axis=-1, keepdims=True)
    return xc * lax.rsqrt(var + EPS) * g + b


def _mod_kernel(c_ref, w_ref, b_ref, o_ref):
    c = c_ref[...]
    c_act = c * jax.nn.sigmoid(c)
    o_ref[0] = _dot3(c_act, w_ref[0]) + b_ref[0]


def _ada_mod(c, ada_w, ada_b):
    depth, d, n = ada_w.shape
    bsz = c.shape[0]
    return pl.pallas_call(
        _mod_kernel,
        grid=(depth, n // MOD_TILE),
        in_specs=[
            pl.BlockSpec((bsz, d), lambda l, j: (0, 0)),
            pl.BlockSpec((1, d, MOD_TILE), lambda l, j: (l, 0, j)),
            pl.BlockSpec((1, 1, MOD_TILE), lambda l, j: (l, 0, j)),
        ],
        out_specs=pl.BlockSpec((1, bsz, MOD_TILE), lambda l, j: (l, 0, j)),
        out_shape=jax.ShapeDtypeStruct((depth, bsz, n), F32),
        compiler_params=pltpu.CompilerParams(
            dimension_semantics=("arbitrary", "arbitrary"), vmem_limit_bytes=VMEM_LIMIT),
        name="ada_mod",
    )(c, ada_w, ada_b.reshape(depth, 1, n))


def _route(h2, rw_ref, rb_ref):
    logits = lax.dot_general(rw_ref[...], h2, (((1,), (1,)), ((), ())),
                             precision=lax.Precision.HIGHEST,
                             preferred_element_type=F32)
    scores = jax.nn.sigmoid(logits)
    biased = scores + rb_ref[...]
    ng, ne = N_GROUPS, EXPERTS_PER_GROUP
    b = [biased[j * ng:(j + 1) * ng] for j in range(ne)]
    s = [scores[j * ng:(j + 1) * ng] for j in range(ne)]
    gscore = None
    for i in range(ne):
        for j in range(i + 1, ne):
            pair = b[i] + b[j]
            gscore = pair if gscore is None else jnp.maximum(gscore, pair)
    gid = lax.broadcasted_iota(I32, gscore.shape, 0)
    gmax = jnp.max(gscore, axis=0, keepdims=True)
    best = jnp.min(jnp.where(gscore == gmax, gid, ng), axis=0, keepdims=True)
    onb = gid == best
    cb = [jnp.sum(jnp.where(onb, b[j], 0.0), axis=0, keepdims=True) for j in range(ne)]
    cs = [jnp.sum(jnp.where(onb, s[j], 0.0), axis=0, keepdims=True) for j in range(ne)]
    sel = []
    for j in range(ne):
        rank = jnp.zeros_like(best)
        for i in range(ne):
            if i == j:
                continue
            ahead = (cb[i] >= cb[j]) if i < j else (cb[i] > cb[j])
            rank = rank + ahead.astype(I32)
        sel.append(rank < 2)
    wsum = sum(jnp.where(sel[j], cs[j], 0.0) for j in range(ne))
    gates = jnp.concatenate([jnp.where(sel[j], cs[j], 0.0) / wsum for j in range(ne)], axis=0)
    return gates, jnp.where(onb, 1.0, 0.0)


def _for_each_chunk(nrows, max_rows, fn):
    size = max_rows
    while size >= ROW_ALIGN:
        off = nrows & ~(2 * size - 1)

        @pl.when((nrows & size) != 0)
        def _(off=off, size=size):
            fn(pl.multiple_of(off, ROW_ALIGN), size)

        size //= 2


def _sort_matrix(q):
    rows = lax.broadcasted_iota(I32, (SORT_ROWS, q.shape[1]), 0)
    return jnp.where(rows == q, 1.0, 0.0).astype(BF16)


def _run_copies(op, src_of, dst_of, sem, nrows, bases, region_rows):
    tile_off = 0
    for g in range(N_GROUPS):
        def fn(off, size, g=g, tile_off=tile_off):
            a = pl.multiple_of(tile_off + off, ROW_ALIGN)
            b = pl.multiple_of(g * region_rows + bases[g] + off, ROW_ALIGN)
            cp = pltpu.make_async_copy(src_of(a, b, size), dst_of(a, b, size), sem)
            getattr(cp, op)()
        _for_each_chunk(nrows[g], SEQ_TILE, fn)
        tile_off = tile_off + nrows[g]


def _prenorm_route_scatter(x1, mod, tile, n_tiles, gffn_ref, rw_ref, rb_ref,
                           hs_ref, q_ref, segb_ref, segn_ref, fill_ref,
                           rowbuf, tri, zbuf, fill_s, pend_s, ssem, zsem):
    ts = x1.shape[0]
    region_rows = hs_ref.shape[0] // N_GROUPS
    sh2, sc2 = mod[3:4], mod[4:5]
    h2 = _rms(x1, gffn_ref[...]) * (1.0 + sc2) + sh2
    gates, onehot = _route(h2, rw_ref, rb_ref)

    @pl.when(tile == 0)
    def _():
        for g in range(N_GROUPS):
            fill_s[g] = 0
        k = lax.broadcasted_iota(I32, (ts, ts), 0)
        t = lax.broadcasted_iota(I32, (ts, ts), 1)
        tri[...] = jnp.where(k <= t, 1.0, 0.0).astype(BF16)

    oh8 = jnp.concatenate([onehot, jnp.zeros((SUBLANES - N_GROUPS, ts), F32)], axis=0).astype(BF16)
    csum = jnp.dot(oh8, tri[...], preferred_element_type=F32)[:N_GROUPS]
    counts = [csum[g, ts - 1].astype(I32) for g in range(N_GROUPS)]
    nrows = [(c + (ROW_ALIGN - 1)) & ~(ROW_ALIGN - 1) for c in counts]
    q = jnp.zeros((1, ts), F32)
    tile_off = 0
    for g in range(N_GROUPS):
        q = q + onehot[g:g + 1] * (csum[g:g + 1] - 1.0 + tile_off.astype(F32)
                                   if g else csum[g:g + 1] - 1.0)
        tile_off = tile_off + nrows[g]
    q = q.astype(I32)
    q_ref[0] = q

    hi = gates.astype(BF16).astype(F32)
    mid = (gates - hi).astype(BF16).astype(F32)
    lo = (gates - hi - mid).astype(BF16).astype(F32)
    meta_t = jnp.concatenate(
        [hi, mid, lo, jnp.zeros((LANES - 3 * EXPERTS_PER_GROUP, ts), F32)], axis=0)
    rows = jnp.concatenate([h2.astype(BF16), meta_t.T.astype(BF16)], axis=1)
    sorted_rows = jnp.dot(_sort_matrix(q), rows, preferred_element_type=F32).astype(BF16)

    src_of = lambda a, b, size: rowbuf.at[pl.ds(a, size)]
    dst_of = lambda a, b, size: hs_ref.at[pl.ds(b, size)]

    @pl.when(tile > 0)
    def _():
        _run_copies("wait", src_of, dst_of, ssem, [pend_s[g] for g in range(N_GROUPS)],
                    [pend_s[N_GROUPS + g] for g in range(N_GROUPS)], region_rows)

    rowbuf[...] = sorted_rows
    bases = [fill_s[g] for g in range(N_GROUPS)]
    for g in range(N_GROUPS):
        pend_s[g] = nrows[g]
        pend_s[N_GROUPS + g] = bases[g]
        segn_ref[tile * N_GROUPS + g] = nrows[g]
        segb_ref[tile * N_GROUPS + g] = bases[g]
        fill_s[g] = bases[g] + nrows[g]
    _run_copies("start", src_of, dst_of, ssem, nrows, bases, region_rows)

    @pl.when(tile == n_tiles - 1)
    def _():
        _run_copies("wait", src_of, dst_of, ssem, nrows, bases, region_rows)
        zbuf[...] = jnp.zeros_like(zbuf)
        for op in ("start", "wait"):
            for g in range(N_GROUPS):
                fill = fill_s[g]
                def fn(off, size, g=g, fill=fill, op=op):
                    b = pl.multiple_of(g * region_rows + fill + off, ROW_ALIGN)
                    cp = pltpu.make_async_copy(zbuf.at[pl.ds(0, size)], hs_ref.at[pl.ds(b, size)], zsem)
                    getattr(cp, op)()
                _for_each_chunk((-fill) & (MOE_TILE - 1), MOE_TILE // 2, fn)
        for g in range(N_GROUPS):
            fill_ref[g] = fill_s[g]


def _gather_prev_ffn(tile, n_tiles, segb_ref, segn_ref, ys_ref, q_ref, ybuf, gsem):
    region_rows = ys_ref.shape[0] // N_GROUPS

    def copies(op, t, slot):
        nrows = [segn_ref[t * N_GROUPS + g] for g in range(N_GROUPS)]
        bases = [segb_ref[t * N_GROUPS + g] for g in range(N_GROUPS)]
        src_of = lambda a, b, size: ys_ref.at[pl.ds(b, size)]
        dst_of = lambda a, b, size: ybuf.at[slot, pl.ds(a, size)]
        _run_copies(op, src_of, dst_of, gsem.at[slot], nrows, bases, region_rows)

    def issue(t, slot):
        ybuf[slot, SEQ_TILE:SORT_ROWS, :] = jnp.zeros((SORT_ROWS - SEQ_TILE, D_MODEL), BF16)
        copies("start", t, slot)

    @pl.when(tile == 0)
    def _():
        issue(tile, 0)

    @pl.when(tile + 1 < n_tiles)
    def _():
        issue(tile + 1, (tile + 1) % 2)

    slot = tile % 2
    copies("wait", tile, slot)
    p = _sort_matrix(q_ref[0])
    return lax.dot_general(p, ybuf[slot], (((0,), (0,)), ((), ())), preferred_element_type=F32)


def _load_weight_bf16(w_hbm, w_vmem, stage, sem):
    rows = stage.shape[1]
    n = w_vmem.shape[0] // rows
    copy = lambda c: pltpu.make_async_copy(w_hbm.at[pl.ds(c * rows, rows)], stage.at[c % 2],
                                           sem.at[c % 2])
    copy(0).start()
    for c in range(n):
        if c + 1 < n:
            copy(c + 1).start()
        copy(c).wait()
        w_vmem[c * rows:(c + 1) * rows, :] = stage[c % 2].astype(BF16)


def _weight_scratch(k, n):
    return [pltpu.VMEM((k, n), BF16), pltpu.VMEM((2, W_STAGE_ROWS, n), F32),
            pltpu.SemaphoreType.DMA((2,))]


def _hybrid_kernel(x_ref, mod_ref, win_hbm, wout_hbm, gmix_ref, gffn_ref, lng_ref, lnb_ref, sw_ref,
                   sb_ref, cw_ref, rw_ref, rb_ref,
                   x1_ref, hs_ref, q_ref, segb_ref, segn_ref, fill_ref,
                   pbuf, win_ref, win_stage, win_sem, wout_ref, wout_stage, wout_sem,
                   rowbuf, tri, zbuf, fill_s, pend_s, ssem, zsem):
    ts = x_ref.shape[1]
    tile = pl.program_id(0) * pl.num_programs(1) + pl.program_id(1)
    n_tiles = pl.num_programs(0) * pl.num_programs(1)

    @pl.when(tile == 0)
    def _():
        _load_weight_bf16(win_hbm.at[0], win_ref, win_stage, win_sem)
        _load_weight_bf16(wout_hbm.at[0], wout_ref, wout_stage, wout_sem)

    x = x_ref[0]
    mod = mod_ref[0]
    sh1, sc1, g1 = mod[0:1], mod[1:2], mod[2:3]
    h = _rms(x, gmix_ref[...]) * (1.0 + sc1) + sh1
    z = jnp.dot(h.astype(BF16), win_ref[...], preferred_element_type=F32)

    u = _gelu(z[:, :D_A])
    v = _gelu(z[:, D_A:2 * D_A])
    v = _layer_norm(v, lng_ref[...], lnb_ref[...]).astype(BF16)
    tril = (lax.broadcasted_iota(I32, (CHUNK, CHUNK), 0)
            >= lax.broadcasted_iota(I32, (CHUNK, CHUNK), 1))
    chunks = []
    for n in range(ts // CHUNK):
        heads = []
        for hd in range(A_HEADS):
            w = jnp.where(tril, sw_ref[hd], 0.0).astype(BF16)
            vb = v[n * CHUNK:(n + 1) * CHUNK, hd * A_HEAD_DIM:(hd + 1) * A_HEAD_DIM]
            heads.append(jnp.dot(w, vb, preferred_element_type=F32) + sb_ref[hd])
        chunks.append(jnp.concatenate(heads, axis=1))
    y_a = u * jnp.concatenate(chunks, axis=0)

    o = 2 * D_A
    bg, cg, hb = z[:, o:o + D_B], z[:, o + D_B:o + 2 * D_B], z[:, o + 2 * D_B:o + 3 * D_B]
    p = cg * hb

    @pl.when(pl.program_id(1) == 0)
    def _():
        pbuf[0:B_HALO, :] = jnp.zeros((B_HALO, D_B), F32)

    pbuf[B_HALO:B_HALO + ts, :] = p
    cw = cw_ref[...]
    conv = (cw[0:1] * pbuf[B_HALO - 2:B_HALO - 2 + ts, :]
            + cw[1:2] * pbuf[B_HALO - 1:B_HALO - 1 + ts, :]
            + cw[2:3] * p)
    pbuf[0:B_HALO, :] = p[ts - B_HALO:ts, :]
    y_b = bg * conv

    ycat = jnp.concatenate([y_a, y_b], axis=1).astype(BF16)
    y = jnp.dot(ycat, wout_ref[...], preferred_element_type=F32)
    x1 = x + g1 * y
    x1_ref[0] = x1
    _prenorm_route_scatter(x1, mod, tile, n_tiles, gffn_ref, rw_ref, rb_ref,
                           hs_ref, q_ref, segb_ref, segn_ref, fill_ref,
                           rowbuf, tri, zbuf, fill_s, pend_s, ssem, zsem)


def _causal_dw_conv(cbuf, w_ref, obuf, ts, col0, ncols):
    n_a = -(-C_CONV // SUBLANES)
    for t0 in range(0, ts, CONV_ROWS):
        for c0 in range(col0, col0 + ncols, LANES):
            cols = slice(c0, c0 + LANES)
            out = None
            for r in range(SUBLANES):
                part = None
                for a in range(n_a):
                    k = C_CONV - 1 - SUBLANES * a - r
                    if k < 0:
                        continue
                    r0 = t0 + C_HALO - SUBLANES * (a + 1)
                    term = w_ref[k:k + 1, cols] * cbuf[r0:r0 + CONV_ROWS + SUBLANES, cols]
                    part = term if part is None else part + term
                piece = part[SUBLANES - r:SUBLANES - r + CONV_ROWS]
                out = piece if out is None else out + piece
            obuf[t0:t0 + CONV_ROWS, cols] = out


def _conformer_kernel(psegb_ref, psegn_ref,
                      xp_ref, ys_ref, pq_ref, pmod_ref, mod_ref, w1_hbm, w2_hbm, gmix_ref, gffn_ref,
                      b1_ref, dww_ref, dwb_ref, lng_ref, lnb_ref, b2_ref, rw_ref, rb_ref,
                      x1_ref, hs_ref, q_ref, segb_ref, segn_ref, fill_ref,
                      cbuf, obuf, ybuf, gsem, w1_ref, w1_stage, w1_sem, w2_ref, w2_stage, w2_sem,
                      rowbuf, tri, zbuf, fill_s, pend_s, ssem, zsem):
    ts = xp_ref.shape[1]
    tile = pl.program_id(0) * pl.num_programs(1) + pl.program_id(1)
    n_tiles = pl.num_programs(0) * pl.num_programs(1)

    @pl.when(tile == 0)
    def _():
        _load_weight_bf16(w1_hbm.at[0], w1_ref, w1_stage, w1_sem)
        _load_weight_bf16(w2_hbm.at[0], w2_ref, w2_stage, w2_sem)

    y_prev = _gather_prev_ffn(tile, n_tiles, psegb_ref, psegn_ref, ys_ref, pq_ref, ybuf, gsem)
    x = xp_ref[0] + pmod_ref[0][5:6] * y_prev
    mod = mod_ref[0]
    sh1, sc1, g1 = mod[0:1], mod[1:2], mod[2:3]
    h = _rms(x, gmix_ref[...]) * (1.0 + sc1) + sh1
    hb = h.astype(BF16)

    @pl.when(pl.program_id(1) == 0)
    def _():
        cbuf[0:C_HALO, :] = jnp.zeros((C_HALO, D_C), F32)

    for c0 in range(0, D_C, GLU_COLS):
        ca, cg = slice(c0, c0 + GLU_COLS), slice(D_C + c0, D_C + c0 + GLU_COLS)
        za = jnp.dot(hb, w1_ref[:, ca], preferred_element_type=F32) + b1_ref[:, ca]
        zg = jnp.dot(hb, w1_ref[:, cg], preferred_element_type=F32) + b1_ref[:, cg]
        z = za * jax.nn.sigmoid(zg)
        cbuf[C_HALO:C_HALO + ts, ca] = z
        _causal_dw_conv(cbuf, dww_ref, obuf, ts, c0, GLU_COLS)
        cbuf[0:C_HALO, ca] = z[ts - C_HALO:ts, :]

    zc = _layer_norm(obuf[...] + dwb_ref[...], lng_ref[...], lnb_ref[...])
    zc = zc * jax.nn.sigmoid(zc)
    y = jnp.dot(zc.astype(BF16), w2_ref[...], preferred_element_type=F32) + b2_ref[...]
    x1 = x + g1 * y
    x1_ref[0] = x1
    _prenorm_route_scatter(x1, mod, tile, n_tiles, gffn_ref, rw_ref, rb_ref,
                           hs_ref, q_ref, segb_ref, segn_ref, fill_ref,
                           rowbuf, tri, zbuf, fill_s, pend_s, ssem, zsem)


def _final_kernel(psegb_ref, psegn_ref, xp_ref, ys_ref, pq_ref, pmod_ref, fg_ref, o_ref, ybuf, gsem):
    tile = pl.program_id(0) * pl.num_programs(1) + pl.program_id(1)
    n_tiles = pl.num_programs(0) * pl.num_programs(1)
    y_prev = _gather_prev_ffn(tile, n_tiles, psegb_ref, psegn_ref, ys_ref, pq_ref, ybuf, gsem)
    x = xp_ref[0] + pmod_ref[0][5:6] * y_prev
    o_ref[0] = _rms(x, fg_ref[...])


def _scatter_outputs(bsz, seq, d):
    t = bsz * seq
    nseq = seq // SEQ_TILE
    n_tiles = t // SEQ_TILE
    smem = pl.BlockSpec(memory_space=pltpu.SMEM)
    specs = [pl.BlockSpec((1, SEQ_TILE, d), lambda b, s, *_: (b, s, 0)),
             pl.BlockSpec(memory_space=pl.ANY),
             pl.BlockSpec((1, 1, SEQ_TILE), lambda b, s, *_: (b * nseq + s, 0, 0)),
             smem, smem, smem]
    shapes = [jax.ShapeDtypeStruct((bsz, seq, d), F32),
              jax.ShapeDtypeStruct((N_GROUPS * _region_rows(t), ROW_W), BF16),
              jax.ShapeDtypeStruct((n_tiles, 1, SEQ_TILE), I32),
              jax.ShapeDtypeStruct((n_tiles * N_GROUPS,), I32),
              jax.ShapeDtypeStruct((n_tiles * N_GROUPS,), I32),
              jax.ShapeDtypeStruct((N_GROUPS,), I32)]
    scratch = [pltpu.VMEM((SORT_ROWS, ROW_W), BF16),
               pltpu.VMEM((SEQ_TILE, SEQ_TILE), BF16),
               pltpu.VMEM((MOE_TILE // 2, ROW_W), BF16),
               pltpu.SMEM((N_GROUPS,), I32),
               pltpu.SMEM((2 * N_GROUPS,), I32),
               pltpu.SemaphoreType.DMA(()),
               pltpu.SemaphoreType.DMA(())]
    return specs, shapes, scratch


def _gather_inputs(bsz, seq, d):
    nseq = seq // SEQ_TILE
    specs = [pl.BlockSpec((1, SEQ_TILE, d), lambda b, s, *_: (b, s, 0)),
             pl.BlockSpec(memory_space=pl.ANY),
             pl.BlockSpec((1, 1, SEQ_TILE), lambda b, s, *_: (b * nseq + s, 0, 0)),
             pl.BlockSpec((1, N_MOD, d), lambda b, s, *_: (b, 0, 0))]
    scratch = [pltpu.VMEM((2, SORT_ROWS, d), BF16), pltpu.SemaphoreType.DMA((2,))]
    return specs, scratch


def _const_specs(consts):
    return [pl.BlockSpec(a.shape, lambda b, s, *_, nd=a.ndim: (0,) * nd) for a in consts]


_PARAMS = pltpu.CompilerParams(dimension_semantics=("arbitrary", "arbitrary"),
                               vmem_limit_bytes=VMEM_LIMIT)


def _hybrid_call(x, mod_l, w_in, w_out, consts):
    bsz, seq, d = x.shape
    out_specs, out_shapes, sc_scratch = _scatter_outputs(bsz, seq, d)
    hbm = pl.BlockSpec(memory_space=pl.ANY)
    return pl.pallas_call(
        _hybrid_kernel,
        grid=(bsz, seq // SEQ_TILE),
        in_specs=[pl.BlockSpec((1, SEQ_TILE, d), lambda b, s: (b, s, 0)),
                  pl.BlockSpec((1, N_MOD, d), lambda b, s: (b, 0, 0)), hbm, hbm]
                 + _const_specs(consts),
        out_specs=out_specs,
        out_shape=out_shapes,
        scratch_shapes=[pltpu.VMEM((SEQ_TILE + B_HALO, D_B), F32)]
                       + _weight_scratch(*w_in.shape[1:]) + _weight_scratch(*w_out.shape[1:])
                       + sc_scratch,
        compiler_params=_PARAMS,
        name="mixer_hybrid",
    )(x, mod_l, w_in, w_out, *consts)


def _conformer_call(prev, mod_prev, mod_l, w1, w2, consts):
    x1p, ysp, qp, segbp, segnp = prev
    bsz, seq, d = x1p.shape
    g_specs, g_scratch = _gather_inputs(bsz, seq, d)
    out_specs, out_shapes, sc_scratch = _scatter_outputs(bsz, seq, d)
    hbm = pl.BlockSpec(memory_space=pl.ANY)
    grid_spec = pltpu.PrefetchScalarGridSpec(
        num_scalar_prefetch=2,
        grid=(bsz, seq // SEQ_TILE),
        in_specs=g_specs + [pl.BlockSpec((1, N_MOD, d), lambda b, s, *_: (b, 0, 0)), hbm, hbm]
                 + _const_specs(consts),
        out_specs=out_specs,
        scratch_shapes=[pltpu.VMEM((SEQ_TILE + C_HALO, D_C), F32), pltpu.VMEM((SEQ_TILE, D_C), F32)]
                       + g_scratch + _weight_scratch(*w1.shape[1:]) + _weight_scratch(*w2.shape[1:])
                       + sc_scratch,
    )
    return pl.pallas_call(
        _conformer_kernel, grid_spec=grid_spec, out_shape=out_shapes,
        compiler_params=_PARAMS, name="mixer_conformer",
    )(segbp, segnp, x1p, ysp, qp, mod_prev, mod_l, w1, w2, *consts)


def _final_call(prev, mod_prev, final_g):
    x1p, ysp, qp, segbp, segnp = prev
    bsz, seq, d = x1p.shape
    g_specs, g_scratch = _gather_inputs(bsz, seq, d)
    grid_spec = pltpu.PrefetchScalarGridSpec(
        num_scalar_prefetch=2,
        grid=(bsz, seq // SEQ_TILE),
        in_specs=g_specs + [pl.BlockSpec((1, d), lambda b, s, *_: (0, 0))],
        out_specs=pl.BlockSpec((1, SEQ_TILE, d), lambda b, s, *_: (b, s, 0)),
        scratch_shapes=g_scratch,
    )
    return pl.pallas_call(
        _final_kernel, grid_spec=grid_spec,
        out_shape=jax.ShapeDtypeStruct((bsz, seq, d), F32),
        compiler_params=_PARAMS, name="final_norm",
    )(segbp, segnp, x1p, ysp, qp, mod_prev, final_g.reshape(1, d))


def _moe_kernel(blk_ref, grp_ref, valid_ref, fresh_ref, hs_ref, wg_ref, wu_ref, wd_ref, ys_ref,
                wg_bf, wu_bf, wd_bf):
    i = pl.program_id(0)

    @pl.when(fresh_ref[i] != 0)
    def _():
        wg_bf[...] = wg_ref[0].astype(BF16)
        wu_bf[...] = wu_ref[0].astype(BF16)
        wd_bf[...] = wd_ref[0].astype(BF16)

    @pl.when(valid_ref[i] != 0)
    def _():
        rows = hs_ref[...]
        h = rows[:, :D_MODEL]
        m = rows[:, D_MODEL:].astype(F32)
        ne = EXPERTS_PER_GROUP
        y = None
        for j in range(ne):
            gate = m[:, j:j + 1] + (m[:, ne + j:ne + j + 1] + m[:, 2 * ne + j:2 * ne + j + 1])
            a = jnp.dot(h, wg_bf[j], preferred_element_type=F32)
            a = a * jax.nn.sigmoid(a) * jnp.dot(h, wu_bf[j], preferred_element_type=F32)
            yj = jnp.dot((a * gate).astype(BF16), wd_bf[j], preferred_element_type=F32)
            y = yj if y is None else y + yj
        ys_ref[...] = y.astype(BF16)


def _moe_items(fill, region_rows, n_items):
    tiles = (fill + (MOE_TILE - 1)) // MOE_TILE
    ends = jnp.cumsum(tiles)
    i = jnp.minimum(jnp.arange(n_items, dtype=I32), jnp.maximum(ends[-1] - 1, 0))
    grp = jnp.sum((i[:, None] >= ends[None, :]).astype(I32), axis=1)
    blk = grp * (region_rows // MOE_TILE) + i - (ends - tiles)[grp]
    valid = (jnp.arange(n_items, dtype=I32) < ends[-1]).astype(I32)
    fresh = jnp.concatenate([jnp.ones((1,), I32), (grp[1:] != grp[:-1]).astype(I32)])
    return blk.astype(I32), grp.astype(I32), valid, fresh


def _moe_call(hs, fill, layer, wg, wu, wd, t):
    region_rows = hs.shape[0] // N_GROUPS
    d = D_MODEL
    ne = EXPERTS_PER_GROUP
    n_items = -(-(t + (t // SEQ_TILE) * N_GROUPS * (ROW_ALIGN - 1)) // MOE_TILE) + N_GROUPS
    items = _moe_items(fill, region_rows, n_items)
    w_idx = lambda i, blk, grp, valid, fresh: (layer * N_GROUPS + grp[i], 0, 0, 0)
    row_idx = lambda i, blk, grp, valid, fresh: (blk[i], 0)
    grid_spec = pltpu.PrefetchScalarGridSpec(
        num_scalar_prefetch=len(items),
        grid=(n_items,),
        in_specs=[pl.BlockSpec((MOE_TILE, ROW_W), row_idx),
                  pl.BlockSpec((1, ne, d, D_EXPERT), w_idx),
                  pl.BlockSpec((1, ne, d, D_EXPERT), w_idx),
                  pl.BlockSpec((1, ne, D_EXPERT, d), w_idx)],
        out_specs=pl.BlockSpec((MOE_TILE, d), row_idx),
        scratch_shapes=[pltpu.VMEM((ne, d, D_EXPERT), BF16), pltpu.VMEM((ne, d, D_EXPERT), BF16),
                        pltpu.VMEM((ne, D_EXPERT, d), BF16)],
    )
    return pl.pallas_call(
        _moe_kernel, grid_spec=grid_spec,
        out_shape=jax.ShapeDtypeStruct((hs.shape[0], d), BF16),
        compiler_params=pltpu.CompilerParams(dimension_semantics=("arbitrary",),
                                             vmem_limit_bytes=VMEM_LIMIT),
        name="moe",
    )(*items, hs, wg, wu, wd)


def kernel(x, c, ada_w, ada_b, norm_mix_g, norm_ffn_g, hyb_w_in, sgu_ln_g, sgu_ln_b, sgu_w, sgu_b,
           sconv_w, hyb_w_out, conf_w1, conf_b1, conf_dw_w, conf_dw_b, conf_ln_g, conf_ln_b,
           conf_w2, conf_b2, router_w, router_bias, moe_w_gate, moe_w_up, moe_w_down, final_norm_g):
    depth = ada_w.shape[0]
    bsz, seq, d = x.shape
    t = bsz * seq
    assert depth == 2 and seq % SEQ_TILE == 0
    mod = _ada_mod(c, ada_w, ada_b).reshape(depth, bsz, N_MOD, d)

    perm = jnp.arange(N_EXPERTS).reshape(N_GROUPS, EXPERTS_PER_GROUP).T.reshape(-1)
    rw_t = router_w.T[perm]
    rb_t = router_bias[perm].reshape(N_EXPERTS, 1)

    ne = EXPERTS_PER_GROUP
    wg = moe_w_gate.reshape(depth * N_GROUPS, ne, d, D_EXPERT)
    wu = moe_w_up.reshape(depth * N_GROUPS, ne, d, D_EXPERT)
    wd = moe_w_down.reshape(depth * N_GROUPS, ne, D_EXPERT, d)

    row = lambda a: a.reshape(1, -1)
    consts0 = [row(norm_mix_g[0]), row(norm_ffn_g[0]), row(sgu_ln_g[0]), row(sgu_ln_b[0]), sgu_w[0],
               sgu_b[0].reshape(A_HEADS, CHUNK, 1), sconv_w[0], rw_t, rb_t]
    x1, hs, q, segb, segn, fill = _hybrid_call(x, mod[0], hyb_w_in, hyb_w_out, consts0)
    ys = _moe_call(hs, fill, 0, wg, wu, wd, t)

    consts1 = [row(norm_mix_g[1]), row(norm_ffn_g[1]), row(conf_b1[0]), conf_dw_w[0],
               row(conf_dw_b[0]), row(conf_ln_g[0]), row(conf_ln_b[0]), row(conf_b2[0]), rw_t, rb_t]
    x1, hs, q1, segb1, segn1, fill1 = _conformer_call((x1, ys, q, segb, segn), mod[0], mod[1],
                                                      conf_w1, conf_w2, consts1)
    ys = _moe_call(hs, fill1, 1, wg, wu, wd, t)
    return _final_call((x1, ys, q1, segb1, segn1), mod[1], final_norm_g)
```

```python
import functools

import jax
import jax.numpy as jnp
from jax import lax
from jax.experimental import pallas as pl
from jax.experimental.pallas import tpu as pltpu

D_MODEL = 1024
CHUNK = 128
A_HEADS = 4
A_HEAD_DIM = 128
D_A = A_HEADS * A_HEAD_DIM
D_B = 512
B_CONV = 3
D_IN_HYB = 2 * D_A + 3 * D_B
D_C = D_MODEL
C_CONV = 31
N_EXPERTS = 16
N_GROUPS = 4
EXPERTS_PER_GROUP = N_EXPERTS // N_GROUPS
D_EXPERT = 256
N_MOD = 6
EPS = 1e-6

SUBLANES = 8
LANES = 128
ROW_ALIGN = 16

SEQ_TILE = 512
MOE_TILE = 512
MOD_TILE = 1024
W_STAGE_ROWS = 128
C_HALO = 32
B_HALO = 8
CONV_ROWS = 128
GLU_COLS = 256
SORT_ROWS = SEQ_TILE + N_GROUPS * ROW_ALIGN
ROW_W = D_MODEL + LANES
VMEM_LIMIT = 56 * 1024 * 1024

F32 = jnp.float32
BF16 = jnp.bfloat16
I32 = jnp.int32


def _round_up(n, m):
    return -(-n // m) * m


def _region_rows(t):
    return _round_up(t + (t // SEQ_TILE) * ROW_ALIGN, MOE_TILE)


def _split_bf16(a):
    hi = a.astype(BF16)
    lo = (a - hi.astype(F32)).astype(BF16)
    return hi, lo


def _dot3(a, b):
    ah, al = _split_bf16(a)
    bh, bl = _split_bf16(b)
    d = functools.partial(jnp.dot, preferred_element_type=F32)
    return d(ah, bh) + (d(ah, bl) + d(al, bh))


def _rms(x, g):
    return x * lax.rsqrt(jnp.mean(x * x, axis=-1, keepdims=True) + EPS) * g


def _gelu(x):
    return 0.5 * x * (1.0 + lax.erf(x * (2.0 ** -0.5)))


def _layer_norm(x, g, b):
    mu = jnp.mean(x, axis=-1, keepdims=True)
    xc = x - mu
    var = jnp.mean(xc * xc, axis=-1, keepdims=True)
    return xc * lax.rsqrt(var + EPS) * g + b


def _mod_kernel(c_ref, w_ref, b_ref, o_ref):
    c = c_ref[...]
    c_act = c * jax.nn.sigmoid(c)
    o_ref[0] = _dot3(c_act, w_ref[0]) + b_ref[0]


def _ada_mod(c, ada_w, ada_b):
    depth, d, n = ada_w.shape
    bsz = c.shape[0]
    return pl.pallas_call(
        _mod_kernel,
        grid=(depth, n // MOD_TILE),
        in_specs=[
            pl.BlockSpec((bsz, d), lambda l, j: (0, 0)),
            pl.BlockSpec((1, d, MOD_TILE), lambda l, j: (l, 0, j)),
            pl.BlockSpec((1, 1, MOD_TILE), lambda l, j: (l, 0, j)),
        ],
        out_specs=pl.BlockSpec((1, bsz, MOD_TILE), lambda l, j: (l, 0, j)),
        out_shape=jax.ShapeDtypeStruct((depth, bsz, n), F32),
        compiler_params=pltpu.CompilerParams(
            dimension_semantics=("arbitrary", "arbitrary"), vmem_limit_bytes=VMEM_LIMIT),
        name="ada_mod",
    )(c, ada_w, ada_b.reshape(depth, 1, n))


def _route(h2, rw_ref, rb_ref):
    logits = lax.dot_general(rw_ref[...], h2, (((1,), (1,)), ((), ())),
                             precision=lax.Precision.HIGHEST,
                             preferred_element_type=F32)
    scores = jax.nn.sigmoid(logits)
    biased = scores + rb_ref[...]
    ng, ne = N_GROUPS, EXPERTS_PER_GROUP
    b = [biased[j * ng:(j + 1) * ng] for j in range(ne)]
    s = [scores[j * ng:(j + 1) * ng] for j in range(ne)]
    gscore = None
    for i in range(ne):
        for j in range(i + 1, ne):
            pair = b[i] + b[j]
            gscore = pair if gscore is None else jnp.maximum(gscore, pair)
    gid = lax.broadcasted_iota(I32, gscore.shape, 0)
    gmax = jnp.max(gscore, axis=0, keepdims=True)
    best = jnp.min(jnp.where(gscore == gmax, gid, ng), axis=0, keepdims=True)
    onb = gid == best
    cb = [jnp.sum(jnp.where(onb, b[j], 0.0), axis=0, keepdims=True) for j in range(ne)]
    cs = [jnp.sum(jnp.where(onb, s[j], 0.0), axis=0, keepdims=True) for j in range(ne)]
    sel = []
    for j in range(ne):
        rank = jnp.zeros_like(best)
        for i in range(ne):
            if i == j:
                continue
            ahead = (cb[i] >= cb[j]) if i < j else (cb[i] > cb[j])
            rank = rank + ahead.astype(I32)
        sel.append(rank < 2)
    wsum = sum(jnp.where(sel[j], cs[j], 0.0) for j in range(ne))
    gates = jnp.concatenate([jnp.where(sel[j], cs[j], 0.0) / wsum for j in range(ne)], axis=0)
    return gates, jnp.where(onb, 1.0, 0.0)


def _for_each_chunk(nrows, max_rows, fn):
    size = max_rows
    while size >= ROW_ALIGN:
        off = nrows & ~(2 * size - 1)

        @pl.when((nrows & size) != 0)
        def _(off=off, size=size):
            fn(pl.multiple_of(off, ROW_ALIGN), size)

        size //= 2


def _sort_matrix(q):
    rows = lax.broadcasted_iota(I32, (SORT_ROWS, q.shape[1]), 0)
    return jnp.where(rows == q, 1.0, 0.0).astype(BF16)


def _run_copies(op, src_of, dst_of, sem, nrows, bases, region_rows):
    tile_off = 0
    for g in range(N_GROUPS):
        def fn(off, size, g=g, tile_off=tile_off):
            a = pl.multiple_of(tile_off + off, ROW_ALIGN)
            b = pl.multiple_of(g * region_rows + bases[g] + off, ROW_ALIGN)
            cp = pltpu.make_async_copy(src_of(a, b, size), dst_of(a, b, size), sem)
            getattr(cp, op)()
        _for_each_chunk(nrows[g], SEQ_TILE, fn)
        tile_off = tile_off + nrows[g]


class _ScatterRefs:
    def __init__(self, gffn_ref, rw_ref, rb_ref, hs_ref, q_ref, segb_ref, segn_ref, fill_ref,
                 h2buf, rowbuf, tri, zbuf, fill_s, pend_s, ssem, zsem):
        self.__dict__.update(locals())
        self.region_rows = hs_ref.shape[0] // N_GROUPS
        self.src_of = lambda a, b, size: rowbuf.at[pl.ds(a, size)]
        self.dst_of = lambda a, b, size: hs_ref.at[pl.ds(b, size)]


def _sc_init(sc, tile):
    @pl.when(tile == 0)
    def _():
        for g in range(N_GROUPS):
            sc.fill_s[g] = 0
            sc.pend_s[g] = 0
            sc.pend_s[N_GROUPS + g] = 0
        ts = sc.tri.shape[0]
        k = lax.broadcasted_iota(I32, (ts, ts), 0)
        t = lax.broadcasted_iota(I32, (ts, ts), 1)
        sc.tri[...] = jnp.where(k <= t, 1.0, 0.0).astype(BF16)
        sc.h2buf[...] = jnp.zeros_like(sc.h2buf)


def _sc_pending(sc, op):
    _run_copies(op, sc.src_of, sc.dst_of, sc.ssem, [sc.pend_s[g] for g in range(N_GROUPS)],
                [sc.pend_s[N_GROUPS + g] for g in range(N_GROUPS)], sc.region_rows)


def _prenorm(x1, mod, gffn_ref):
    sh2, sc2 = mod[3:4], mod[4:5]
    return _rms(x1, gffn_ref[...]) * (1.0 + sc2) + sh2


def _sc_route(sc, st):
    st["h2"] = sc.h2buf[...]
    st["gates"], st["onehot"] = _route(st["h2"], sc.rw_ref, sc.rb_ref)


def _sc_rank(sc, st, active):
    onehot = st["onehot"]
    ts = onehot.shape[1]
    oh8 = jnp.concatenate([onehot, jnp.zeros((SUBLANES - N_GROUPS, ts), F32)], axis=0).astype(BF16)
    csum = jnp.dot(oh8, sc.tri[...], preferred_element_type=F32)[:N_GROUPS]
    counts = [csum[g, ts - 1].astype(I32) * active for g in range(N_GROUPS)]
    st["nrows"] = [(c + (ROW_ALIGN - 1)) & ~(ROW_ALIGN - 1) for c in counts]
    q = jnp.zeros((1, ts), F32)
    tile_off = 0
    for g in range(N_GROUPS):
        q = q + onehot[g:g + 1] * (csum[g:g + 1] - 1.0 + tile_off.astype(F32)
                                   if g else csum[g:g + 1] - 1.0)
        tile_off = tile_off + st["nrows"][g]
    st["q"] = q.astype(I32)


def _sc_sort(sc, st):
    gates = st["gates"]
    ts = gates.shape[1]
    hi = gates.astype(BF16).astype(F32)
    mid = (gates - hi).astype(BF16).astype(F32)
    lo = (gates - hi - mid).astype(BF16).astype(F32)
    meta_t = jnp.concatenate(
        [hi, mid, lo, jnp.zeros((LANES - 3 * EXPERTS_PER_GROUP, ts), F32)], axis=0)
    rows = jnp.concatenate([st["h2"].astype(BF16), meta_t.T.astype(BF16)], axis=1)
    sorted_rows = jnp.dot(_sort_matrix(st["q"]), rows, preferred_element_type=F32)
    sc.rowbuf[...] = sorted_rows.astype(BF16)


def _sc_start(sc, st, t_idx):
    sc.q_ref[t_idx] = st["q"]
    nrows = st["nrows"]
    for g in range(N_GROUPS):
        base = sc.fill_s[g]
        sc.pend_s[g] = nrows[g]
        sc.pend_s[N_GROUPS + g] = base
        sc.segn_ref[t_idx * N_GROUPS + g] = nrows[g]
        sc.segb_ref[t_idx * N_GROUPS + g] = base
        sc.fill_s[g] = base + nrows[g]
    _sc_pending(sc, "start")


def _sc_finish(sc, tile, n_tiles):
    @pl.when(tile == n_tiles - 1)
    def _():
        _sc_pending(sc, "wait")
        st = {}
        _sc_route(sc, st)
        _sc_rank(sc, st, 1)
        _sc_sort(sc, st)
        _sc_start(sc, st, tile)
        _sc_pending(sc, "wait")
        sc.zbuf[...] = jnp.zeros_like(sc.zbuf)
        for op in ("start", "wait"):
            for g in range(N_GROUPS):
                fill = sc.fill_s[g]
                def fn(off, size, g=g, fill=fill, op=op):
                    b = pl.multiple_of(g * sc.region_rows + fill + off, ROW_ALIGN)
                    cp = pltpu.make_async_copy(sc.zbuf.at[pl.ds(0, size)],
                                               sc.hs_ref.at[pl.ds(b, size)], sc.zsem)
                    getattr(cp, op)()
                _for_each_chunk((-fill) & (MOE_TILE - 1), MOE_TILE // 2, fn)
        for g in range(N_GROUPS):
            sc.fill_ref[g] = sc.fill_s[g]


def _gather_prev_ffn(tile, n_tiles, segb_ref, segn_ref, ys_ref, q_ref, ybuf, gsem):
    region_rows = ys_ref.shape[0] // N_GROUPS

    def copies(op, t, slot):
        nrows = [segn_ref[t * N_GROUPS + g] for g in range(N_GROUPS)]
        bases = [segb_ref[t * N_GROUPS + g] for g in range(N_GROUPS)]
        src_of = lambda a, b, size: ys_ref.at[pl.ds(b, size)]
        dst_of = lambda a, b, size: ybuf.at[slot, pl.ds(a, size)]
        _run_copies(op, src_of, dst_of, gsem.at[slot], nrows, bases, region_rows)

    def issue(t, slot):
        ybuf[slot, SEQ_TILE:SORT_ROWS, :] = jnp.zeros((SORT_ROWS - SEQ_TILE, D_MODEL), BF16)
        copies("start", t, slot)

    @pl.when(tile == 0)
    def _():
        issue(tile, 0)

    @pl.when(tile + 1 < n_tiles)
    def _():
        issue(tile + 1, (tile + 1) % 2)

    slot = tile % 2
    copies("wait", tile, slot)
    p = _sort_matrix(q_ref[0])
    return lax.dot_general(p, ybuf[slot], (((0,), (0,)), ((), ())), preferred_element_type=F32)


def _load_weight_bf16(w_hbm, w_vmem, stage, sem):
    rows = stage.shape[1]
    n = w_vmem.shape[0] // rows
    copy = lambda c: pltpu.make_async_copy(w_hbm.at[pl.ds(c * rows, rows)], stage.at[c % 2],
                                           sem.at[c % 2])
    copy(0).start()
    for c in range(n):
        if c + 1 < n:
            copy(c + 1).start()
        copy(c).wait()
        w_vmem[c * rows:(c + 1) * rows, :] = stage[c % 2].astype(BF16)


def _weight_scratch(k, n):
    return [pltpu.VMEM((k, n), BF16), pltpu.VMEM((2, W_STAGE_ROWS, n), F32),
            pltpu.SemaphoreType.DMA((2,))]


def _hybrid_kernel(x_ref, mod_ref, win_hbm, wout_hbm, gmix_ref, gffn_ref, lng_ref, lnb_ref, sw_ref,
                   sb_ref, cw_ref, rw_ref, rb_ref,
                   x1_ref, hs_ref, q_ref, segb_ref, segn_ref, fill_ref,
                   pbuf, win_ref, win_stage, win_sem, wout_ref, wout_stage, wout_sem,
                   h2buf, rowbuf, tri, zbuf, fill_s, pend_s, ssem, zsem):
    ts = x_ref.shape[1]
    tile = pl.program_id(0) * pl.num_programs(1) + pl.program_id(1)
    n_tiles = pl.num_programs(0) * pl.num_programs(1)
    sc = _ScatterRefs(gffn_ref, rw_ref, rb_ref, hs_ref, q_ref, segb_ref, segn_ref, fill_ref,
                      h2buf, rowbuf, tri, zbuf, fill_s, pend_s, ssem, zsem)

    @pl.when(tile == 0)
    def _():
        _load_weight_bf16(win_hbm.at[0], win_ref, win_stage, win_sem)
        _load_weight_bf16(wout_hbm.at[0], wout_ref, wout_stage, wout_sem)

    @pl.when(pl.program_id(1) == 0)
    def _():
        pbuf[0:B_HALO, :] = jnp.zeros((B_HALO, D_B), F32)

    _sc_init(sc, tile)
    _sc_pending(sc, "wait")

    prev = {}
    active = (tile > 0).astype(I32)

    x = x_ref[0]
    mod = mod_ref[0]
    sh1, sc1, g1 = mod[0:1], mod[1:2], mod[2:3]
    h = _rms(x, gmix_ref[...]) * (1.0 + sc1) + sh1
    z = jnp.dot(h.astype(BF16), win_ref[...], preferred_element_type=F32)
    _sc_route(sc, prev)

    u = _gelu(z[:, :D_A])
    v = _gelu(z[:, D_A:2 * D_A])
    v = _layer_norm(v, lng_ref[...], lnb_ref[...]).astype(BF16)
    tril = (lax.broadcasted_iota(I32, (CHUNK, CHUNK), 0)
            >= lax.broadcasted_iota(I32, (CHUNK, CHUNK), 1))
    chunks = []
    for n in range(ts // CHUNK):
        heads = []
        for hd in range(A_HEADS):
            w = jnp.where(tril, sw_ref[hd], 0.0).astype(BF16)
            vb = v[n * CHUNK:(n + 1) * CHUNK, hd * A_HEAD_DIM:(hd + 1) * A_HEAD_DIM]
            heads.append(jnp.dot(w, vb, preferred_element_type=F32) + sb_ref[hd])
        chunks.append(jnp.concatenate(heads, axis=1))
    y_a = u * jnp.concatenate(chunks, axis=0)
    _sc_rank(sc, prev, active)

    o = 2 * D_A
    bg, cg, hb = z[:, o:o + D_B], z[:, o + D_B:o + 2 * D_B], z[:, o + 2 * D_B:o + 3 * D_B]
    p = cg * hb
    pbuf[B_HALO:B_HALO + ts, :] = p
    cw = cw_ref[...]
    conv = (cw[0:1] * pbuf[B_HALO - 2:B_HALO - 2 + ts, :]
            + cw[1:2] * pbuf[B_HALO - 1:B_HALO - 1 + ts, :]
            + cw[2:3] * p)
    pbuf[0:B_HALO, :] = p[ts - B_HALO:ts, :]
    y_b = bg * conv
    _sc_sort(sc, prev)

    ycat = jnp.concatenate([y_a, y_b], axis=1).astype(BF16)
    y = jnp.dot(ycat, wout_ref[...], preferred_element_type=F32)
    x1 = x + g1 * y
    x1_ref[0] = x1
    h2buf[...] = _prenorm(x1, mod, gffn_ref)
    _sc_start(sc, prev, jnp.maximum(tile - 1, 0))
    _sc_finish(sc, tile, n_tiles)


def _causal_dw_conv(cbuf, w_ref, obuf, row0, nrows, col0, ncols):
    n_a = -(-C_CONV // SUBLANES)
    for t0 in range(row0, row0 + nrows, CONV_ROWS):
        for c0 in range(col0, col0 + ncols, LANES):
            cols = slice(c0, c0 + LANES)
            out = None
            for r in range(SUBLANES):
                part = None
                for a in range(n_a):
                    k = C_CONV - 1 - SUBLANES * a - r
                    if k < 0:
                        continue
                    r0 = t0 + C_HALO - SUBLANES * (a + 1)
                    term = w_ref[k:k + 1, cols] * cbuf[r0:r0 + CONV_ROWS + SUBLANES, cols]
                    part = term if part is None else part + term
                piece = part[SUBLANES - r:SUBLANES - r + CONV_ROWS]
                out = piece if out is None else out + piece
            obuf[t0:t0 + CONV_ROWS, cols] = out


def _conformer_kernel(psegb_ref, psegn_ref,
                      xp_ref, ys_ref, pq_ref, pmod_ref, mod_ref, w1_hbm, w2_hbm, gmix_ref, gffn_ref,
                      b1_ref, dww_ref, dwb_ref, lng_ref, lnb_ref, b2_ref, rw_ref, rb_ref,
                      x1_ref, hs_ref, q_ref, segb_ref, segn_ref, fill_ref,
                      cbuf, obuf, ybuf, gsem, w1_ref, w1_stage, w1_sem, w2_ref, w2_stage, w2_sem,
                      h2buf, rowbuf, tri, zbuf, fill_s, pend_s, ssem, zsem):
    ts = xp_ref.shape[1]
    tile = pl.program_id(0) * pl.num_programs(1) + pl.program_id(1)
    n_tiles = pl.num_programs(0) * pl.num_programs(1)
    sc = _ScatterRefs(gffn_ref, rw_ref, rb_ref, hs_ref, q_ref, segb_ref, segn_ref, fill_ref,
                      h2buf, rowbuf, tri, zbuf, fill_s, pend_s, ssem, zsem)

    @pl.when(tile == 0)
    def _():
        _load_weight_bf16(w1_hbm.at[0], w1_ref, w1_stage, w1_sem)
        _load_weight_bf16(w2_hbm.at[0], w2_ref, w2_stage, w2_sem)

    @pl.when(pl.program_id(1) == 0)
    def _():
        cbuf[0:C_HALO, :] = jnp.zeros((C_HALO, D_C), F32)

    _sc_init(sc, tile)
    _sc_pending(sc, "wait")
    y_prev = _gather_prev_ffn(tile, n_tiles, psegb_ref, psegn_ref, ys_ref, pq_ref, ybuf, gsem)

    prev = {}
    active = (tile > 0).astype(I32)
    deferred = [lambda: _sc_route(sc, prev), lambda: _sc_rank(sc, prev, active),
                lambda: _sc_sort(sc, prev)]

    x = xp_ref[0] + pmod_ref[0][5:6] * y_prev
    mod = mod_ref[0]
    sh1, sc1, g1 = mod[0:1], mod[1:2], mod[2:3]
    h = _rms(x, gmix_ref[...]) * (1.0 + sc1) + sh1
    hb = h.astype(BF16)
    for k, c0 in enumerate(range(0, D_C, GLU_COLS)):
        ca, cg = slice(c0, c0 + GLU_COLS), slice(D_C + c0, D_C + c0 + GLU_COLS)
        za = jnp.dot(hb, w1_ref[:, ca], preferred_element_type=F32) + b1_ref[:, ca]
        zg = jnp.dot(hb, w1_ref[:, cg], preferred_element_type=F32) + b1_ref[:, cg]
        cbuf[C_HALO:C_HALO + ts, ca] = za * jax.nn.sigmoid(zg)
        if k < len(deferred):
            deferred[k]()
        _causal_dw_conv(cbuf, dww_ref, obuf, 0, ts, c0, GLU_COLS)
    cbuf[0:C_HALO, :] = cbuf[ts:ts + C_HALO, :]

    zc = _layer_norm(obuf[...] + dwb_ref[...], lng_ref[...], lnb_ref[...])
    zc = zc * jax.nn.sigmoid(zc)
    y = jnp.dot(zc.astype(BF16), w2_ref[...], preferred_element_type=F32) + b2_ref[...]
    x1 = x + g1 * y
    x1_ref[0] = x1
    h2buf[...] = _prenorm(x1, mod, gffn_ref)
    _sc_start(sc, prev, jnp.maximum(tile - 1, 0))
    _sc_finish(sc, tile, n_tiles)


def _final_kernel(psegb_ref, psegn_ref, xp_ref, ys_ref, pq_ref, pmod_ref, fg_ref, o_ref, ybuf, gsem):
    tile = pl.program_id(0) * pl.num_programs(1) + pl.program_id(1)
    n_tiles = pl.num_programs(0) * pl.num_programs(1)
    y_prev = _gather_prev_ffn(tile, n_tiles, psegb_ref, psegn_ref, ys_ref, pq_ref, ybuf, gsem)
    x = xp_ref[0] + pmod_ref[0][5:6] * y_prev
    o_ref[0] = _rms(x, fg_ref[...])


def _scatter_outputs(bsz, seq, d):
    t = bsz * seq
    n_tiles = t // SEQ_TILE
    smem = pl.BlockSpec(memory_space=pltpu.SMEM)
    specs = [pl.BlockSpec((1, SEQ_TILE, d), lambda b, s, *_: (b, s, 0)),
             pl.BlockSpec(memory_space=pl.ANY),
             pl.BlockSpec((n_tiles, 1, SEQ_TILE), lambda b, s, *_: (0, 0, 0)),
             smem, smem, smem]
    shapes = [jax.ShapeDtypeStruct((bsz, seq, d), F32),
              jax.ShapeDtypeStruct((N_GROUPS * _region_rows(t), ROW_W), BF16),
              jax.ShapeDtypeStruct((n_tiles, 1, SEQ_TILE), I32),
              jax.ShapeDtypeStruct((n_tiles * N_GROUPS,), I32),
              jax.ShapeDtypeStruct((n_tiles * N_GROUPS,), I32),
              jax.ShapeDtypeStruct((N_GROUPS,), I32)]
    scratch = [pltpu.VMEM((SEQ_TILE, d), F32),
               pltpu.VMEM((SORT_ROWS, ROW_W), BF16),
               pltpu.VMEM((SEQ_TILE, SEQ_TILE), BF16),
               pltpu.VMEM((MOE_TILE // 2, ROW_W), BF16),
               pltpu.SMEM((N_GROUPS,), I32),
               pltpu.SMEM((2 * N_GROUPS,), I32),
               pltpu.SemaphoreType.DMA(()),
               pltpu.SemaphoreType.DMA(())]
    return specs, shapes, scratch


def _gather_inputs(bsz, seq, d):
    nseq = seq // SEQ_TILE
    specs = [pl.BlockSpec((1, SEQ_TILE, d), lambda b, s, *_: (b, s, 0)),
             pl.BlockSpec(memory_space=pl.ANY),
             pl.BlockSpec((1, 1, SEQ_TILE), lambda b, s, *_: (b * nseq + s, 0, 0)),
             pl.BlockSpec((1, N_MOD, d), lambda b, s, *_: (b, 0, 0))]
    scratch = [pltpu.VMEM((2, SORT_ROWS, d), BF16), pltpu.SemaphoreType.DMA((2,))]
    return specs, scratch


def _const_specs(consts):
    return [pl.BlockSpec(a.shape, lambda b, s, *_, nd=a.ndim: (0,) * nd) for a in consts]


_PARAMS = pltpu.CompilerParams(dimension_semantics=("arbitrary", "arbitrary"),
                               vmem_limit_bytes=VMEM_LIMIT)


def _hybrid_call(x, mod_l, w_in, w_out, consts):
    bsz, seq, d = x.shape
    out_specs, out_shapes, sc_scratch = _scatter_outputs(bsz, seq, d)
    hbm = pl.BlockSpec(memory_space=pl.ANY)
    return pl.pallas_call(
        _hybrid_kernel,
        grid=(bsz, seq // SEQ_TILE),
        in_specs=[pl.BlockSpec((1, SEQ_TILE, d), lambda b, s: (b, s, 0)),
                  pl.BlockSpec((1, N_MOD, d), lambda b, s: (b, 0, 0)), hbm, hbm]
                 + _const_specs(consts),
        out_specs=out_specs,
        out_shape=out_shapes,
        scratch_shapes=[pltpu.VMEM((SEQ_TILE + B_HALO, D_B), F32)]
                       + _weight_scratch(*w_in.shape[1:]) + _weight_scratch(*w_out.shape[1:])
                       + sc_scratch,
        compiler_params=_PARAMS,
        name="mixer_hybrid",
    )(x, mod_l, w_in, w_out, *consts)


def _conformer_call(prev, mod_prev, mod_l, w1, w2, consts):
    x1p, ysp, qp, segbp, segnp = prev
    bsz, seq, d = x1p.shape
    g_specs, g_scratch = _gather_inputs(bsz, seq, d)
    out_specs, out_shapes, sc_scratch = _scatter_outputs(bsz, seq, d)
    hbm = pl.BlockSpec(memory_space=pl.ANY)
    grid_spec = pltpu.PrefetchScalarGridSpec(
        num_scalar_prefetch=2,
        grid=(bsz, seq // SEQ_TILE),
        in_specs=g_specs + [pl.BlockSpec((1, N_MOD, d), lambda b, s, *_: (b, 0, 0)), hbm, hbm]
                 + _const_specs(consts),
        out_specs=out_specs,
        scratch_shapes=[pltpu.VMEM((SEQ_TILE + C_HALO, D_C), F32), pltpu.VMEM((SEQ_TILE, D_C), F32)]
                       + g_scratch + _weight_scratch(*w1.shape[1:]) + _weight_scratch(*w2.shape[1:])
                       + sc_scratch,
    )
    return pl.pallas_call(
        _conformer_kernel, grid_spec=grid_spec, out_shape=out_shapes,
        compiler_params=_PARAMS, name="mixer_conformer",
    )(segbp, segnp, x1p, ysp, qp, mod_prev, mod_l, w1, w2, *consts)


def _final_call(prev, mod_prev, final_g):
    x1p, ysp, qp, segbp, segnp = prev
    bsz, seq, d = x1p.shape
    g_specs, g_scratch = _gather_inputs(bsz, seq, d)
    grid_spec = pltpu.PrefetchScalarGridSpec(
        num_scalar_prefetch=2,
        grid=(bsz, seq // SEQ_TILE),
        in_specs=g_specs + [pl.BlockSpec((1, d), lambda b, s, *_: (0, 0))],
        out_specs=pl.BlockSpec((1, SEQ_TILE, d), lambda b, s, *_: (b, s, 0)),
        scratch_shapes=g_scratch,
    )
    return pl.pallas_call(
        _final_kernel, grid_spec=grid_spec,
        out_shape=jax.ShapeDtypeStruct((bsz, seq, d), F32),
        compiler_params=_PARAMS, name="final_norm",
    )(segbp, segnp, x1p, ysp, qp, mod_prev, final_g.reshape(1, d))


def _moe_kernel(blk_ref, grp_ref, valid_ref, fresh_ref, hs_ref, wg_ref, wu_ref, wd_ref, ys_ref,
                wg_bf, wu_bf, wd_bf):
    i = pl.program_id(0)

    @pl.when(fresh_ref[i] != 0)
    def _():
        wg_bf[...] = wg_ref[0].astype(BF16)
        wu_bf[...] = wu_ref[0].astype(BF16)
        wd_bf[...] = wd_ref[0].astype(BF16)

    @pl.when(valid_ref[i] != 0)
    def _():
        rows = hs_ref[...]
        h = rows[:, :D_MODEL]
        m = rows[:, D_MODEL:].astype(F32)
        ne = EXPERTS_PER_GROUP
        y = None
        for j in range(ne):
            gate = m[:, j:j + 1] + (m[:, ne + j:ne + j + 1] + m[:, 2 * ne + j:2 * ne + j + 1])
            a = jnp.dot(h, wg_bf[j], preferred_element_type=F32)
            a = a * jax.nn.sigmoid(a) * jnp.dot(h, wu_bf[j], preferred_element_type=F32)
            yj = jnp.dot((a * gate).astype(BF16), wd_bf[j], preferred_element_type=F32)
            y = yj if y is None else y + yj
        ys_ref[...] = y.astype(BF16)


def _moe_items(fill, region_rows, n_items):
    tiles = (fill + (MOE_TILE - 1)) // MOE_TILE
    ends = jnp.cumsum(tiles)
    i = jnp.minimum(jnp.arange(n_items, dtype=I32), jnp.maximum(ends[-1] - 1, 0))
    grp = jnp.sum((i[:, None] >= ends[None, :]).astype(I32), axis=1)
    blk = grp * (region_rows // MOE_TILE) + i - (ends - tiles)[grp]
    valid = (jnp.arange(n_items, dtype=I32) < ends[-1]).astype(I32)
    fresh = jnp.concatenate([jnp.ones((1,), I32), (grp[1:] != grp[:-1]).astype(I32)])
    return blk.astype(I32), grp.astype(I32), valid, fresh


def _moe_call(hs, fill, layer, wg, wu, wd, t):
    region_rows = hs.shape[0] // N_GROUPS
    d = D_MODEL
    ne = EXPERTS_PER_GROUP
    n_items = -(-(t + (t // SEQ_TILE) * N_GROUPS * (ROW_ALIGN - 1)) // MOE_TILE) + N_GROUPS
    items = _moe_items(fill, region_rows, n_items)
    w_idx = lambda i, blk, grp, valid, fresh: (layer * N_GROUPS + grp[i], 0, 0, 0)
    row_idx = lambda i, blk, grp, valid, fresh: (blk[i], 0)
    grid_spec = pltpu.PrefetchScalarGridSpec(
        num_scalar_prefetch=len(items),
        grid=(n_items,),
        in_specs=[pl.BlockSpec((MOE_TILE, ROW_W), row_idx),
                  pl.BlockSpec((1, ne, d, D_EXPERT), w_idx),
                  pl.BlockSpec((1, ne, d, D_EXPERT), w_idx),
                  pl.BlockSpec((1, ne, D_EXPERT, d), w_idx)],
        out_specs=pl.BlockSpec((MOE_TILE, d), row_idx),
        scratch_shapes=[pltpu.VMEM((ne, d, D_EXPERT), BF16), pltpu.VMEM((ne, d, D_EXPERT), BF16),
                        pltpu.VMEM((ne, D_EXPERT, d), BF16)],
    )
    return pl.pallas_call(
        _moe_kernel, grid_spec=grid_spec,
        out_shape=jax.ShapeDtypeStruct((hs.shape[0], d), BF16),
        compiler_params=pltpu.CompilerParams(dimension_semantics=("arbitrary",),
                                             vmem_limit_bytes=VMEM_LIMIT),
        name="moe",
    )(*items, hs, wg, wu, wd)


def kernel(x, c, ada_w, ada_b, norm_mix_g, norm_ffn_g, hyb_w_in, sgu_ln_g, sgu_ln_b, sgu_w, sgu_b,
           sconv_w, hyb_w_out, conf_w1, conf_b1, conf_dw_w, conf_dw_b, conf_ln_g, conf_ln_b,
           conf_w2, conf_b2, router_w, router_bias, moe_w_gate, moe_w_up, moe_w_down, final_norm_g):
    depth = ada_w.shape[0]
    bsz, seq, d = x.shape
    t = bsz * seq
    assert depth == 2 and seq % SEQ_TILE == 0
    mod = _ada_mod(c, ada_w, ada_b).reshape(depth, bsz, N_MOD, d)

    perm = jnp.arange(N_EXPERTS).reshape(N_GROUPS, EXPERTS_PER_GROUP).T.reshape(-1)
    rw_t = router_w.T[perm]
    rb_t = router_bias[perm].reshape(N_EXPERTS, 1)

    ne = EXPERTS_PER_GROUP
    wg = moe_w_gate.reshape(depth * N_GROUPS, ne, d, D_EXPERT)
    wu = moe_w_up.reshape(depth * N_GROUPS, ne, d, D_EXPERT)
    wd = moe_w_down.reshape(depth * N_GROUPS, ne, D_EXPERT, d)

    row = lambda a: a.reshape(1, -1)
    consts0 = [row(norm_mix_g[0]), row(norm_ffn_g[0]), row(sgu_ln_g[0]), row(sgu_ln_b[0]), sgu_w[0],
               sgu_b[0].reshape(A_HEADS, CHUNK, 1), sconv_w[0], rw_t, rb_t]
    x1, hs, q, segb, segn, fill = _hybrid_call(x, mod[0], hyb_w_in, hyb_w_out, consts0)
    ys = _moe_call(hs, fill, 0, wg, wu, wd, t)

    consts1 = [row(norm_mix_g[1]), row(norm_ffn_g[1]), row(conf_b1[0]), conf_dw_w[0],
               row(conf_dw_b[0]), row(conf_ln_g[0]), row(conf_ln_b[0]), row(conf_b2[0]), rw_t, rb_t]
    x1, hs, q1, segb1, segn1, fill1 = _conformer_call((x1, ys, q, segb, segn), mod[0], mod[1],
                                                      conf_w1, conf_w2, consts1)
    ys = _moe_call(hs, fill1, 1, wg, wu, wd, t)
    return _final_call((x1, ys, q1, segb1, segn1), mod[1], final_norm_g)
```

```python
import functools

import jax
import jax.numpy as jnp
from jax import lax
from jax.experimental import pallas as pl
from jax.experimental.pallas import tpu as pltpu

D_MODEL = 1024
CHUNK = 128
A_HEADS = 4
A_HEAD_DIM = 128
D_A = A_HEADS * A_HEAD_DIM
D_B = 512
B_CONV = 3
D_IN_HYB = 2 * D_A + 3 * D_B
D_C = D_MODEL
C_CONV = 31
N_EXPERTS = 16
N_GROUPS = 4
EXPERTS_PER_GROUP = N_EXPERTS // N_GROUPS
D_EXPERT = 256
N_MOD = 6
EPS = 1e-6

SUBLANES = 8
LANES = 128
ROW_ALIGN = 16

SEQ_TILE = 512
MOE_TILE = 512
MOD_TILE = 1024
W_STAGE_ROWS = 128
C_HALO = 32
B_HALO = 8
CONV_ROWS = 128
GLU_COLS = 256
SORT_ROWS = SEQ_TILE + N_GROUPS * ROW_ALIGN
ROW_W = D_MODEL + LANES
VMEM_LIMIT = 56 * 1024 * 1024

F32 = jnp.float32
BF16 = jnp.bfloat16
I32 = jnp.int32


def _round_up(n, m):
    return -(-n // m) * m


def _region_rows(t):
    return _round_up(t + (t // SEQ_TILE) * ROW_ALIGN, MOE_TILE)


def _split_bf16(a):
    hi = a.astype(BF16)
    lo = (a - hi.astype(F32)).astype(BF16)
    return hi, lo


def _dot3(a, b):
    ah, al = _split_bf16(a)
    bh, bl = _split_bf16(b)
    d = functools.partial(jnp.dot, preferred_element_type=F32)
    return d(ah, bh) + (d(ah, bl) + d(al, bh))


def _rms(x, g):
    return x * lax.rsqrt(jnp.mean(x * x, axis=-1, keepdims=True) + EPS) * g


def _rms_mod(x, g, shift, scale):
    return x * lax.rsqrt(jnp.mean(x * x, axis=-1, keepdims=True) + EPS) * (g * (1.0 + scale)) + shift


def _gelu(x):
    return 0.5 * x * (1.0 + lax.erf(x * (2.0 ** -0.5)))


def _layer_norm(x, g, b):
    mu = jnp.mean(x, axis=-1, keepdims=True)
    xc = x - mu
    var = jnp.mean(xc * xc, axis=-1, keepdims=True)
    return xc * lax.rsqrt(var + EPS) * g + b


def _mod_kernel(c_ref, w_ref, b_ref, o_ref):
    c = c_ref[...]
    c_act = c * jax.nn.sigmoid(c)
    o_ref[0] = _dot3(c_act, w_ref[0]) + b_ref[0]


def _ada_mod(c, ada_w, ada_b):
    depth, d, n = ada_w.shape
    bsz = c.shape[0]
    return pl.pallas_call(
        _mod_kernel,
        grid=(depth, n // MOD_TILE),
        in_specs=[
            pl.BlockSpec((bsz, d), lambda l, j: (0, 0)),
            pl.BlockSpec((1, d, MOD_TILE), lambda l, j: (l, 0, j)),
            pl.BlockSpec((1, 1, MOD_TILE), lambda l, j: (l, 0, j)),
        ],
        out_specs=pl.BlockSpec((1, bsz, MOD_TILE), lambda l, j: (l, 0, j)),
        out_shape=jax.ShapeDtypeStruct((depth, bsz, n), F32),
        compiler_params=pltpu.CompilerParams(
            dimension_semantics=("arbitrary", "arbitrary"), vmem_limit_bytes=VMEM_LIMIT),
        name="ada_mod",
    )(c, ada_w, ada_b.reshape(depth, 1, n))


def _route(h2, rw_ref, rb_ref):
    wh, wl = _split_bf16(rw_ref[...])
    hh, hl = _split_bf16(h2)
    nt = functools.partial(lax.dot_general, dimension_numbers=(((1,), (1,)), ((), ())),
                           preferred_element_type=F32)
    logits = nt(wh, hh) + (nt(wh, hl) + nt(wl, hh))
    scores = jax.nn.sigmoid(logits)
    biased = scores + rb_ref[...]
    ng, ne = N_GROUPS, EXPERTS_PER_GROUP
    b = [biased[j * ng:(j + 1) * ng] for j in range(ne)]
    s = [scores[j * ng:(j + 1) * ng] for j in range(ne)]
    gscore = None
    for i in range(ne):
        for j in range(i + 1, ne):
            pair = b[i] + b[j]
            gscore = pair if gscore is None else jnp.maximum(gscore, pair)
    gid = lax.broadcasted_iota(I32, gscore.shape, 0)
    gmax = jnp.max(gscore, axis=0, keepdims=True)
    best = jnp.min(jnp.where(gscore == gmax, gid, ng), axis=0, keepdims=True)
    onb = gid == best
    cb = [jnp.sum(jnp.where(onb, b[j], 0.0), axis=0, keepdims=True) for j in range(ne)]
    cs = [jnp.sum(jnp.where(onb, s[j], 0.0), axis=0, keepdims=True) for j in range(ne)]
    sel = []
    for j in range(ne):
        rank = jnp.zeros_like(best)
        for i in range(ne):
            if i == j:
                continue
            ahead = (cb[i] >= cb[j]) if i < j else (cb[i] > cb[j])
            rank = rank + ahead.astype(I32)
        sel.append(rank < 2)
    wsum = sum(jnp.where(sel[j], cs[j], 0.0) for j in range(ne))
    gates = jnp.concatenate([jnp.where(sel[j], cs[j], 0.0) / wsum for j in range(ne)], axis=0)
    return gates, jnp.where(onb, 1.0, 0.0)


def _for_each_chunk(nrows, max_rows, fn):
    size = max_rows
    while size >= ROW_ALIGN:
        off = nrows & ~(2 * size - 1)

        @pl.when((nrows & size) != 0)
        def _(off=off, size=size):
            fn(pl.multiple_of(off, ROW_ALIGN), size)

        size //= 2


def _sort_matrix(q):
    rows = lax.broadcasted_iota(I32, (SORT_ROWS, q.shape[1]), 0)
    return jnp.where(rows == q, 1.0, 0.0).astype(BF16)


def _run_copies(op, src_of, dst_of, sem, nrows, bases, region_rows):
    tile_off = 0
    for g in range(N_GROUPS):
        def fn(off, size, g=g, tile_off=tile_off):
            a = pl.multiple_of(tile_off + off, ROW_ALIGN)
            b = pl.multiple_of(g * region_rows + bases[g] + off, ROW_ALIGN)
            cp = pltpu.make_async_copy(src_of(a, b, size), dst_of(a, b, size), sem)
            getattr(cp, op)()
        _for_each_chunk(nrows[g], SEQ_TILE, fn)
        tile_off = tile_off + nrows[g]


class _ScatterRefs:
    def __init__(self, gffn_ref, rw_ref, rb_ref, hs_ref, q_ref, segb_ref, segn_ref, fill_ref,
                 h2buf, rowbuf, tri, zbuf, fill_s, pend_s, ssem, zsem):
        self.__dict__.update(locals())
        self.region_rows = hs_ref.shape[0] // N_GROUPS
        self.src_of = lambda a, b, size: rowbuf.at[pl.ds(a, size)]
        self.dst_of = lambda a, b, size: hs_ref.at[pl.ds(b, size)]


def _sc_init(sc, tile):
    @pl.when(tile == 0)
    def _():
        for g in range(N_GROUPS):
            sc.fill_s[g] = 0
            sc.pend_s[g] = 0
            sc.pend_s[N_GROUPS + g] = 0
        ts = sc.tri.shape[0]
        k = lax.broadcasted_iota(I32, (ts, ts), 0)
        t = lax.broadcasted_iota(I32, (ts, ts), 1)
        sc.tri[...] = jnp.where(k <= t, 1.0, 0.0).astype(BF16)
        sc.h2buf[...] = jnp.zeros_like(sc.h2buf)


def _sc_pending(sc, op):
    _run_copies(op, sc.src_of, sc.dst_of, sc.ssem, [sc.pend_s[g] for g in range(N_GROUPS)],
                [sc.pend_s[N_GROUPS + g] for g in range(N_GROUPS)], sc.region_rows)


def _prenorm(x1, mod, gffn_ref):
    sh2, sc2 = mod[3:4], mod[4:5]
    return _rms_mod(x1, gffn_ref[...], sh2, sc2)


def _sc_route(sc, st):
    st["h2"] = sc.h2buf[...]
    st["gates"], st["onehot"] = _route(st["h2"], sc.rw_ref, sc.rb_ref)


def _sc_rank(sc, st, active):
    onehot = st["onehot"]
    ts = onehot.shape[1]
    oh8 = jnp.concatenate([onehot, jnp.zeros((SUBLANES - N_GROUPS, ts), F32)], axis=0).astype(BF16)
    csum = jnp.dot(oh8, sc.tri[...], preferred_element_type=F32)[:N_GROUPS]
    counts = [csum[g, ts - 1].astype(I32) * active for g in range(N_GROUPS)]
    st["nrows"] = [(c + (ROW_ALIGN - 1)) & ~(ROW_ALIGN - 1) for c in counts]
    q = jnp.zeros((1, ts), F32)
    tile_off = 0
    for g in range(N_GROUPS):
        q = q + onehot[g:g + 1] * (csum[g:g + 1] - 1.0 + tile_off.astype(F32)
                                   if g else csum[g:g + 1] - 1.0)
        tile_off = tile_off + st["nrows"][g]
    st["q"] = q.astype(I32)


def _sc_sort(sc, st):
    gates = st["gates"]
    ts = gates.shape[1]
    hi = gates.astype(BF16).astype(F32)
    mid = (gates - hi).astype(BF16).astype(F32)
    lo = (gates - hi - mid).astype(BF16).astype(F32)
    meta_t = jnp.concatenate(
        [hi, mid, lo, jnp.zeros((LANES - 3 * EXPERTS_PER_GROUP, ts), F32)], axis=0)
    rows = jnp.concatenate([st["h2"].astype(BF16), meta_t.T.astype(BF16)], axis=1)
    sorted_rows = jnp.dot(_sort_matrix(st["q"]), rows, preferred_element_type=F32)
    sc.rowbuf[...] = sorted_rows.astype(BF16)


def _sc_start(sc, st, t_idx):
    sc.q_ref[t_idx] = st["q"]
    nrows = st["nrows"]
    for g in range(N_GROUPS):
        base = sc.fill_s[g]
        sc.pend_s[g] = nrows[g]
        sc.pend_s[N_GROUPS + g] = base
        sc.segn_ref[t_idx * N_GROUPS + g] = nrows[g]
        sc.segb_ref[t_idx * N_GROUPS + g] = base
        sc.fill_s[g] = base + nrows[g]
    _sc_pending(sc, "start")


def _sc_finish(sc):
    _sc_pending(sc, "wait")
    sc.zbuf[...] = jnp.zeros_like(sc.zbuf)
    for op in ("start", "wait"):
        for g in range(N_GROUPS):
            fill = sc.fill_s[g]
            def fn(off, size, g=g, fill=fill, op=op):
                b = pl.multiple_of(g * sc.region_rows + fill + off, ROW_ALIGN)
                cp = pltpu.make_async_copy(sc.zbuf.at[pl.ds(0, size)],
                                           sc.hs_ref.at[pl.ds(b, size)], sc.zsem)
                getattr(cp, op)()
            _for_each_chunk((-fill) & (MOE_TILE - 1), MOE_TILE // 2, fn)
    for g in range(N_GROUPS):
        sc.fill_ref[g] = sc.fill_s[g]


def _gather_prev_ffn(tile, n_tiles, segb_ref, segn_ref, ys_ref, q_ref, ybuf, gsem):
    region_rows = ys_ref.shape[0] // N_GROUPS

    def copies(op, t, slot):
        nrows = [segn_ref[t * N_GROUPS + g] for g in range(N_GROUPS)]
        bases = [segb_ref[t * N_GROUPS + g] for g in range(N_GROUPS)]
        src_of = lambda a, b, size: ys_ref.at[pl.ds(b, size)]
        dst_of = lambda a, b, size: ybuf.at[slot, pl.ds(a, size)]
        _run_copies(op, src_of, dst_of, gsem.at[slot], nrows, bases, region_rows)

    def issue(t, slot):
        ybuf[slot, SEQ_TILE:SORT_ROWS, :] = jnp.zeros((SORT_ROWS - SEQ_TILE, D_MODEL), BF16)
        copies("start", t, slot)

    @pl.when(tile == 0)
    def _():
        issue(tile, 0)

    @pl.when(tile + 1 < n_tiles)
    def _():
        issue(tile + 1, (tile + 1) % 2)

    slot = tile % 2
    copies("wait", tile, slot)
    p = _sort_matrix(q_ref[0])
    return lax.dot_general(p, ybuf[slot], (((0,), (0,)), ((), ())), preferred_element_type=F32)


def _load_weight_bf16(w_hbm, w_vmem, stage, sem):
    rows = stage.shape[1]
    n = w_vmem.shape[0] // rows
    copy = lambda c: pltpu.make_async_copy(w_hbm.at[pl.ds(c * rows, rows)], stage.at[c % 2],
                                           sem.at[c % 2])
    copy(0).start()
    for c in range(n):
        if c + 1 < n:
            copy(c + 1).start()
        copy(c).wait()
        w_vmem[c * rows:(c + 1) * rows, :] = stage[c % 2].astype(BF16)


def _weight_scratch(k, n):
    return [pltpu.VMEM((k, n), BF16), pltpu.VMEM((2, W_STAGE_ROWS, n), F32),
            pltpu.SemaphoreType.DMA((2,))]


def _hybrid_kernel(x_ref, mod_ref, win_hbm, wout_hbm, gmix_ref, gffn_ref, lng_ref, lnb_ref, sw_ref,
                   sb_ref, cw_ref, rw_ref, rb_ref,
                   x1_ref, hs_ref, q_ref, segb_ref, segn_ref, fill_ref,
                   pbuf, win_ref, win_stage, win_sem, wout_ref, wout_stage, wout_sem,
                   h2buf, rowbuf, tri, zbuf, fill_s, pend_s, ssem, zsem):
    ts = x_ref.shape[1]
    tile = pl.program_id(0) * pl.num_programs(1) + pl.program_id(1)
    n_tiles = pl.num_programs(0) * pl.num_programs(1)
    sc = _ScatterRefs(gffn_ref, rw_ref, rb_ref, hs_ref, q_ref, segb_ref, segn_ref, fill_ref,
                      h2buf, rowbuf, tri, zbuf, fill_s, pend_s, ssem, zsem)

    @pl.when(tile == 0)
    def _():
        _load_weight_bf16(win_hbm.at[0], win_ref, win_stage, win_sem)
        _load_weight_bf16(wout_hbm.at[0], wout_ref, wout_stage, wout_sem)

    @pl.when(pl.program_id(1) == 0)
    def _():
        pbuf[0:B_HALO, :] = jnp.zeros((B_HALO, D_B), F32)

    _sc_init(sc, tile)
    _sc_pending(sc, "wait")

    prev = {}
    active = (tile > 0).astype(I32)

    x = x_ref[0]
    mod = mod_ref[0]
    sh1, sc1, g1 = mod[0:1], mod[1:2], mod[2:3]
    h = _rms_mod(x, gmix_ref[...], sh1, sc1)
    hb16 = h.astype(BF16)
    zcol = lambda j: jnp.dot(hb16, win_ref[:, j * D_A:(j + 1) * D_A], preferred_element_type=F32)

    v = _layer_norm(_gelu(zcol(1)), lng_ref[...], lnb_ref[...]).astype(BF16)
    _sc_route(sc, prev)
    u = _gelu(zcol(0))
    tril = (lax.broadcasted_iota(I32, (CHUNK, CHUNK), 0)
            >= lax.broadcasted_iota(I32, (CHUNK, CHUNK), 1))
    chunks = []
    for n in range(ts // CHUNK):
        heads = []
        for hd in range(A_HEADS):
            w = jnp.where(tril, sw_ref[hd], 0.0).astype(BF16)
            vb = v[n * CHUNK:(n + 1) * CHUNK, hd * A_HEAD_DIM:(hd + 1) * A_HEAD_DIM]
            heads.append(jnp.dot(w, vb, preferred_element_type=F32) + sb_ref[hd])
        chunks.append(jnp.concatenate(heads, axis=1))
    y_a = u * jnp.concatenate(chunks, axis=0)
    _sc_rank(sc, prev, active)

    p = zcol(3) * zcol(4)
    pbuf[B_HALO:B_HALO + ts, :] = p
    cw = cw_ref[...]
    conv = (cw[0:1] * pbuf[B_HALO - 2:B_HALO - 2 + ts, :]
            + cw[1:2] * pbuf[B_HALO - 1:B_HALO - 1 + ts, :]
            + cw[2:3] * p)
    pbuf[0:B_HALO, :] = p[ts - B_HALO:ts, :]
    y_b = zcol(2) * conv
    _sc_sort(sc, prev)

    ycat = jnp.concatenate([y_a, y_b], axis=1).astype(BF16)
    y = jnp.dot(ycat, wout_ref[...], preferred_element_type=F32)
    x1 = x + g1 * y
    x1_ref[0] = x1
    h2buf[...] = _prenorm(x1, mod, gffn_ref)
    _sc_start(sc, prev, jnp.maximum(tile - 1, 0))

    @pl.when(tile == n_tiles - 1)
    def _():
        _sc_pending(sc, "wait")
        last = {}
        _sc_route(sc, last)
        _sc_rank(sc, last, 1)
        _sc_sort(sc, last)
        _sc_start(sc, last, tile)
        _sc_finish(sc)


def _shift_down(part, r):
    if r == 0:
        return part[SUBLANES:]
    tiles = part.reshape(part.shape[0] // SUBLANES, SUBLANES, part.shape[1])
    rolled = pltpu.roll(tiles, r, axis=1)
    low = lax.broadcasted_iota(I32, (1, SUBLANES, part.shape[1]), 1) < r
    merged = jnp.where(low, rolled[:-1], rolled[1:])
    return merged.reshape(part.shape[0] - SUBLANES, part.shape[1])


def _causal_dw_conv(cbuf, w_ref, obuf, row0, nrows, col0, ncols):
    n_a = -(-C_CONV // SUBLANES)
    for t0 in range(row0, row0 + nrows, CONV_ROWS):
        for c0 in range(col0, col0 + ncols, LANES):
            cols = slice(c0, c0 + LANES)
            out = None
            for r in range(SUBLANES):
                part = None
                for a in range(n_a):
                    k = C_CONV - 1 - SUBLANES * a - r
                    if k < 0:
                        continue
                    r0 = t0 + C_HALO - SUBLANES * (a + 1)
                    term = w_ref[k:k + 1, cols] * cbuf[r0:r0 + CONV_ROWS + SUBLANES, cols]
                    part = term if part is None else part + term
                piece = _shift_down(part, r)
                out = piece if out is None else out + piece
            obuf[t0:t0 + CONV_ROWS, cols] = out


def _conformer_kernel(psegb_ref, psegn_ref,
                      xp_ref, ys_ref, pq_ref, pmod_ref, mod_ref, w1_hbm, w2_hbm, gmix_ref, gffn_ref,
                      b1_ref, dww_ref, dwb_ref, lng_ref, lnb_ref, b2_ref, rw_ref, rb_ref,
                      x1_ref, hs_ref, q_ref, segb_ref, segn_ref, fill_ref,
                      cbuf, obuf, ybuf, gsem, w1_ref, w1_stage, w1_sem, w2_ref, w2_stage, w2_sem,
                      h2buf, rowbuf, tri, zbuf, fill_s, pend_s, ssem, zsem):
    ts = xp_ref.shape[1]
    tile = pl.program_id(0) * pl.num_programs(1) + pl.program_id(1)
    n_tiles = pl.num_programs(0) * pl.num_programs(1)
    sc = _ScatterRefs(gffn_ref, rw_ref, rb_ref, hs_ref, q_ref, segb_ref, segn_ref, fill_ref,
                      h2buf, rowbuf, tri, zbuf, fill_s, pend_s, ssem, zsem)

    @pl.when(tile == 0)
    def _():
        _load_weight_bf16(w1_hbm.at[0], w1_ref, w1_stage, w1_sem)
        _load_weight_bf16(w2_hbm.at[0], w2_ref, w2_stage, w2_sem)

    @pl.when(pl.program_id(1) == 0)
    def _():
        cbuf[0:C_HALO, :] = jnp.zeros((C_HALO, D_C), F32)

    _sc_init(sc, tile)
    _sc_pending(sc, "wait")
    y_prev = _gather_prev_ffn(tile, n_tiles, psegb_ref, psegn_ref, ys_ref, pq_ref, ybuf, gsem)

    prev = {}
    active = (tile > 0).astype(I32)
    deferred = [lambda: _sc_route(sc, prev), lambda: _sc_rank(sc, prev, active),
                lambda: _sc_sort(sc, prev)]

    x = xp_ref[0] + pmod_ref[0][5:6] * y_prev
    mod = mod_ref[0]
    sh1, sc1, g1 = mod[0:1], mod[1:2], mod[2:3]
    h = _rms_mod(x, gmix_ref[...], sh1, sc1)
    hb = h.astype(BF16)
    for k, c0 in enumerate(range(0, D_C, GLU_COLS)):
        ca, cg = slice(c0, c0 + GLU_COLS), slice(D_C + c0, D_C + c0 + GLU_COLS)
        za = jnp.dot(hb, w1_ref[:, ca], preferred_element_type=F32) + b1_ref[:, ca]
        zg = jnp.dot(hb, w1_ref[:, cg], preferred_element_type=F32) + b1_ref[:, cg]
        cbuf[C_HALO:C_HALO + ts, ca] = za * jax.nn.sigmoid(zg)
        if k < len(deferred):
            deferred[k]()
        _causal_dw_conv(cbuf, dww_ref, obuf, 0, ts, c0, GLU_COLS)
    cbuf[0:C_HALO, :] = cbuf[ts:ts + C_HALO, :]

    zc = _layer_norm(obuf[...] + dwb_ref[...], lng_ref[...], lnb_ref[...])
    zc = zc * jax.nn.sigmoid(zc)
    y = jnp.dot(zc.astype(BF16), w2_ref[...], preferred_element_type=F32) + b2_ref[...]
    x1 = x + g1 * y
    x1_ref[0] = x1
    h2buf[...] = _prenorm(x1, mod, gffn_ref)
    _sc_start(sc, prev, jnp.maximum(tile - 1, 0))

    @pl.when(tile == n_tiles - 1)
    def _():
        _sc_pending(sc, "wait")
        last = {}
        _sc_route(sc, last)
        _sc_rank(sc, last, 1)
        _sc_sort(sc, last)
        _sc_start(sc, last, tile)
        _sc_finish(sc)


def _final_kernel(psegb_ref, psegn_ref, xp_ref, ys_ref, pq_ref, pmod_ref, fg_ref, o_ref, ybuf, gsem):
    tile = pl.program_id(0) * pl.num_programs(1) + pl.program_id(1)
    n_tiles = pl.num_programs(0) * pl.num_programs(1)
    y_prev = _gather_prev_ffn(tile, n_tiles, psegb_ref, psegn_ref, ys_ref, pq_ref, ybuf, gsem)
    x = xp_ref[0] + pmod_ref[0][5:6] * y_prev
    o_ref[0] = _rms(x, fg_ref[...])


def _scatter_outputs(bsz, seq, d):
    t = bsz * seq
    n_tiles = t // SEQ_TILE
    smem = pl.BlockSpec(memory_space=pltpu.SMEM)
    specs = [pl.BlockSpec((1, SEQ_TILE, d), lambda b, s, *_: (b, s, 0)),
             pl.BlockSpec(memory_space=pl.ANY),
             pl.BlockSpec((n_tiles, 1, SEQ_TILE), lambda b, s, *_: (0, 0, 0)),
             smem, smem, smem]
    shapes = [jax.ShapeDtypeStruct((bsz, seq, d), F32),
              jax.ShapeDtypeStruct((N_GROUPS * _region_rows(t), ROW_W), BF16),
              jax.ShapeDtypeStruct((n_tiles, 1, SEQ_TILE), I32),
              jax.ShapeDtypeStruct((n_tiles * N_GROUPS,), I32),
              jax.ShapeDtypeStruct((n_tiles * N_GROUPS,), I32),
              jax.ShapeDtypeStruct((N_GROUPS,), I32)]
    scratch = [pltpu.VMEM((SEQ_TILE, d), F32),
               pltpu.VMEM((SORT_ROWS, ROW_W), BF16),
               pltpu.VMEM((SEQ_TILE, SEQ_TILE), BF16),
               pltpu.VMEM((MOE_TILE // 2, ROW_W), BF16),
               pltpu.SMEM((N_GROUPS,), I32),
               pltpu.SMEM((2 * N_GROUPS,), I32),
               pltpu.SemaphoreType.DMA(()),
               pltpu.SemaphoreType.DMA(())]
    return specs, shapes, scratch


def _gather_inputs(bsz, seq, d):
    nseq = seq // SEQ_TILE
    specs = [pl.BlockSpec((1, SEQ_TILE, d), lambda b, s, *_: (b, s, 0)),
             pl.BlockSpec(memory_space=pl.ANY),
             pl.BlockSpec((1, 1, SEQ_TILE), lambda b, s, *_: (b * nseq + s, 0, 0)),
             pl.BlockSpec((1, N_MOD, d), lambda b, s, *_: (b, 0, 0))]
    scratch = [pltpu.VMEM((2, SORT_ROWS, d), BF16), pltpu.SemaphoreType.DMA((2,))]
    return specs, scratch


def _const_specs(consts):
    return [pl.BlockSpec(a.shape, lambda b, s, *_, nd=a.ndim: (0,) * nd) for a in consts]


_PARAMS = pltpu.CompilerParams(dimension_semantics=("arbitrary", "arbitrary"),
                               vmem_limit_bytes=VMEM_LIMIT)


def _hybrid_call(x, mod_l, w_in, w_out, consts):
    bsz, seq, d = x.shape
    out_specs, out_shapes, sc_scratch = _scatter_outputs(bsz, seq, d)
    hbm = pl.BlockSpec(memory_space=pl.ANY)
    return pl.pallas_call(
        _hybrid_kernel,
        grid=(bsz, seq // SEQ_TILE),
        in_specs=[pl.BlockSpec((1, SEQ_TILE, d), lambda b, s: (b, s, 0)),
                  pl.BlockSpec((1, N_MOD, d), lambda b, s: (b, 0, 0)), hbm, hbm]
                 + _const_specs(consts),
        out_specs=out_specs,
        out_shape=out_shapes,
        scratch_shapes=[pltpu.VMEM((SEQ_TILE + B_HALO, D_B), F32)]
                       + _weight_scratch(*w_in.shape[1:]) + _weight_scratch(*w_out.shape[1:])
                       + sc_scratch,
        compiler_params=_PARAMS,
        name="mixer_hybrid",
    )(x, mod_l, w_in, w_out, *consts)


def _conformer_call(prev, mod_prev, mod_l, w1, w2, consts):
    x1p, ysp, qp, segbp, segnp = prev
    bsz, seq, d = x1p.shape
    g_specs, g_scratch = _gather_inputs(bsz, seq, d)
    out_specs, out_shapes, sc_scratch = _scatter_outputs(bsz, seq, d)
    hbm = pl.BlockSpec(memory_space=pl.ANY)
    grid_spec = pltpu.PrefetchScalarGridSpec(
        num_scalar_prefetch=2,
        grid=(bsz, seq // SEQ_TILE),
        in_specs=g_specs + [pl.BlockSpec((1, N_MOD, d), lambda b, s, *_: (b, 0, 0)), hbm, hbm]
                 + _const_specs(consts),
        out_specs=out_specs,
        scratch_shapes=[pltpu.VMEM((SEQ_TILE + C_HALO, D_C), F32), pltpu.VMEM((SEQ_TILE, D_C), F32)]
                       + g_scratch + _weight_scratch(*w1.shape[1:]) + _weight_scratch(*w2.shape[1:])
                       + sc_scratch,
    )
    return pl.pallas_call(
        _conformer_kernel, grid_spec=grid_spec, out_shape=out_shapes,
        compiler_params=_PARAMS, name="mixer_conformer",
    )(segbp, segnp, x1p, ysp, qp, mod_prev, mod_l, w1, w2, *consts)


def _final_call(prev, mod_prev, final_g):
    x1p, ysp, qp, segbp, segnp = prev
    bsz, seq, d = x1p.shape
    g_specs, g_scratch = _gather_inputs(bsz, seq, d)
    grid_spec = pltpu.PrefetchScalarGridSpec(
        num_scalar_prefetch=2,
        grid=(bsz, seq // SEQ_TILE),
        in_specs=g_specs + [pl.BlockSpec((1, d), lambda b, s, *_: (0, 0))],
        out_specs=pl.BlockSpec((1, SEQ_TILE, d), lambda b, s, *_: (b, s, 0)),
        scratch_shapes=g_scratch,
    )
    return pl.pallas_call(
        _final_kernel, grid_spec=grid_spec,
        out_shape=jax.ShapeDtypeStruct((bsz, seq, d), F32),
        compiler_params=_PARAMS, name="final_norm",
    )(segbp, segnp, x1p, ysp, qp, mod_prev, final_g.reshape(1, d))


def _moe_kernel(blk_ref, grp_ref, valid_ref, fresh_ref, hs_ref, wg_ref, wu_ref, wd_ref, ys_ref,
                wg_bf, wu_bf, wd_bf):
    i = pl.program_id(0)

    @pl.when(fresh_ref[i] != 0)
    def _():
        wg_bf[...] = wg_ref[0].astype(BF16)
        wu_bf[...] = wu_ref[0].astype(BF16)
        wd_bf[...] = wd_ref[0].astype(BF16)

    @pl.when(valid_ref[i] != 0)
    def _():
        rows = hs_ref[...]
        h = rows[:, :D_MODEL]
        m = rows[:, D_MODEL:].astype(F32)
        ne = EXPERTS_PER_GROUP
        y = None
        for j in range(ne):
            gate = m[:, j:j + 1] + (m[:, ne + j:ne + j + 1] + m[:, 2 * ne + j:2 * ne + j + 1])
            a = jnp.dot(h, wg_bf[j], preferred_element_type=F32)
            a = a * jax.nn.sigmoid(a) * jnp.dot(h, wu_bf[j], preferred_element_type=F32)
            yj = jnp.dot((a * gate).astype(BF16), wd_bf[j], preferred_element_type=F32)
            y = yj if y is None else y + yj
        ys_ref[...] = y.astype(BF16)


def _moe_items(fill, region_rows, n_items):
    tiles = (fill + (MOE_TILE - 1)) // MOE_TILE
    ends = jnp.cumsum(tiles)
    i = jnp.minimum(jnp.arange(n_items, dtype=I32), jnp.maximum(ends[-1] - 1, 0))
    grp = jnp.sum((i[:, None] >= ends[None, :]).astype(I32), axis=1)
    blk = grp * (region_rows // MOE_TILE) + i - (ends - tiles)[grp]
    valid = (jnp.arange(n_items, dtype=I32) < ends[-1]).astype(I32)
    fresh = jnp.concatenate([jnp.ones((1,), I32), (grp[1:] != grp[:-1]).astype(I32)])
    return blk.astype(I32), grp.astype(I32), valid, fresh


def _moe_call(hs, fill, layer, wg, wu, wd, t):
    region_rows = hs.shape[0] // N_GROUPS
    d = D_MODEL
    ne = EXPERTS_PER_GROUP
    n_items = -(-(t + (t // SEQ_TILE) * N_GROUPS * (ROW_ALIGN - 1)) // MOE_TILE) + N_GROUPS
    items = _moe_items(fill, region_rows, n_items)
    w_idx = lambda i, blk, grp, valid, fresh: (layer * N_GROUPS + grp[i], 0, 0, 0)
    row_idx = lambda i, blk, grp, valid, fresh: (blk[i], 0)
    grid_spec = pltpu.PrefetchScalarGridSpec(
        num_scalar_prefetch=len(items),
        grid=(n_items,),
        in_specs=[pl.BlockSpec((MOE_TILE, ROW_W), row_idx),
                  pl.BlockSpec((1, ne, d, D_EXPERT), w_idx),
                  pl.BlockSpec((1, ne, d, D_EXPERT), w_idx),
                  pl.BlockSpec((1, ne, D_EXPERT, d), w_idx)],
        out_specs=pl.BlockSpec((MOE_TILE, d), row_idx),
        scratch_shapes=[pltpu.VMEM((ne, d, D_EXPERT), BF16), pltpu.VMEM((ne, d, D_EXPERT), BF16),
                        pltpu.VMEM((ne, D_EXPERT, d), BF16)],
    )
    return pl.pallas_call(
        _moe_kernel, grid_spec=grid_spec,
        out_shape=jax.ShapeDtypeStruct((hs.shape[0], d), BF16),
        compiler_params=pltpu.CompilerParams(dimension_semantics=("arbitrary",),
                                             vmem_limit_bytes=VMEM_LIMIT),
        name="moe",
    )(*items, hs, wg, wu, wd)


def kernel(x, c, ada_w, ada_b, norm_mix_g, norm_ffn_g, hyb_w_in, sgu_ln_g, sgu_ln_b, sgu_w, sgu_b,
           sconv_w, hyb_w_out, conf_w1, conf_b1, conf_dw_w, conf_dw_b, conf_ln_g, conf_ln_b,
           conf_w2, conf_b2, router_w, router_bias, moe_w_gate, moe_w_up, moe_w_down, final_norm_g):
    depth = ada_w.shape[0]
    bsz, seq, d = x.shape
    t = bsz * seq
    assert depth == 2 and seq % SEQ_TILE == 0
    mod = _ada_mod(c, ada_w, ada_b).reshape(depth, bsz, N_MOD, d)

    perm = jnp.arange(N_EXPERTS).reshape(N_GROUPS, EXPERTS_PER_GROUP).T.reshape(-1)
    rw_t = router_w.T[perm]
    rb_t = router_bias[perm].reshape(N_EXPERTS, 1)

    ne = EXPERTS_PER_GROUP
    wg = moe_w_gate.reshape(depth * N_GROUPS, ne, d, D_EXPERT)
    wu = moe_w_up.reshape(depth * N_GROUPS, ne, d, D_EXPERT)
    wd = moe_w_down.reshape(depth * N_GROUPS, ne, D_EXPERT, d)

    row = lambda a: a.reshape(1, -1)
    consts0 = [row(norm_mix_g[0]), row(norm_ffn_g[0]), row(sgu_ln_g[0]), row(sgu_ln_b[0]), sgu_w[0],
               sgu_b[0].reshape(A_HEADS, CHUNK, 1), sconv_w[0], rw_t, rb_t]
    x1, hs, q, segb, segn, fill = _hybrid_call(x, mod[0], hyb_w_in, hyb_w_out, consts0)
    ys = _moe_call(hs, fill, 0, wg, wu, wd, t)

    consts1 = [row(norm_mix_g[1]), row(norm_ffn_g[1]), row(conf_b1[0]), conf_dw_w[0],
               row(conf_dw_b[0]), row(conf_ln_g[0]), row(conf_ln_b[0]), row(conf_b2[0]), rw_t, rb_t]
    x1, hs, q1, segb1, segn1, fill1 = _conformer_call((x1, ys, q, segb, segn), mod[0], mod[1],
                                                      conf_w1, conf_w2, consts1)
    ys = _moe_call(hs, fill1, 1, wg, wu, wd, t)
    return _final_call((x1, ys, q1, segb1, segn1), mod[1], final_norm_g)
```

```python
import functools

import jax
import jax.numpy as jnp
from jax import lax
from jax.experimental import pallas as pl
from jax.experimental.pallas import tpu as pltpu

D_MODEL = 1024
CHUNK = 128
A_HEADS = 4
A_HEAD_DIM = 128
D_A = A_HEADS * A_HEAD_DIM
D_B = 512
B_CONV = 3
D_IN_HYB = 2 * D_A + 3 * D_B
D_C = D_MODEL
C_CONV = 31
N_EXPERTS = 16
N_GROUPS = 4
EXPERTS_PER_GROUP = N_EXPERTS // N_GROUPS
D_EXPERT = 256
N_MOD = 6
EPS = 1e-6

SUBLANES = 8
LANES = 128
ROW_ALIGN = 16

SEQ_TILE = 512
MOE_TILE = 512
MOD_TILE = 1024
W_STAGE_ROWS = 128
C_HALO = 32
B_HALO = 8
CONV_ROWS = 128
GLU_COLS = 256
SORT_ROWS = SEQ_TILE + N_GROUPS * ROW_ALIGN
ROW_W = D_MODEL + LANES
VMEM_LIMIT = 56 * 1024 * 1024

F32 = jnp.float32
BF16 = jnp.bfloat16
I32 = jnp.int32


def _round_up(n, m):
    return -(-n // m) * m


def _region_rows(t):
    return _round_up(t + (t // SEQ_TILE) * ROW_ALIGN, MOE_TILE)


def _split_bf16(a):
    hi = a.astype(BF16)
    lo = (a - hi.astype(F32)).astype(BF16)
    return hi, lo


def _dot3(a, b):
    ah, al = _split_bf16(a)
    bh, bl = _split_bf16(b)
    d = functools.partial(jnp.dot, preferred_element_type=F32)
    return d(ah, bh) + (d(ah, bl) + d(al, bh))


def _rms(x, g):
    return x * lax.rsqrt(jnp.mean(x * x, axis=-1, keepdims=True) + EPS) * g


def _rms_mod(x, g, shift, scale):
    return x * lax.rsqrt(jnp.mean(x * x, axis=-1, keepdims=True) + EPS) * (g * (1.0 + scale)) + shift


def _gelu(x):
    return 0.5 * x * (1.0 + lax.erf(x * (2.0 ** -0.5)))


def _layer_norm(x, g, b):
    mu = jnp.mean(x, axis=-1, keepdims=True)
    xc = x - mu
    var = jnp.mean(xc * xc, axis=-1, keepdims=True)
    return xc * lax.rsqrt(var + EPS) * g + b


def _mod_kernel(c_ref, w_ref, b_ref, o_ref):
    c = c_ref[...]
    c_act = c * jax.nn.sigmoid(c)
    o_ref[0] = _dot3(c_act, w_ref[0]) + b_ref[0]


def _ada_mod(c, ada_w, ada_b):
    depth, d, n = ada_w.shape
    bsz = c.shape[0]
    return pl.pallas_call(
        _mod_kernel,
        grid=(depth, n // MOD_TILE),
        in_specs=[
            pl.BlockSpec((bsz, d), lambda l, j: (0, 0)),
            pl.BlockSpec((1, d, MOD_TILE), lambda l, j: (l, 0, j)),
            pl.BlockSpec((1, 1, MOD_TILE), lambda l, j: (l, 0, j)),
        ],
        out_specs=pl.BlockSpec((1, bsz, MOD_TILE), lambda l, j: (l, 0, j)),
        out_shape=jax.ShapeDtypeStruct((depth, bsz, n), F32),
        compiler_params=pltpu.CompilerParams(
            dimension_semantics=("arbitrary", "arbitrary"), vmem_limit_bytes=VMEM_LIMIT),
        name="ada_mod",
    )(c, ada_w, ada_b.reshape(depth, 1, n))


def _route(h2, rw_ref, rb_ref):
    wh, wl = _split_bf16(rw_ref[...])
    hh, hl = _split_bf16(h2)
    nt = functools.partial(lax.dot_general, dimension_numbers=(((1,), (1,)), ((), ())),
                           preferred_element_type=F32)
    logits = nt(wh, hh) + (nt(wh, hl) + nt(wl, hh))
    scores = jax.nn.sigmoid(logits)
    biased = scores + rb_ref[...]
    ng, ne = N_GROUPS, EXPERTS_PER_GROUP
    b = [biased[j * ng:(j + 1) * ng] for j in range(ne)]
    s = [scores[j * ng:(j + 1) * ng] for j in range(ne)]
    gscore = None
    for i in range(ne):
        for j in range(i + 1, ne):
            pair = b[i] + b[j]
            gscore = pair if gscore is None else jnp.maximum(gscore, pair)
    gid = lax.broadcasted_iota(I32, gscore.shape, 0)
    gmax = jnp.max(gscore, axis=0, keepdims=True)
    best = jnp.min(jnp.where(gscore == gmax, gid, ng), axis=0, keepdims=True)
    onb = gid == best
    cb = [jnp.sum(jnp.where(onb, b[j], 0.0), axis=0, keepdims=True) for j in range(ne)]
    cs = [jnp.sum(jnp.where(onb, s[j], 0.0), axis=0, keepdims=True) for j in range(ne)]
    sel = []
    for j in range(ne):
        rank = jnp.zeros_like(best)
        for i in range(ne):
            if i == j:
                continue
            ahead = (cb[i] >= cb[j]) if i < j else (cb[i] > cb[j])
            rank = rank + ahead.astype(I32)
        sel.append(rank < 2)
    wsum = sum(jnp.where(sel[j], cs[j], 0.0) for j in range(ne))
    gates = jnp.concatenate([jnp.where(sel[j], cs[j], 0.0) / wsum for j in range(ne)], axis=0)
    return gates, jnp.where(onb, 1.0, 0.0)


def _sort_matrix(q):
    rows = lax.broadcasted_iota(I32, (SORT_ROWS, q.shape[1]), 0)
    return jnp.where(rows == q, 1.0, 0.0).astype(BF16)


def _run_copies(op, src_of, dst_of, sem, nrows, bases, region_rows):
    if op == "wait":
        total = pl.multiple_of(sum(nrows[1:], nrows[0]), ROW_ALIGN)

        @pl.when(total > 0)
        def _():
            pltpu.make_async_copy(src_of(0, 0, total), dst_of(0, 0, total), sem).wait()
        return
    tile_off = 0
    for g in range(N_GROUPS):
        n = pl.multiple_of(nrows[g], ROW_ALIGN)
        a = pl.multiple_of(tile_off, ROW_ALIGN)
        b = pl.multiple_of(g * region_rows + bases[g], ROW_ALIGN)

        @pl.when(n > 0)
        def _(a=a, b=b, n=n):
            pltpu.make_async_copy(src_of(a, b, n), dst_of(a, b, n), sem).start()

        tile_off = tile_off + nrows[g]


class _ScatterRefs:
    def __init__(self, gffn_ref, rw_ref, rb_ref, hs_ref, q_ref, segb_ref, segn_ref, fill_ref,
                 h2buf, rowbuf, tri, zbuf, fill_s, pend_s, ssem, zsem):
        self.__dict__.update(locals())
        self.region_rows = hs_ref.shape[0] // N_GROUPS
        self.src_of = lambda a, b, size: rowbuf.at[pl.ds(a, size)]
        self.dst_of = lambda a, b, size: hs_ref.at[pl.ds(b, size)]


def _sc_init(sc, tile):
    @pl.when(tile == 0)
    def _():
        for g in range(N_GROUPS):
            sc.fill_s[g] = 0
            sc.pend_s[g] = 0
            sc.pend_s[N_GROUPS + g] = 0
        ts = sc.tri.shape[0]
        k = lax.broadcasted_iota(I32, (ts, ts), 0)
        t = lax.broadcasted_iota(I32, (ts, ts), 1)
        sc.tri[...] = jnp.where(k <= t, 1.0, 0.0).astype(BF16)
        sc.h2buf[...] = jnp.zeros_like(sc.h2buf)


def _sc_pending(sc, op):
    _run_copies(op, sc.src_of, sc.dst_of, sc.ssem, [sc.pend_s[g] for g in range(N_GROUPS)],
                [sc.pend_s[N_GROUPS + g] for g in range(N_GROUPS)], sc.region_rows)


def _prenorm(x1, mod, gffn_ref):
    sh2, sc2 = mod[3:4], mod[4:5]
    return _rms_mod(x1, gffn_ref[...], sh2, sc2)


def _sc_route(sc, st):
    st["h2"] = sc.h2buf[...]
    st["gates"], st["onehot"] = _route(st["h2"], sc.rw_ref, sc.rb_ref)


def _sc_rank(sc, st, active):
    onehot = st["onehot"]
    ts = onehot.shape[1]
    oh8 = jnp.concatenate([onehot, jnp.zeros((SUBLANES - N_GROUPS, ts), F32)], axis=0).astype(BF16)
    csum = jnp.dot(oh8, sc.tri[...], preferred_element_type=F32)[:N_GROUPS]
    counts = [csum[g, ts - 1].astype(I32) * active for g in range(N_GROUPS)]
    st["nrows"] = [(c + (ROW_ALIGN - 1)) & ~(ROW_ALIGN - 1) for c in counts]
    q = jnp.zeros((1, ts), F32)
    tile_off = 0
    for g in range(N_GROUPS):
        q = q + onehot[g:g + 1] * (csum[g:g + 1] - 1.0 + tile_off.astype(F32)
                                   if g else csum[g:g + 1] - 1.0)
        tile_off = tile_off + st["nrows"][g]
    st["q"] = q.astype(I32)


def _sc_sort(sc, st):
    gates = st["gates"]
    ts = gates.shape[1]
    hi = gates.astype(BF16).astype(F32)
    mid = (gates - hi).astype(BF16).astype(F32)
    lo = (gates - hi - mid).astype(BF16).astype(F32)
    meta_t = jnp.concatenate(
        [hi, mid, lo, jnp.zeros((LANES - 3 * EXPERTS_PER_GROUP, ts), F32)], axis=0)
    rows = jnp.concatenate([st["h2"].astype(BF16), meta_t.T.astype(BF16)], axis=1)
    sorted_rows = jnp.dot(_sort_matrix(st["q"]), rows, preferred_element_type=F32)
    sc.rowbuf[...] = sorted_rows.astype(BF16)


def _sc_start(sc, st, t_idx):
    sc.q_ref[t_idx] = st["q"]
    nrows = st["nrows"]
    for g in range(N_GROUPS):
        base = sc.fill_s[g]
        sc.pend_s[g] = nrows[g]
        sc.pend_s[N_GROUPS + g] = base
        sc.segn_ref[t_idx * N_GROUPS + g] = nrows[g]
        sc.segb_ref[t_idx * N_GROUPS + g] = base
        sc.fill_s[g] = base + nrows[g]
    _sc_pending(sc, "start")


def _sc_finish(sc):
    _sc_pending(sc, "wait")
    sc.zbuf[...] = jnp.zeros_like(sc.zbuf)
    for op in ("start", "wait"):
        for g in range(N_GROUPS):
            fill = sc.fill_s[g]
            n = pl.multiple_of((-fill) & (MOE_TILE - 1), ROW_ALIGN)
            b = pl.multiple_of(g * sc.region_rows + fill, ROW_ALIGN)

            @pl.when(n > 0)
            def _(n=n, b=b, op=op):
                cp = pltpu.make_async_copy(sc.zbuf.at[pl.ds(0, n)], sc.hs_ref.at[pl.ds(b, n)], sc.zsem)
                getattr(cp, op)()
    for g in range(N_GROUPS):
        sc.fill_ref[g] = sc.fill_s[g]


def _gather_prev_ffn(tile, n_tiles, segb_ref, segn_ref, ys_ref, q_ref, ybuf, gsem):
    region_rows = ys_ref.shape[0] // N_GROUPS

    def copies(op, t, slot):
        nrows = [segn_ref[t * N_GROUPS + g] for g in range(N_GROUPS)]
        bases = [segb_ref[t * N_GROUPS + g] for g in range(N_GROUPS)]
        src_of = lambda a, b, size: ys_ref.at[pl.ds(b, size)]
        dst_of = lambda a, b, size: ybuf.at[slot, pl.ds(a, size)]
        _run_copies(op, src_of, dst_of, gsem.at[slot], nrows, bases, region_rows)

    def issue(t, slot):
        ybuf[slot, SEQ_TILE:SORT_ROWS, :] = jnp.zeros((SORT_ROWS - SEQ_TILE, D_MODEL), BF16)
        copies("start", t, slot)

    @pl.when(tile == 0)
    def _():
        issue(tile, 0)

    @pl.when(tile + 1 < n_tiles)
    def _():
        issue(tile + 1, (tile + 1) % 2)

    slot = tile % 2
    copies("wait", tile, slot)
    p = _sort_matrix(q_ref[0])
    return lax.dot_general(p, ybuf[slot], (((0,), (0,)), ((), ())), preferred_element_type=F32)


def _load_weight_bf16(w_hbm, w_vmem, stage, sem):
    rows = stage.shape[1]
    n = w_vmem.shape[0] // rows
    copy = lambda c: pltpu.make_async_copy(w_hbm.at[pl.ds(c * rows, rows)], stage.at[c % 2],
                                           sem.at[c % 2])
    copy(0).start()
    for c in range(n):
        if c + 1 < n:
            copy(c + 1).start()
        copy(c).wait()
        w_vmem[c * rows:(c + 1) * rows, :] = stage[c % 2].astype(BF16)


def _weight_scratch(k, n):
    return [pltpu.VMEM((k, n), BF16), pltpu.VMEM((2, W_STAGE_ROWS, n), F32),
            pltpu.SemaphoreType.DMA((2,))]


def _hybrid_kernel(x_ref, mod_ref, win_hbm, wout_hbm, gmix_ref, gffn_ref, lng_ref, lnb_ref, sw_ref,
                   sb_ref, cw_ref, rw_ref, rb_ref,
                   x1_ref, hs_ref, q_ref, segb_ref, segn_ref, fill_ref,
                   pbuf, win_ref, win_stage, win_sem, wout_ref, wout_stage, wout_sem,
                   h2buf, rowbuf, tri, zbuf, fill_s, pend_s, ssem, zsem):
    ts = x_ref.shape[1]
    tile = pl.program_id(0) * pl.num_programs(1) + pl.program_id(1)
    n_tiles = pl.num_programs(0) * pl.num_programs(1)
    sc = _ScatterRefs(gffn_ref, rw_ref, rb_ref, hs_ref, q_ref, segb_ref, segn_ref, fill_ref,
                      h2buf, rowbuf, tri, zbuf, fill_s, pend_s, ssem, zsem)

    @pl.when(tile == 0)
    def _():
        _load_weight_bf16(win_hbm.at[0], win_ref, win_stage, win_sem)
        _load_weight_bf16(wout_hbm.at[0], wout_ref, wout_stage, wout_sem)

    @pl.when(pl.program_id(1) == 0)
    def _():
        pbuf[0:B_HALO, :] = jnp.zeros((B_HALO, D_B), F32)

    _sc_init(sc, tile)
    _sc_pending(sc, "wait")

    prev = {}
    active = (tile > 0).astype(I32)

    x = x_ref[0]
    mod = mod_ref[0]
    sh1, sc1, g1 = mod[0:1], mod[1:2], mod[2:3]
    h = _rms_mod(x, gmix_ref[...], sh1, sc1)
    hb16 = h.astype(BF16)
    zcol = lambda j: jnp.dot(hb16, win_ref[:, j * D_A:(j + 1) * D_A], preferred_element_type=F32)

    v = _layer_norm(_gelu(zcol(1)), lng_ref[...], lnb_ref[...]).astype(BF16)
    _sc_route(sc, prev)
    u = _gelu(zcol(0))
    tril = (lax.broadcasted_iota(I32, (CHUNK, CHUNK), 0)
            >= lax.broadcasted_iota(I32, (CHUNK, CHUNK), 1))
    chunks = []
    for n in range(ts // CHUNK):
        heads = []
        for hd in range(A_HEADS):
            w = jnp.where(tril, sw_ref[hd], 0.0).astype(BF16)
            vb = v[n * CHUNK:(n + 1) * CHUNK, hd * A_HEAD_DIM:(hd + 1) * A_HEAD_DIM]
            heads.append(jnp.dot(w, vb, preferred_element_type=F32) + sb_ref[hd])
        chunks.append(jnp.concatenate(heads, axis=1))
    y_a = u * jnp.concatenate(chunks, axis=0)
    _sc_rank(sc, prev, active)

    p = zcol(3) * zcol(4)
    pbuf[B_HALO:B_HALO + ts, :] = p
    cw = cw_ref[...]
    conv = (cw[0:1] * pbuf[B_HALO - 2:B_HALO - 2 + ts, :]
            + cw[1:2] * pbuf[B_HALO - 1:B_HALO - 1 + ts, :]
            + cw[2:3] * p)
    pbuf[0:B_HALO, :] = p[ts - B_HALO:ts, :]
    y_b = zcol(2) * conv
    _sc_sort(sc, prev)

    ycat = jnp.concatenate([y_a, y_b], axis=1).astype(BF16)
    y = jnp.dot(ycat, wout_ref[...], preferred_element_type=F32)
    x1 = x + g1 * y
    x1_ref[0] = x1
    h2buf[...] = _prenorm(x1, mod, gffn_ref)
    _sc_start(sc, prev, jnp.maximum(tile - 1, 0))

    @pl.when(tile == n_tiles - 1)
    def _():
        _sc_pending(sc, "wait")
        last = {}
        _sc_route(sc, last)
        _sc_rank(sc, last, 1)
        _sc_sort(sc, last)
        _sc_start(sc, last, tile)
        _sc_finish(sc)


def _shift_down(part, r):
    if r == 0:
        return part[SUBLANES:]
    tiles = part.reshape(part.shape[0] // SUBLANES, SUBLANES, part.shape[1])
    rolled = pltpu.roll(tiles, r, axis=1)
    low = lax.broadcasted_iota(I32, (1, SUBLANES, part.shape[1]), 1) < r
    merged = jnp.where(low, rolled[:-1], rolled[1:])
    return merged.reshape(part.shape[0] - SUBLANES, part.shape[1])


def _causal_dw_conv(cbuf, w_ref, obuf, row0, nrows, col0, ncols):
    n_a = -(-C_CONV // SUBLANES)
    for t0 in range(row0, row0 + nrows, CONV_ROWS):
        for c0 in range(col0, col0 + ncols, LANES):
            cols = slice(c0, c0 + LANES)
            out = None
            for r in range(SUBLANES):
                part = None
                for a in range(n_a):
                    k = C_CONV - 1 - SUBLANES * a - r
                    if k < 0:
                        continue
                    r0 = t0 + C_HALO - SUBLANES * (a + 1)
                    term = w_ref[k:k + 1, cols] * cbuf[r0:r0 + CONV_ROWS + SUBLANES, cols]
                    part = term if part is None else part + term
                piece = _shift_down(part, r)
                out = piece if out is None else out + piece
            obuf[t0:t0 + CONV_ROWS, cols] = out


def _conformer_kernel(psegb_ref, psegn_ref,
                      xp_ref, ys_ref, pq_ref, pmod_ref, mod_ref, w1_hbm, w2_hbm, gmix_ref, gffn_ref,
                      b1_ref, dww_ref, dwb_ref, lng_ref, lnb_ref, b2_ref, rw_ref, rb_ref,
                      x1_ref, hs_ref, q_ref, segb_ref, segn_ref, fill_ref,
                      cbuf, obuf, ybuf, gsem, w1_ref, w1_stage, w1_sem, w2_ref, w2_stage, w2_sem,
                      h2buf, rowbuf, tri, zbuf, fill_s, pend_s, ssem, zsem):
    ts = xp_ref.shape[1]
    tile = pl.program_id(0) * pl.num_programs(1) + pl.program_id(1)
    n_tiles = pl.num_programs(0) * pl.num_programs(1)
    sc = _ScatterRefs(gffn_ref, rw_ref, rb_ref, hs_ref, q_ref, segb_ref, segn_ref, fill_ref,
                      h2buf, rowbuf, tri, zbuf, fill_s, pend_s, ssem, zsem)

    @pl.when(tile == 0)
    def _():
        _load_weight_bf16(w1_hbm.at[0], w1_ref, w1_stage, w1_sem)
        _load_weight_bf16(w2_hbm.at[0], w2_ref, w2_stage, w2_sem)

    @pl.when(pl.program_id(1) == 0)
    def _():
        cbuf[0:C_HALO, :] = jnp.zeros((C_HALO, D_C), F32)

    _sc_init(sc, tile)
    _sc_pending(sc, "wait")
    y_prev = _gather_prev_ffn(tile, n_tiles, psegb_ref, psegn_ref, ys_ref, pq_ref, ybuf, gsem)

    prev = {}
    active = (tile > 0).astype(I32)
    deferred = [lambda: _sc_route(sc, prev), lambda: _sc_rank(sc, prev, active),
                lambda: _sc_sort(sc, prev)]

    x = xp_ref[0] + pmod_ref[0][5:6] * y_prev
    mod = mod_ref[0]
    sh1, sc1, g1 = mod[0:1], mod[1:2], mod[2:3]
    h = _rms_mod(x, gmix_ref[...], sh1, sc1)
    hb = h.astype(BF16)
    for k, c0 in enumerate(range(0, D_C, GLU_COLS)):
        ca, cg = slice(c0, c0 + GLU_COLS), slice(D_C + c0, D_C + c0 + GLU_COLS)
        za = jnp.dot(hb, w1_ref[:, ca], preferred_element_type=F32) + b1_ref[:, ca]
        zg = jnp.dot(hb, w1_ref[:, cg], preferred_element_type=F32) + b1_ref[:, cg]
        cbuf[C_HALO:C_HALO + ts, ca] = za * jax.nn.sigmoid(zg)
        if k < len(deferred):
            deferred[k]()
        _causal_dw_conv(cbuf, dww_ref, obuf, 0, ts, c0, GLU_COLS)
    cbuf[0:C_HALO, :] = cbuf[ts:ts + C_HALO, :]

    zc = _layer_norm(obuf[...] + dwb_ref[...], lng_ref[...], lnb_ref[...])
    zc = zc * jax.nn.sigmoid(zc)
    y = jnp.dot(zc.astype(BF16), w2_ref[...], preferred_element_type=F32) + b2_ref[...]
    x1 = x + g1 * y
    x1_ref[0] = x1
    h2buf[...] = _prenorm(x1, mod, gffn_ref)
    _sc_start(sc, prev, jnp.maximum(tile - 1, 0))

    @pl.when(tile == n_tiles - 1)
    def _():
        _sc_pending(sc, "wait")
        last = {}
        _sc_route(sc, last)
        _sc_rank(sc, last, 1)
        _sc_sort(sc, last)
        _sc_start(sc, last, tile)
        _sc_finish(sc)


def _final_kernel(psegb_ref, psegn_ref, xp_ref, ys_ref, pq_ref, pmod_ref, fg_ref, o_ref, ybuf, gsem):
    tile = pl.program_id(0) * pl.num_programs(1) + pl.program_id(1)
    n_tiles = pl.num_programs(0) * pl.num_programs(1)
    y_prev = _gather_prev_ffn(tile, n_tiles, psegb_ref, psegn_ref, ys_ref, pq_ref, ybuf, gsem)
    x = xp_ref[0] + pmod_ref[0][5:6] * y_prev
    o_ref[0] = _rms(x, fg_ref[...])


def _scatter_outputs(bsz, seq, d):
    t = bsz * seq
    n_tiles = t // SEQ_TILE
    smem = pl.BlockSpec(memory_space=pltpu.SMEM)
    specs = [pl.BlockSpec((1, SEQ_TILE, d), lambda b, s, *_: (b, s, 0)),
             pl.BlockSpec(memory_space=pl.ANY),
             pl.BlockSpec((n_tiles, 1, SEQ_TILE), lambda b, s, *_: (0, 0, 0)),
             smem, smem, smem]
    shapes = [jax.ShapeDtypeStruct((bsz, seq, d), F32),
              jax.ShapeDtypeStruct((N_GROUPS * _region_rows(t), ROW_W), BF16),
              jax.ShapeDtypeStruct((n_tiles, 1, SEQ_TILE), I32),
              jax.ShapeDtypeStruct((n_tiles * N_GROUPS,), I32),
              jax.ShapeDtypeStruct((n_tiles * N_GROUPS,), I32),
              jax.ShapeDtypeStruct((N_GROUPS,), I32)]
    scratch = [pltpu.VMEM((SEQ_TILE, d), F32),
               pltpu.VMEM((SORT_ROWS, ROW_W), BF16),
               pltpu.VMEM((SEQ_TILE, SEQ_TILE), BF16),
               pltpu.VMEM((MOE_TILE, ROW_W), BF16),
               pltpu.SMEM((N_GROUPS,), I32),
               pltpu.SMEM((2 * N_GROUPS,), I32),
               pltpu.SemaphoreType.DMA(()),
               pltpu.SemaphoreType.DMA(())]
    return specs, shapes, scratch


def _gather_inputs(bsz, seq, d):
    nseq = seq // SEQ_TILE
    specs = [pl.BlockSpec((1, SEQ_TILE, d), lambda b, s, *_: (b, s, 0)),
             pl.BlockSpec(memory_space=pl.ANY),
             pl.BlockSpec((1, 1, SEQ_TILE), lambda b, s, *_: (b * nseq + s, 0, 0)),
             pl.BlockSpec((1, N_MOD, d), lambda b, s, *_: (b, 0, 0))]
    scratch = [pltpu.VMEM((2, SORT_ROWS, d), BF16), pltpu.SemaphoreType.DMA((2,))]
    return specs, scratch


def _const_specs(consts):
    return [pl.BlockSpec(a.shape, lambda b, s, *_, nd=a.ndim: (0,) * nd) for a in consts]


_PARAMS = pltpu.CompilerParams(dimension_semantics=("arbitrary", "arbitrary"),
                               vmem_limit_bytes=VMEM_LIMIT)


def _hybrid_call(x, mod_l, w_in, w_out, consts):
    bsz, seq, d = x.shape
    out_specs, out_shapes, sc_scratch = _scatter_outputs(bsz, seq, d)
    hbm = pl.BlockSpec(memory_space=pl.ANY)
    return pl.pallas_call(
        _hybrid_kernel,
        grid=(bsz, seq // SEQ_TILE),
        in_specs=[pl.BlockSpec((1, SEQ_TILE, d), lambda b, s: (b, s, 0)),
                  pl.BlockSpec((1, N_MOD, d), lambda b, s: (b, 0, 0)), hbm, hbm]
                 + _const_specs(consts),
        out_specs=out_specs,
        out_shape=out_shapes,
        scratch_shapes=[pltpu.VMEM((SEQ_TILE + B_HALO, D_B), F32)]
                       + _weight_scratch(*w_in.shape[1:]) + _weight_scratch(*w_out.shape[1:])
                       + sc_scratch,
        compiler_params=_PARAMS,
        name="mixer_hybrid",
    )(x, mod_l, w_in, w_out, *consts)


def _conformer_call(prev, mod_prev, mod_l, w1, w2, consts):
    x1p, ysp, qp, segbp, segnp = prev
    bsz, seq, d = x1p.shape
    g_specs, g_scratch = _gather_inputs(bsz, seq, d)
    out_specs, out_shapes, sc_scratch = _scatter_outputs(bsz, seq, d)
    hbm = pl.BlockSpec(memory_space=pl.ANY)
    grid_spec = pltpu.PrefetchScalarGridSpec(
        num_scalar_prefetch=2,
        grid=(bsz, seq // SEQ_TILE),
        in_specs=g_specs + [pl.BlockSpec((1, N_MOD, d), lambda b, s, *_: (b, 0, 0)), hbm, hbm]
                 + _const_specs(consts),
        out_specs=out_specs,
        scratch_shapes=[pltpu.VMEM((SEQ_TILE + C_HALO, D_C), F32), pltpu.VMEM((SEQ_TILE, D_C), F32)]
                       + g_scratch + _weight_scratch(*w1.shape[1:]) + _weight_scratch(*w2.shape[1:])
                       + sc_scratch,
    )
    return pl.pallas_call(
        _conformer_kernel, grid_spec=grid_spec, out_shape=out_shapes,
        compiler_params=_PARAMS, name="mixer_conformer",
    )(segbp, segnp, x1p, ysp, qp, mod_prev, mod_l, w1, w2, *consts)


def _final_call(prev, mod_prev, final_g):
    x1p, ysp, qp, segbp, segnp = prev
    bsz, seq, d = x1p.shape
    g_specs, g_scratch = _gather_inputs(bsz, seq, d)
    grid_spec = pltpu.PrefetchScalarGridSpec(
        num_scalar_prefetch=2,
        grid=(bsz, seq // SEQ_TILE),
        in_specs=g_specs + [pl.BlockSpec((1, d), lambda b, s, *_: (0, 0))],
        out_specs=pl.BlockSpec((1, SEQ_TILE, d), lambda b, s, *_: (b, s, 0)),
        scratch_shapes=g_scratch,
    )
    return pl.pallas_call(
        _final_kernel, grid_spec=grid_spec,
        out_shape=jax.ShapeDtypeStruct((bsz, seq, d), F32),
        compiler_params=_PARAMS, name="final_norm",
    )(segbp, segnp, x1p, ysp, qp, mod_prev, final_g.reshape(1, d))


def _moe_kernel(blk_ref, grp_ref, valid_ref, fresh_ref, hs_ref, wg_ref, wu_ref, wd_ref, ys_ref,
                wg_bf, wu_bf, wd_bf):
    i = pl.program_id(0)

    @pl.when(fresh_ref[i] != 0)
    def _():
        wg_bf[...] = wg_ref[0].astype(BF16)
        wu_bf[...] = wu_ref[0].astype(BF16)
        wd_bf[...] = wd_ref[0].astype(BF16)

    @pl.when(valid_ref[i] != 0)
    def _():
        rows = hs_ref[...]
        h = rows[:, :D_MODEL]
        m = rows[:, D_MODEL:].astype(F32)
        ne = EXPERTS_PER_GROUP
        y = None
        for j in range(ne):
            gate = m[:, j:j + 1] + (m[:, ne + j:ne + j + 1] + m[:, 2 * ne + j:2 * ne + j + 1])
            a = jnp.dot(h, wg_bf[j], preferred_element_type=F32)
            a = a * jax.nn.sigmoid(a) * jnp.dot(h, wu_bf[j], preferred_element_type=F32)
            yj = jnp.dot((a * gate).astype(BF16), wd_bf[j], preferred_element_type=F32)
            y = yj if y is None else y + yj
        ys_ref[...] = y.astype(BF16)


def _moe_items(fill, region_rows, n_items):
    tiles = (fill + (MOE_TILE - 1)) // MOE_TILE
    ends = jnp.cumsum(tiles)
    i = jnp.minimum(jnp.arange(n_items, dtype=I32), jnp.maximum(ends[-1] - 1, 0))
    grp = jnp.sum((i[:, None] >= ends[None, :]).astype(I32), axis=1)
    blk = grp * (region_rows // MOE_TILE) + i - (ends - tiles)[grp]
    valid = (jnp.arange(n_items, dtype=I32) < ends[-1]).astype(I32)
    fresh = jnp.concatenate([jnp.ones((1,), I32), (grp[1:] != grp[:-1]).astype(I32)])
    return blk.astype(I32), grp.astype(I32), valid, fresh


def _moe_call(hs, fill, layer, wg, wu, wd, t):
    region_rows = hs.shape[0] // N_GROUPS
    d = D_MODEL
    ne = EXPERTS_PER_GROUP
    n_items = -(-(t + (t // SEQ_TILE) * N_GROUPS * (ROW_ALIGN - 1)) // MOE_TILE) + N_GROUPS
    items = _moe_items(fill, region_rows, n_items)
    w_idx = lambda i, blk, grp, valid, fresh: (layer * N_GROUPS + grp[i], 0, 0, 0)
    row_idx = lambda i, blk, grp, valid, fresh: (blk[i], 0)
    grid_spec = pltpu.PrefetchScalarGridSpec(
        num_scalar_prefetch=len(items),
        grid=(n_items,),
        in_specs=[pl.BlockSpec((MOE_TILE, ROW_W), row_idx),
                  pl.BlockSpec((1, ne, d, D_EXPERT), w_idx),
                  pl.BlockSpec((1, ne, d, D_EXPERT), w_idx),
                  pl.BlockSpec((1, ne, D_EXPERT, d), w_idx)],
        out_specs=pl.BlockSpec((MOE_TILE, d), row_idx),
        scratch_shapes=[pltpu.VMEM((ne, d, D_EXPERT), BF16), pltpu.VMEM((ne, d, D_EXPERT), BF16),
                        pltpu.VMEM((ne, D_EXPERT, d), BF16)],
    )
    return pl.pallas_call(
        _moe_kernel, grid_spec=grid_spec,
        out_shape=jax.ShapeDtypeStruct((hs.shape[0], d), BF16),
        compiler_params=pltpu.CompilerParams(dimension_semantics=("arbitrary",),
                                             vmem_limit_bytes=VMEM_LIMIT),
        name="moe",
    )(*items, hs, wg, wu, wd)


def kernel(x, c, ada_w, ada_b, norm_mix_g, norm_ffn_g, hyb_w_in, sgu_ln_g, sgu_ln_b, sgu_w, sgu_b,
           sconv_w, hyb_w_out, conf_w1, conf_b1, conf_dw_w, conf_dw_b, conf_ln_g, conf_ln_b,
           conf_w2, conf_b2, router_w, router_bias, moe_w_gate, moe_w_up, moe_w_down, final_norm_g):
    depth = ada_w.shape[0]
    bsz, seq, d = x.shape
    t = bsz * seq
    assert depth == 2 and seq % SEQ_TILE == 0
    mod = _ada_mod(c, ada_w, ada_b).reshape(depth, bsz, N_MOD, d)

    perm = jnp.arange(N_EXPERTS).reshape(N_GROUPS, EXPERTS_PER_GROUP).T.reshape(-1)
    rw_t = router_w.T[perm]
    rb_t = router_bias[perm].reshape(N_EXPERTS, 1)

    ne = EXPERTS_PER_GROUP
    wg = moe_w_gate.reshape(depth * N_GROUPS, ne, d, D_EXPERT)
    wu = moe_w_up.reshape(depth * N_GROUPS, ne, d, D_EXPERT)
    wd = moe_w_down.reshape(depth * N_GROUPS, ne, D_EXPERT, d)

    row = lambda a: a.reshape(1, -1)
    consts0 = [row(norm_mix_g[0]), row(norm_ffn_g[0]), row(sgu_ln_g[0]), row(sgu_ln_b[0]), sgu_w[0],
               sgu_b[0].reshape(A_HEADS, CHUNK, 1), sconv_w[0], rw_t, rb_t]
    x1, hs, q, segb, segn, fill = _hybrid_call(x, mod[0], hyb_w_in, hyb_w_out, consts0)
    ys = _moe_call(hs, fill, 0, wg, wu, wd, t)

    consts1 = [row(norm_mix_g[1]), row(norm_ffn_g[1]), row(conf_b1[0]), conf_dw_w[0],
               row(conf_dw_b[0]), row(conf_ln_g[0]), row(conf_ln_b[0]), row(conf_b2[0]), rw_t, rb_t]
    x1, hs, q1, segb1, segn1, fill1 = _conformer_call((x1, ys, q, segb, segn), mod[0], mod[1],
                                                      conf_w1, conf_w2, consts1)
    ys = _moe_call(hs, fill1, 1, wg, wu, wd, t)
    return _final_call((x1, ys, q1, segb1, segn1), mod[1], final_norm_g)
```

```python
import functools

import jax
import jax.numpy as jnp
from jax import lax
from jax.experimental import pallas as pl
from jax.experimental.pallas import tpu as pltpu

D_MODEL = 1024
CHUNK = 128
A_HEADS = 4
A_HEAD_DIM = 128
D_A = A_HEADS * A_HEAD_DIM
D_B = 512
B_CONV = 3
D_IN_HYB = 2 * D_A + 3 * D_B
D_C = D_MODEL
C_CONV = 31
N_EXPERTS = 16
N_GROUPS = 4
EXPERTS_PER_GROUP = N_EXPERTS // N_GROUPS
D_EXPERT = 256
N_MOD = 6
EPS = 1e-6

SUBLANES = 8
LANES = 128
ROW_ALIGN = 16

SEQ_TILE = 512
MOE_TILE = 512
MOD_TILE = 1024
W_STAGE_ROWS = 128
C_HALO = 32
B_HALO = 8
CONV_ROWS = 128
GLU_COLS = 256
SORT_ROWS = SEQ_TILE + N_GROUPS * ROW_ALIGN
ROW_W = D_MODEL + LANES
VMEM_LIMIT = 56 * 1024 * 1024

F32 = jnp.float32
BF16 = jnp.bfloat16
I32 = jnp.int32


def _round_up(n, m):
    return -(-n // m) * m


def _region_rows(t):
    return _round_up(t + (t // SEQ_TILE) * ROW_ALIGN, MOE_TILE)


def _split_bf16(a):
    hi = a.astype(BF16)
    lo = (a - hi.astype(F32)).astype(BF16)
    return hi, lo


def _dot3(a, b):
    ah, al = _split_bf16(a)
    bh, bl = _split_bf16(b)
    d = functools.partial(jnp.dot, preferred_element_type=F32)
    return d(ah, bh) + (d(ah, bl) + d(al, bh))


def _rms(x, g):
    return x * lax.rsqrt(jnp.mean(x * x, axis=-1, keepdims=True) + EPS) * g


def _rms_mod(x, g, shift, scale):
    return x * lax.rsqrt(jnp.mean(x * x, axis=-1, keepdims=True) + EPS) * (g * (1.0 + scale)) + shift


def _gelu(x):
    return 0.5 * x * (1.0 + lax.erf(x * (2.0 ** -0.5)))


def _layer_norm(x, g, b):
    mu = jnp.mean(x, axis=-1, keepdims=True)
    xc = x - mu
    var = jnp.mean(xc * xc, axis=-1, keepdims=True)
    return xc * lax.rsqrt(var + EPS) * g + b


def _mod_kernel(c_ref, w_ref, b_ref, o_ref):
    c = c_ref[...]
    c_act = c * jax.nn.sigmoid(c)
    o_ref[0] = _dot3(c_act, w_ref[0]) + b_ref[0]


def _ada_mod(c, ada_w, ada_b):
    depth, d, n = ada_w.shape
    bsz = c.shape[0]
    return pl.pallas_call(
        _mod_kernel,
        grid=(depth, n // MOD_TILE),
        in_specs=[
            pl.BlockSpec((bsz, d), lambda l, j: (0, 0)),
            pl.BlockSpec((1, d, MOD_TILE), lambda l, j: (l, 0, j)),
            pl.BlockSpec((1, 1, MOD_TILE), lambda l, j: (l, 0, j)),
        ],
        out_specs=pl.BlockSpec((1, bsz, MOD_TILE), lambda l, j: (l, 0, j)),
        out_shape=jax.ShapeDtypeStruct((depth, bsz, n), F32),
        compiler_params=pltpu.CompilerParams(
            dimension_semantics=("arbitrary", "arbitrary"), vmem_limit_bytes=VMEM_LIMIT),
        name="ada_mod",
    )(c, ada_w, ada_b.reshape(depth, 1, n))


def _route(h2, rw_ref, rb_ref):
    wh, wl = _split_bf16(rw_ref[...])
    hh, hl = _split_bf16(h2)
    nt = functools.partial(lax.dot_general, dimension_numbers=(((1,), (1,)), ((), ())),
                           preferred_element_type=F32)
    logits = nt(wh, hh) + (nt(wh, hl) + nt(wl, hh))
    scores = jax.nn.sigmoid(logits)
    biased = scores + rb_ref[...]
    ng, ne = N_GROUPS, EXPERTS_PER_GROUP
    b = [biased[j * ng:(j + 1) * ng] for j in range(ne)]
    s = [scores[j * ng:(j + 1) * ng] for j in range(ne)]
    gscore = None
    for i in range(ne):
        for j in range(i + 1, ne):
            pair = b[i] + b[j]
            gscore = pair if gscore is None else jnp.maximum(gscore, pair)
    gid = lax.broadcasted_iota(I32, gscore.shape, 0)
    gmax = jnp.max(gscore, axis=0, keepdims=True)
    best = jnp.min(jnp.where(gscore == gmax, gid, ng), axis=0, keepdims=True)
    onb = gid == best
    cb = [jnp.sum(jnp.where(onb, b[j], 0.0), axis=0, keepdims=True) for j in range(ne)]
    cs = [jnp.sum(jnp.where(onb, s[j], 0.0), axis=0, keepdims=True) for j in range(ne)]
    sel = []
    for j in range(ne):
        rank = jnp.zeros_like(best)
        for i in range(ne):
            if i == j:
                continue
            ahead = (cb[i] >= cb[j]) if i < j else (cb[i] > cb[j])
            rank = rank + ahead.astype(I32)
        sel.append(rank < 2)
    wsum = sum(jnp.where(sel[j], cs[j], 0.0) for j in range(ne))
    gates = jnp.concatenate([jnp.where(sel[j], cs[j], 0.0) / wsum for j in range(ne)], axis=0)
    return gates, jnp.where(onb, 1.0, 0.0)


def _sort_matrix(q):
    rows = lax.broadcasted_iota(I32, (SORT_ROWS, q.shape[1]), 0)
    return jnp.where(rows == q, 1.0, 0.0).astype(BF16)


def _run_copies(op, src_of, dst_of, sem, nrows, bases, region_rows):
    if op == "wait":
        total = pl.multiple_of(sum(nrows[1:], nrows[0]), ROW_ALIGN)

        @pl.when(total > 0)
        def _():
            pltpu.make_async_copy(src_of(0, 0, total), dst_of(0, 0, total), sem).wait()
        return
    tile_off = 0
    for g in range(N_GROUPS):
        n = pl.multiple_of(nrows[g], ROW_ALIGN)
        a = pl.multiple_of(tile_off, ROW_ALIGN)
        b = pl.multiple_of(g * region_rows + bases[g], ROW_ALIGN)

        @pl.when(n > 0)
        def _(a=a, b=b, n=n):
            pltpu.make_async_copy(src_of(a, b, n), dst_of(a, b, n), sem).start()

        tile_off = tile_off + nrows[g]


class _ScatterRefs:
    def __init__(self, gffn_ref, rw_ref, rb_ref, hs_ref, q_ref, segb_ref, segn_ref, fill_ref,
                 h2buf, rowbuf, tri, zbuf, fill_s, pend_s, ssem, zsem):
        self.__dict__.update(locals())
        self.region_rows = hs_ref.shape[0] // N_GROUPS
        self.dst_of = lambda a, b, size: hs_ref.at[pl.ds(b, size)]


def _sc_init(sc, tile):
    @pl.when(tile == 0)
    def _():
        for g in range(N_GROUPS):
            sc.fill_s[g] = 0
        for i in range(sc.pend_s.shape[0]):
            sc.pend_s[i] = 0
        ts = sc.tri.shape[0]
        k = lax.broadcasted_iota(I32, (ts, ts), 0)
        t = lax.broadcasted_iota(I32, (ts, ts), 1)
        sc.tri[...] = jnp.where(k <= t, 1.0, 0.0).astype(BF16)
        sc.h2buf[...] = jnp.zeros_like(sc.h2buf)


def _sc_pending(sc, op, slot):
    rec = slot * (2 * N_GROUPS)
    src_of = lambda a, b, size: sc.rowbuf.at[slot, pl.ds(a, size)]
    _run_copies(op, src_of, sc.dst_of, sc.ssem.at[slot], [sc.pend_s[rec + g] for g in range(N_GROUPS)],
                [sc.pend_s[rec + N_GROUPS + g] for g in range(N_GROUPS)], sc.region_rows)


def _prenorm(x1, mod, gffn_ref):
    sh2, sc2 = mod[3:4], mod[4:5]
    return _rms_mod(x1, gffn_ref[...], sh2, sc2)


def _sc_route(sc, st):
    st["h2"] = sc.h2buf[...]
    st["gates"], st["onehot"] = _route(st["h2"], sc.rw_ref, sc.rb_ref)


def _sc_rank(sc, st, active):
    onehot = st["onehot"]
    ts = onehot.shape[1]
    oh8 = jnp.concatenate([onehot, jnp.zeros((SUBLANES - N_GROUPS, ts), F32)], axis=0).astype(BF16)
    csum = jnp.dot(oh8, sc.tri[...], preferred_element_type=F32)[:N_GROUPS]
    counts = [csum[g, ts - 1].astype(I32) * active for g in range(N_GROUPS)]
    st["nrows"] = [(c + (ROW_ALIGN - 1)) & ~(ROW_ALIGN - 1) for c in counts]
    q = jnp.zeros((1, ts), F32)
    tile_off = 0
    for g in range(N_GROUPS):
        q = q + onehot[g:g + 1] * (csum[g:g + 1] - 1.0 + tile_off.astype(F32)
                                   if g else csum[g:g + 1] - 1.0)
        tile_off = tile_off + st["nrows"][g]
    st["q"] = q.astype(I32)


def _sc_sort(sc, st, slot):
    gates = st["gates"]
    ts = gates.shape[1]
    hi = gates.astype(BF16).astype(F32)
    mid = (gates - hi).astype(BF16).astype(F32)
    lo = (gates - hi - mid).astype(BF16).astype(F32)
    meta_t = jnp.concatenate(
        [hi, mid, lo, jnp.zeros((LANES - 3 * EXPERTS_PER_GROUP, ts), F32)], axis=0)
    rows = jnp.concatenate([st["h2"].astype(BF16), meta_t.T.astype(BF16)], axis=1)
    sorted_rows = jnp.dot(_sort_matrix(st["q"]), rows, preferred_element_type=F32)
    sc.rowbuf[slot] = sorted_rows.astype(BF16)


def _sc_start(sc, st, t_idx, slot):
    sc.q_ref[t_idx] = st["q"]
    nrows = st["nrows"]
    rec = slot * (2 * N_GROUPS)
    for g in range(N_GROUPS):
        base = sc.fill_s[g]
        sc.pend_s[rec + g] = nrows[g]
        sc.pend_s[rec + N_GROUPS + g] = base
        sc.segn_ref[t_idx * N_GROUPS + g] = nrows[g]
        sc.segb_ref[t_idx * N_GROUPS + g] = base
        sc.fill_s[g] = base + nrows[g]
    _sc_pending(sc, "start", slot)


def _sc_finish(sc):
    _sc_pending(sc, "wait", 0)
    _sc_pending(sc, "wait", 1)
    sc.zbuf[...] = jnp.zeros_like(sc.zbuf)
    for op in ("start", "wait"):
        for g in range(N_GROUPS):
            fill = sc.fill_s[g]
            n = pl.multiple_of((-fill) & (MOE_TILE - 1), ROW_ALIGN)
            b = pl.multiple_of(g * sc.region_rows + fill, ROW_ALIGN)

            @pl.when(n > 0)
            def _(n=n, b=b, op=op):
                cp = pltpu.make_async_copy(sc.zbuf.at[pl.ds(0, n)], sc.hs_ref.at[pl.ds(b, n)], sc.zsem)
                getattr(cp, op)()
    for g in range(N_GROUPS):
        sc.fill_ref[g] = sc.fill_s[g]


def _gather_prev_ffn(tile, n_tiles, segb_ref, segn_ref, ys_ref, q_ref, ybuf, gsem):
    region_rows = ys_ref.shape[0] // N_GROUPS

    def copies(op, t, slot):
        nrows = [segn_ref[t * N_GROUPS + g] for g in range(N_GROUPS)]
        bases = [segb_ref[t * N_GROUPS + g] for g in range(N_GROUPS)]
        src_of = lambda a, b, size: ys_ref.at[pl.ds(b, size)]
        dst_of = lambda a, b, size: ybuf.at[slot, pl.ds(a, size)]
        _run_copies(op, src_of, dst_of, gsem.at[slot], nrows, bases, region_rows)

    def issue(t, slot):
        ybuf[slot, SEQ_TILE:SORT_ROWS, :] = jnp.zeros((SORT_ROWS - SEQ_TILE, D_MODEL), BF16)
        copies("start", t, slot)

    @pl.when(tile == 0)
    def _():
        issue(tile, 0)

    @pl.when(tile + 1 < n_tiles)
    def _():
        issue(tile + 1, (tile + 1) % 2)

    slot = tile % 2
    copies("wait", tile, slot)
    p = _sort_matrix(q_ref[0])
    return lax.dot_general(p, ybuf[slot], (((0,), (0,)), ((), ())), preferred_element_type=F32)


def _load_weight_bf16(w_hbm, w_vmem, stage, sem):
    rows = stage.shape[1]
    n = w_vmem.shape[0] // rows
    copy = lambda c: pltpu.make_async_copy(w_hbm.at[pl.ds(c * rows, rows)], stage.at[c % 2],
                                           sem.at[c % 2])
    copy(0).start()
    for c in range(n):
        if c + 1 < n:
            copy(c + 1).start()
        copy(c).wait()
        w_vmem[c * rows:(c + 1) * rows, :] = stage[c % 2].astype(BF16)


def _weight_scratch(k, n):
    return [pltpu.VMEM((k, n), BF16), pltpu.VMEM((2, W_STAGE_ROWS, n), F32),
            pltpu.SemaphoreType.DMA((2,))]


def _hybrid_kernel(x_ref, mod_ref, win_hbm, wout_hbm, gmix_ref, gffn_ref, lng_ref, lnb_ref, sw_ref,
                   sb_ref, cw_ref, rw_ref, rb_ref,
                   x1_ref, hs_ref, q_ref, segb_ref, segn_ref, fill_ref,
                   pbuf, win_ref, win_stage, win_sem, wout_ref, wout_stage, wout_sem,
                   h2buf, rowbuf, tri, zbuf, fill_s, pend_s, ssem, zsem):
    ts = x_ref.shape[1]
    tile = pl.program_id(0) * pl.num_programs(1) + pl.program_id(1)
    n_tiles = pl.num_programs(0) * pl.num_programs(1)
    sc = _ScatterRefs(gffn_ref, rw_ref, rb_ref, hs_ref, q_ref, segb_ref, segn_ref, fill_ref,
                      h2buf, rowbuf, tri, zbuf, fill_s, pend_s, ssem, zsem)

    @pl.when(tile == 0)
    def _():
        _load_weight_bf16(win_hbm.at[0], win_ref, win_stage, win_sem)
        _load_weight_bf16(wout_hbm.at[0], wout_ref, wout_stage, wout_sem)

    @pl.when(pl.program_id(1) == 0)
    def _():
        pbuf[0:B_HALO, :] = jnp.zeros((B_HALO, D_B), F32)

    _sc_init(sc, tile)
    slot = tile % 2
    _sc_pending(sc, "wait", slot)

    prev = {}
    active = (tile > 0).astype(I32)

    x = x_ref[0]
    mod = mod_ref[0]
    sh1, sc1, g1 = mod[0:1], mod[1:2], mod[2:3]
    h = _rms_mod(x, gmix_ref[...], sh1, sc1)
    hb16 = h.astype(BF16)
    zcol = lambda j: jnp.dot(hb16, win_ref[:, j * D_A:(j + 1) * D_A], preferred_element_type=F32)

    v = _layer_norm(_gelu(zcol(1)), lng_ref[...], lnb_ref[...]).astype(BF16)
    _sc_route(sc, prev)
    u = _gelu(zcol(0))
    tril = (lax.broadcasted_iota(I32, (CHUNK, CHUNK), 0)
            >= lax.broadcasted_iota(I32, (CHUNK, CHUNK), 1))
    chunks = []
    for n in range(ts // CHUNK):
        heads = []
        for hd in range(A_HEADS):
            w = jnp.where(tril, sw_ref[hd], 0.0).astype(BF16)
            vb = v[n * CHUNK:(n + 1) * CHUNK, hd * A_HEAD_DIM:(hd + 1) * A_HEAD_DIM]
            heads.append(jnp.dot(w, vb, preferred_element_type=F32) + sb_ref[hd])
        chunks.append(jnp.concatenate(heads, axis=1))
    y_a = u * jnp.concatenate(chunks, axis=0)
    _sc_rank(sc, prev, active)

    p = zcol(3) * zcol(4)
    pbuf[B_HALO:B_HALO + ts, :] = p
    cw = cw_ref[...]
    conv = (cw[0:1] * pbuf[B_HALO - 2:B_HALO - 2 + ts, :]
            + cw[1:2] * pbuf[B_HALO - 1:B_HALO - 1 + ts, :]
            + cw[2:3] * p)
    pbuf[0:B_HALO, :] = p[ts - B_HALO:ts, :]
    y_b = zcol(2) * conv
    _sc_sort(sc, prev, slot)

    ycat = jnp.concatenate([y_a, y_b], axis=1).astype(BF16)
    y = jnp.dot(ycat, wout_ref[...], preferred_element_type=F32)
    x1 = x + g1 * y
    x1_ref[0] = x1
    h2buf[...] = _prenorm(x1, mod, gffn_ref)
    _sc_start(sc, prev, jnp.maximum(tile - 1, 0), slot)

    @pl.when(tile == n_tiles - 1)
    def _():
        _sc_pending(sc, "wait", 1 - slot)
        last = {}
        _sc_route(sc, last)
        _sc_rank(sc, last, 1)
        _sc_sort(sc, last, 1 - slot)
        _sc_start(sc, last, tile, 1 - slot)
        _sc_finish(sc)


def _shift_down(part, r):
    if r == 0:
        return part[SUBLANES:]
    tiles = part.reshape(part.shape[0] // SUBLANES, SUBLANES, part.shape[1])
    rolled = pltpu.roll(tiles, r, axis=1)
    low = lax.broadcasted_iota(I32, (1, SUBLANES, part.shape[1]), 1) < r
    merged = jnp.where(low, rolled[:-1], rolled[1:])
    return merged.reshape(part.shape[0] - SUBLANES, part.shape[1])


def _causal_dw_conv(cbuf, w_ref, obuf, row0, nrows, col0, ncols):
    n_a = -(-C_CONV // SUBLANES)
    for t0 in range(row0, row0 + nrows, CONV_ROWS):
        for c0 in range(col0, col0 + ncols, LANES):
            cols = slice(c0, c0 + LANES)
            out = None
            for r in range(SUBLANES):
                part = None
                for a in range(n_a):
                    k = C_CONV - 1 - SUBLANES * a - r
                    if k < 0:
                        continue
                    r0 = t0 + C_HALO - SUBLANES * (a + 1)
                    term = w_ref[k:k + 1, cols] * cbuf[r0:r0 + CONV_ROWS + SUBLANES, cols]
                    part = term if part is None else part + term
                piece = _shift_down(part, r)
                out = piece if out is None else out + piece
            obuf[t0:t0 + CONV_ROWS, cols] = out


def _conformer_kernel(psegb_ref, psegn_ref,
                      xp_ref, ys_ref, pq_ref, pmod_ref, mod_ref, w1_hbm, w2_hbm, gmix_ref, gffn_ref,
                      b1_ref, dww_ref, dwb_ref, lng_ref, lnb_ref, b2_ref, rw_ref, rb_ref,
                      x1_ref, hs_ref, q_ref, segb_ref, segn_ref, fill_ref,
                      cbuf, obuf, ybuf, gsem, w1_ref, w1_stage, w1_sem, w2_ref, w2_stage, w2_sem,
                      h2buf, rowbuf, tri, zbuf, fill_s, pend_s, ssem, zsem):
    ts = xp_ref.shape[1]
    tile = pl.program_id(0) * pl.num_programs(1) + pl.program_id(1)
    n_tiles = pl.num_programs(0) * pl.num_programs(1)
    sc = _ScatterRefs(gffn_ref, rw_ref, rb_ref, hs_ref, q_ref, segb_ref, segn_ref, fill_ref,
                      h2buf, rowbuf, tri, zbuf, fill_s, pend_s, ssem, zsem)

    @pl.when(tile == 0)
    def _():
        _load_weight_bf16(w1_hbm.at[0], w1_ref, w1_stage, w1_sem)
        _load_weight_bf16(w2_hbm.at[0], w2_ref, w2_stage, w2_sem)

    @pl.when(pl.program_id(1) == 0)
    def _():
        cbuf[0:C_HALO, :] = jnp.zeros((C_HALO, D_C), F32)

    _sc_init(sc, tile)
    slot = tile % 2
    _sc_pending(sc, "wait", slot)
    y_prev = _gather_prev_ffn(tile, n_tiles, psegb_ref, psegn_ref, ys_ref, pq_ref, ybuf, gsem)

    prev = {}
    active = (tile > 0).astype(I32)
    deferred = [lambda: _sc_route(sc, prev), lambda: _sc_rank(sc, prev, active),
                lambda: _sc_sort(sc, prev, slot)]

    x = xp_ref[0] + pmod_ref[0][5:6] * y_prev
    mod = mod_ref[0]
    sh1, sc1, g1 = mod[0:1], mod[1:2], mod[2:3]
    h = _rms_mod(x, gmix_ref[...], sh1, sc1)
    hb = h.astype(BF16)
    for k, c0 in enumerate(range(0, D_C, GLU_COLS)):
        ca, cg = slice(c0, c0 + GLU_COLS), slice(D_C + c0, D_C + c0 + GLU_COLS)
        za = jnp.dot(hb, w1_ref[:, ca], preferred_element_type=F32) + b1_ref[:, ca]
        zg = jnp.dot(hb, w1_ref[:, cg], preferred_element_type=F32) + b1_ref[:, cg]
        cbuf[C_HALO:C_HALO + ts, ca] = za * jax.nn.sigmoid(zg)
        if k < len(deferred):
            deferred[k]()
        _causal_dw_conv(cbuf, dww_ref, obuf, 0, ts, c0, GLU_COLS)
    cbuf[0:C_HALO, :] = cbuf[ts:ts + C_HALO, :]

    zc = _layer_norm(obuf[...] + dwb_ref[...], lng_ref[...], lnb_ref[...])
    zc = zc * jax.nn.sigmoid(zc)
    y = jnp.dot(zc.astype(BF16), w2_ref[...], preferred_element_type=F32) + b2_ref[...]
    x1 = x + g1 * y
    x1_ref[0] = x1
    h2buf[...] = _prenorm(x1, mod, gffn_ref)
    _sc_start(sc, prev, jnp.maximum(tile - 1, 0), slot)

    @pl.when(tile == n_tiles - 1)
    def _():
        _sc_pending(sc, "wait", 1 - slot)
        last = {}
        _sc_route(sc, last)
        _sc_rank(sc, last, 1)
        _sc_sort(sc, last, 1 - slot)
        _sc_start(sc, last, tile, 1 - slot)
        _sc_finish(sc)


def _final_kernel(psegb_ref, psegn_ref, xp_ref, ys_ref, pq_ref, pmod_ref, fg_ref, o_ref, ybuf, gsem):
    tile = pl.program_id(0) * pl.num_programs(1) + pl.program_id(1)
    n_tiles = pl.num_programs(0) * pl.num_programs(1)
    y_prev = _gather_prev_ffn(tile, n_tiles, psegb_ref, psegn_ref, ys_ref, pq_ref, ybuf, gsem)
    x = xp_ref[0] + pmod_ref[0][5:6] * y_prev
    o_ref[0] = _rms(x, fg_ref[...])


def _scatter_outputs(bsz, seq, d):
    t = bsz * seq
    n_tiles = t // SEQ_TILE
    smem = pl.BlockSpec(memory_space=pltpu.SMEM)
    specs = [pl.BlockSpec((1, SEQ_TILE, d), lambda b, s, *_: (b, s, 0)),
             pl.BlockSpec(memory_space=pl.ANY),
             pl.BlockSpec((n_tiles, 1, SEQ_TILE), lambda b, s, *_: (0, 0, 0)),
             smem, smem, smem]
    shapes = [jax.ShapeDtypeStruct((bsz, seq, d), F32),
              jax.ShapeDtypeStruct((N_GROUPS * _region_rows(t), ROW_W), BF16),
              jax.ShapeDtypeStruct((n_tiles, 1, SEQ_TILE), I32),
              jax.ShapeDtypeStruct((n_tiles * N_GROUPS,), I32),
              jax.ShapeDtypeStruct((n_tiles * N_GROUPS,), I32),
              jax.ShapeDtypeStruct((N_GROUPS,), I32)]
    scratch = [pltpu.VMEM((SEQ_TILE, d), F32),
               pltpu.VMEM((2, SORT_ROWS, ROW_W), BF16),
               pltpu.VMEM((SEQ_TILE, SEQ_TILE), BF16),
               pltpu.VMEM((MOE_TILE, ROW_W), BF16),
               pltpu.SMEM((N_GROUPS,), I32),
               pltpu.SMEM((4 * N_GROUPS,), I32),
               pltpu.SemaphoreType.DMA((2,)),
               pltpu.SemaphoreType.DMA(())]
    return specs, shapes, scratch


def _gather_inputs(bsz, seq, d):
    nseq = seq // SEQ_TILE
    specs = [pl.BlockSpec((1, SEQ_TILE, d), lambda b, s, *_: (b, s, 0)),
             pl.BlockSpec(memory_space=pl.ANY),
             pl.BlockSpec((1, 1, SEQ_TILE), lambda b, s, *_: (b * nseq + s, 0, 0)),
             pl.BlockSpec((1, N_MOD, d), lambda b, s, *_: (b, 0, 0))]
    scratch = [pltpu.VMEM((2, SORT_ROWS, d), BF16), pltpu.SemaphoreType.DMA((2,))]
    return specs, scratch


def _const_specs(consts):
    return [pl.BlockSpec(a.shape, lambda b, s, *_, nd=a.ndim: (0,) * nd) for a in consts]


_PARAMS = pltpu.CompilerParams(dimension_semantics=("arbitrary", "arbitrary"),
                               vmem_limit_bytes=VMEM_LIMIT)


def _hybrid_call(x, mod_l, w_in, w_out, consts):
    bsz, seq, d = x.shape
    out_specs, out_shapes, sc_scratch = _scatter_outputs(bsz, seq, d)
    hbm = pl.BlockSpec(memory_space=pl.ANY)
    return pl.pallas_call(
        _hybrid_kernel,
        grid=(bsz, seq // SEQ_TILE),
        in_specs=[pl.BlockSpec((1, SEQ_TILE, d), lambda b, s: (b, s, 0)),
                  pl.BlockSpec((1, N_MOD, d), lambda b, s: (b, 0, 0)), hbm, hbm]
                 + _const_specs(consts),
        out_specs=out_specs,
        out_shape=out_shapes,
        scratch_shapes=[pltpu.VMEM((SEQ_TILE + B_HALO, D_B), F32)]
                       + _weight_scratch(*w_in.shape[1:]) + _weight_scratch(*w_out.shape[1:])
                       + sc_scratch,
        compiler_params=_PARAMS,
        name="mixer_hybrid",
    )(x, mod_l, w_in, w_out, *consts)


def _conformer_call(prev, mod_prev, mod_l, w1, w2, consts):
    x1p, ysp, qp, segbp, segnp = prev
    bsz, seq, d = x1p.shape
    g_specs, g_scratch = _gather_inputs(bsz, seq, d)
    out_specs, out_shapes, sc_scratch = _scatter_outputs(bsz, seq, d)
    hbm = pl.BlockSpec(memory_space=pl.ANY)
    grid_spec = pltpu.PrefetchScalarGridSpec(
        num_scalar_prefetch=2,
        grid=(bsz, seq // SEQ_TILE),
        in_specs=g_specs + [pl.BlockSpec((1, N_MOD, d), lambda b, s, *_: (b, 0, 0)), hbm, hbm]
                 + _const_specs(consts),
        out_specs=out_specs,
        scratch_shapes=[pltpu.VMEM((SEQ_TILE + C_HALO, D_C), F32), pltpu.VMEM((SEQ_TILE, D_C), F32)]
                       + g_scratch + _weight_scratch(*w1.shape[1:]) + _weight_scratch(*w2.shape[1:])
                       + sc_scratch,
    )
    return pl.pallas_call(
        _conformer_kernel, grid_spec=grid_spec, out_shape=out_shapes,
        compiler_params=_PARAMS, name="mixer_conformer",
    )(segbp, segnp, x1p, ysp, qp, mod_prev, mod_l, w1, w2, *consts)


def _final_call(prev, mod_prev, final_g):
    x1p, ysp, qp, segbp, segnp = prev
    bsz, seq, d = x1p.shape
    g_specs, g_scratch = _gather_inputs(bsz, seq, d)
    grid_spec = pltpu.PrefetchScalarGridSpec(
        num_scalar_prefetch=2,
        grid=(bsz, seq // SEQ_TILE),
        in_specs=g_specs + [pl.BlockSpec((1, d), lambda b, s, *_: (0, 0))],
        out_specs=pl.BlockSpec((1, SEQ_TILE, d), lambda b, s, *_: (b, s, 0)),
        scratch_shapes=g_scratch,
    )
    return pl.pallas_call(
        _final_kernel, grid_spec=grid_spec,
        out_shape=jax.ShapeDtypeStruct((bsz, seq, d), F32),
        compiler_params=_PARAMS, name="final_norm",
    )(segbp, segnp, x1p, ysp, qp, mod_prev, final_g.reshape(1, d))


def _moe_kernel(blk_ref, grp_ref, valid_ref, fresh_ref, hs_ref, wg_ref, wu_ref, wd_ref, ys_ref,
                wg_bf, wu_bf, wd_bf):
    i = pl.program_id(0)

    @pl.when(fresh_ref[i] != 0)
    def _():
        wg_bf[...] = wg_ref[0].astype(BF16)
        wu_bf[...] = wu_ref[0].astype(BF16)
        wd_bf[...] = wd_ref[0].astype(BF16)

    @pl.when(valid_ref[i] != 0)
    def _():
        rows = hs_ref[...]
        h = rows[:, :D_MODEL]
        m = rows[:, D_MODEL:].astype(F32)
        ne = EXPERTS_PER_GROUP
        y = None
        for j in range(ne):
            gate = m[:, j:j + 1] + (m[:, ne + j:ne + j + 1] + m[:, 2 * ne + j:2 * ne + j + 1])
            a = jnp.dot(h, wg_bf[j], preferred_element_type=F32)
            a = a * jax.nn.sigmoid(a) * jnp.dot(h, wu_bf[j], preferred_element_type=F32)
            yj = jnp.dot((a * gate).astype(BF16), wd_bf[j], preferred_element_type=F32)
            y = yj if y is None else y + yj
        ys_ref[...] = y.astype(BF16)


def _moe_items(fill, region_rows, n_items):
    tiles = (fill + (MOE_TILE - 1)) // MOE_TILE
    ends = jnp.cumsum(tiles)
    i = jnp.minimum(jnp.arange(n_items, dtype=I32), jnp.maximum(ends[-1] - 1, 0))
    grp = jnp.sum((i[:, None] >= ends[None, :]).astype(I32), axis=1)
    blk = grp * (region_rows // MOE_TILE) + i - (ends - tiles)[grp]
    valid = (jnp.arange(n_items, dtype=I32) < ends[-1]).astype(I32)
    fresh = jnp.concatenate([jnp.ones((1,), I32), (grp[1:] != grp[:-1]).astype(I32)])
    return blk.astype(I32), grp.astype(I32), valid, fresh


def _moe_call(hs, fill, layer, wg, wu, wd, t):
    region_rows = hs.shape[0] // N_GROUPS
    d = D_MODEL
    ne = EXPERTS_PER_GROUP
    n_items = -(-(t + (t // SEQ_TILE) * N_GROUPS * (ROW_ALIGN - 1)) // MOE_TILE) + N_GROUPS
    items = _moe_items(fill, region_rows, n_items)
    w_idx = lambda i, blk, grp, valid, fresh: (layer * N_GROUPS + grp[i], 0, 0, 0)
    row_idx = lambda i, blk, grp, valid, fresh: (blk[i], 0)
    grid_spec = pltpu.PrefetchScalarGridSpec(
        num_scalar_prefetch=len(items),
        grid=(n_items,),
        in_specs=[pl.BlockSpec((MOE_TILE, ROW_W), row_idx),
                  pl.BlockSpec((1, ne, d, D_EXPERT), w_idx),
                  pl.BlockSpec((1, ne, d, D_EXPERT), w_idx),
                  pl.BlockSpec((1, ne, D_EXPERT, d), w_idx)],
        out_specs=pl.BlockSpec((MOE_TILE, d), row_idx),
        scratch_shapes=[pltpu.VMEM((ne, d, D_EXPERT), BF16), pltpu.VMEM((ne, d, D_EXPERT), BF16),
                        pltpu.VMEM((ne, D_EXPERT, d), BF16)],
    )
    return pl.pallas_call(
        _moe_kernel, grid_spec=grid_spec,
        out_shape=jax.ShapeDtypeStruct((hs.shape[0], d), BF16),
        compiler_params=pltpu.CompilerParams(dimension_semantics=("arbitrary",),
                                             vmem_limit_bytes=VMEM_LIMIT),
        name="moe",
    )(*items, hs, wg, wu, wd)


def kernel(x, c, ada_w, ada_b, norm_mix_g, norm_ffn_g, hyb_w_in, sgu_ln_g, sgu_ln_b, sgu_w, sgu_b,
           sconv_w, hyb_w_out, conf_w1, conf_b1, conf_dw_w, conf_dw_b, conf_ln_g, conf_ln_b,
           conf_w2, conf_b2, router_w, router_bias, moe_w_gate, moe_w_up, moe_w_down, final_norm_g):
    depth = ada_w.shape[0]
    bsz, seq, d = x.shape
    t = bsz * seq
    assert depth == 2 and seq % SEQ_TILE == 0
    mod = _ada_mod(c, ada_w, ada_b).reshape(depth, bsz, N_MOD, d)

    perm = jnp.arange(N_EXPERTS).reshape(N_GROUPS, EXPERTS_PER_GROUP).T.reshape(-1)
    rw_t = router_w.T[perm]
    rb_t = router_bias[perm].reshape(N_EXPERTS, 1)

    ne = EXPERTS_PER_GROUP
    wg = moe_w_gate.reshape(depth * N_GROUPS, ne, d, D_EXPERT)
    wu = moe_w_up.reshape(depth * N_GROUPS, ne, d, D_EXPERT)
    wd = moe_w_down.reshape(depth * N_GROUPS, ne, D_EXPERT, d)

    row = lambda a: a.reshape(1, -1)
    consts0 = [row(norm_mix_g[0]), row(norm_ffn_g[0]), row(sgu_ln_g[0]), row(sgu_ln_b[0]), sgu_w[0],
               sgu_b[0].reshape(A_HEADS, CHUNK, 1), sconv_w[0], rw_t, rb_t]
    x1, hs, q, segb, segn, fill = _hybrid_call(x, mod[0], hyb_w_in, hyb_w_out, consts0)
    ys = _moe_call(hs, fill, 0, wg, wu, wd, t)

    consts1 = [row(norm_mix_g[1]), row(norm_ffn_g[1]), row(conf_b1[0]), conf_dw_w[0],
               row(conf_dw_b[0]), row(conf_ln_g[0]), row(conf_ln_b[0]), row(conf_b2[0]), rw_t, rb_t]
    x1, hs, q1, segb1, segn1, fill1 = _conformer_call((x1, ys, q, segb, segn), mod[0], mod[1],
                                                      conf_w1, conf_w2, consts1)
    ys = _moe_call(hs, fill1, 1, wg, wu, wd, t)
    return _final_call((x1, ys, q1, segb1, segn1), mod[1], final_norm_g)
```

```python
import functools

import jax
import jax.numpy as jnp
from jax import lax
from jax.experimental import pallas as pl
from jax.experimental.pallas import tpu as pltpu

D_MODEL = 1024
CHUNK = 128
A_HEADS = 4
A_HEAD_DIM = 128
D_A = A_HEADS * A_HEAD_DIM
D_B = 512
B_CONV = 3
D_IN_HYB = 2 * D_A + 3 * D_B
D_C = D_MODEL
C_CONV = 31
N_EXPERTS = 16
N_GROUPS = 4
EXPERTS_PER_GROUP = N_EXPERTS // N_GROUPS
D_EXPERT = 256
N_MOD = 6
EPS = 1e-6

SUBLANES = 8
LANES = 128
ROW_ALIGN = 16

SEQ_TILE = 512
MOE_TILE = 512
MOD_TILE = 1024
W_STAGE_ROWS = 128
C_HALO = 40
B_HALO = 8
CONV_ROWS = 128
GLU_COLS = 256
S_LEAD = ROW_ALIGN
S_ROWS = CONV_ROWS + S_LEAD
S_SLOTS = 8
SORT_ROWS = SEQ_TILE + N_GROUPS * ROW_ALIGN
ROW_W = D_MODEL + LANES
VMEM_LIMIT = 56 * 1024 * 1024

F32 = jnp.float32
BF16 = jnp.bfloat16
I32 = jnp.int32


def _round_up(n, m):
    return -(-n // m) * m


def _region_rows(t):
    return _round_up(t + (t // SEQ_TILE) * ROW_ALIGN, MOE_TILE)


def _split_bf16(a):
    hi = a.astype(BF16)
    lo = (a - hi.astype(F32)).astype(BF16)
    return hi, lo


def _dot3(a, b):
    ah, al = _split_bf16(a)
    bh, bl = _split_bf16(b)
    d = functools.partial(jnp.dot, preferred_element_type=F32)
    return d(ah, bh) + (d(ah, bl) + d(al, bh))


def _rms(x, g):
    return x * lax.rsqrt(jnp.mean(x * x, axis=-1, keepdims=True) + EPS) * g


def _rms_mod(x, g, shift, scale):
    return x * lax.rsqrt(jnp.mean(x * x, axis=-1, keepdims=True) + EPS) * (g * (1.0 + scale)) + shift


def _gelu(x):
    return 0.5 * x * (1.0 + lax.erf(x * (2.0 ** -0.5)))


def _layer_norm(x, g, b):
    mu = jnp.mean(x, axis=-1, keepdims=True)
    xc = x - mu
    var = jnp.mean(xc * xc, axis=-1, keepdims=True)
    return xc * lax.rsqrt(var + EPS) * g + b


def _mod_kernel(c_ref, w_ref, b_ref, o_ref):
    c = c_ref[...]
    c_act = c * jax.nn.sigmoid(c)
    o_ref[0] = _dot3(c_act, w_ref[0]) + b_ref[0]


def _ada_mod(c, ada_w, ada_b):
    depth, d, n = ada_w.shape
    bsz = c.shape[0]
    return pl.pallas_call(
        _mod_kernel,
        grid=(depth, n // MOD_TILE),
        in_specs=[
            pl.BlockSpec((bsz, d), lambda l, j: (0, 0)),
            pl.BlockSpec((1, d, MOD_TILE), lambda l, j: (l, 0, j)),
            pl.BlockSpec((1, 1, MOD_TILE), lambda l, j: (l, 0, j)),
        ],
        out_specs=pl.BlockSpec((1, bsz, MOD_TILE), lambda l, j: (l, 0, j)),
        out_shape=jax.ShapeDtypeStruct((depth, bsz, n), F32),
        compiler_params=pltpu.CompilerParams(
            dimension_semantics=("arbitrary", "arbitrary"), vmem_limit_bytes=VMEM_LIMIT),
        name="ada_mod",
    )(c, ada_w, ada_b.reshape(depth, 1, n))


def _route(h2, rw_ref, rb_ref):
    wh, wl = _split_bf16(rw_ref[...])
    hh, hl = _split_bf16(h2)
    nt = functools.partial(lax.dot_general, dimension_numbers=(((1,), (1,)), ((), ())),
                           preferred_element_type=F32)
    logits = nt(wh, hh) + (nt(wh, hl) + nt(wl, hh))
    scores = jax.nn.sigmoid(logits)
    biased = scores + rb_ref[...]
    ng, ne = N_GROUPS, EXPERTS_PER_GROUP
    b = [biased[j * ng:(j + 1) * ng] for j in range(ne)]
    s = [scores[j * ng:(j + 1) * ng] for j in range(ne)]
    gscore = None
    for i in range(ne):
        for j in range(i + 1, ne):
            pair = b[i] + b[j]
            gscore = pair if gscore is None else jnp.maximum(gscore, pair)
    gid = lax.broadcasted_iota(I32, gscore.shape, 0)
    gmax = jnp.max(gscore, axis=0, keepdims=True)
    best = jnp.min(jnp.where(gscore == gmax, gid, ng), axis=0, keepdims=True)
    onb = gid == best
    cb = [jnp.sum(jnp.where(onb, b[j], 0.0), axis=0, keepdims=True) for j in range(ne)]
    cs = [jnp.sum(jnp.where(onb, s[j], 0.0), axis=0, keepdims=True) for j in range(ne)]
    sel = []
    for j in range(ne):
        rank = jnp.zeros_like(best)
        for i in range(ne):
            if i == j:
                continue
            ahead = (cb[i] >= cb[j]) if i < j else (cb[i] > cb[j])
            rank = rank + ahead.astype(I32)
        sel.append(rank < 2)
    wsum = sum(jnp.where(sel[j], cs[j], 0.0) for j in range(ne))
    gates = jnp.concatenate([jnp.where(sel[j], cs[j], 0.0) / wsum for j in range(ne)], axis=0)
    return gates, jnp.where(onb, 1.0, 0.0)


def _sort_matrix(q):
    rows = lax.broadcasted_iota(I32, (SORT_ROWS, q.shape[1]), 0)
    return jnp.where(rows == q, 1.0, 0.0).astype(BF16)


def _run_copies(op, src_of, dst_of, sem, nrows, bases, region_rows):
    if op == "wait":
        total = pl.multiple_of(sum(nrows[1:], nrows[0]), ROW_ALIGN)

        @pl.when(total > 0)
        def _():
            pltpu.make_async_copy(src_of(0, 0, total), dst_of(0, 0, total), sem).wait()
        return
    tile_off = 0
    for g in range(N_GROUPS):
        n = pl.multiple_of(nrows[g], ROW_ALIGN)
        a = pl.multiple_of(tile_off, ROW_ALIGN)
        b = pl.multiple_of(g * region_rows + bases[g], ROW_ALIGN)

        @pl.when(n > 0)
        def _(a=a, b=b, n=n):
            pltpu.make_async_copy(src_of(a, b, n), dst_of(a, b, n), sem).start()

        tile_off = tile_off + nrows[g]


class _ScatterRefs:
    def __init__(self, gffn_ref, rw_ref, rb_ref, hs_ref, q_ref, segb_ref, segn_ref, fill_ref,
                 h2buf, rowbuf, tri, zbuf, fill_s, pend_s, ssem, zsem):
        self.__dict__.update(locals())
        self.region_rows = hs_ref.shape[0] // N_GROUPS
        self.dst_of = lambda a, b, size: hs_ref.at[pl.ds(b, size)]


def _sc_init(sc, tile):
    @pl.when(tile == 0)
    def _():
        for g in range(N_GROUPS):
            sc.fill_s[g] = 0
        for i in range(sc.pend_s.shape[0]):
            sc.pend_s[i] = 0
        ts = sc.tri.shape[0]
        k = lax.broadcasted_iota(I32, (ts, ts), 0)
        t = lax.broadcasted_iota(I32, (ts, ts), 1)
        sc.tri[...] = jnp.where(k <= t, 1.0, 0.0).astype(BF16)
        sc.h2buf[...] = jnp.zeros_like(sc.h2buf)


def _sc_pending(sc, op, slot):
    rec = slot * (2 * N_GROUPS)
    src_of = lambda a, b, size: sc.rowbuf.at[slot, pl.ds(a, size)]
    _run_copies(op, src_of, sc.dst_of, sc.ssem.at[slot], [sc.pend_s[rec + g] for g in range(N_GROUPS)],
                [sc.pend_s[rec + N_GROUPS + g] for g in range(N_GROUPS)], sc.region_rows)


def _prenorm(x1, mod, gffn_ref):
    sh2, sc2 = mod[3:4], mod[4:5]
    return _rms_mod(x1, gffn_ref[...], sh2, sc2)


def _sc_route(sc, st):
    st["h2"] = sc.h2buf[...]
    st["gates"], st["onehot"] = _route(st["h2"], sc.rw_ref, sc.rb_ref)


def _sc_rank(sc, st, active):
    onehot = st["onehot"]
    ts = onehot.shape[1]
    oh8 = jnp.concatenate([onehot, jnp.zeros((SUBLANES - N_GROUPS, ts), F32)], axis=0).astype(BF16)
    csum = jnp.dot(oh8, sc.tri[...], preferred_element_type=F32)[:N_GROUPS]
    counts = [csum[g, ts - 1].astype(I32) * active for g in range(N_GROUPS)]
    st["nrows"] = [(c + (ROW_ALIGN - 1)) & ~(ROW_ALIGN - 1) for c in counts]
    q = jnp.zeros((1, ts), F32)
    tile_off = 0
    for g in range(N_GROUPS):
        q = q + onehot[g:g + 1] * (csum[g:g + 1] - 1.0 + tile_off.astype(F32)
                                   if g else csum[g:g + 1] - 1.0)
        tile_off = tile_off + st["nrows"][g]
    st["q"] = q.astype(I32)


def _sc_sort(sc, st, slot):
    gates = st["gates"]
    ts = gates.shape[1]
    hi = gates.astype(BF16).astype(F32)
    mid = (gates - hi).astype(BF16).astype(F32)
    lo = (gates - hi - mid).astype(BF16).astype(F32)
    meta_t = jnp.concatenate(
        [hi, mid, lo, jnp.zeros((LANES - 3 * EXPERTS_PER_GROUP, ts), F32)], axis=0)
    rows = jnp.concatenate([st["h2"].astype(BF16), meta_t.T.astype(BF16)], axis=1)
    sorted_rows = jnp.dot(_sort_matrix(st["q"]), rows, preferred_element_type=F32)
    sc.rowbuf[slot] = sorted_rows.astype(BF16)


def _sc_start(sc, st, t_idx, slot):
    sc.q_ref[t_idx] = st["q"]
    nrows = st["nrows"]
    rec = slot * (2 * N_GROUPS)
    for g in range(N_GROUPS):
        base = sc.fill_s[g]
        sc.pend_s[rec + g] = nrows[g]
        sc.pend_s[rec + N_GROUPS + g] = base
        sc.segn_ref[t_idx * N_GROUPS + g] = nrows[g]
        sc.segb_ref[t_idx * N_GROUPS + g] = base
        sc.fill_s[g] = base + nrows[g]
    _sc_pending(sc, "start", slot)


def _sc_finish(sc):
    _sc_pending(sc, "wait", 0)
    _sc_pending(sc, "wait", 1)
    sc.zbuf[...] = jnp.zeros_like(sc.zbuf)
    for op in ("start", "wait"):
        for g in range(N_GROUPS):
            fill = sc.fill_s[g]
            n = pl.multiple_of((-fill) & (MOE_TILE - 1), ROW_ALIGN)
            b = pl.multiple_of(g * sc.region_rows + fill, ROW_ALIGN)

            @pl.when(n > 0)
            def _(n=n, b=b, op=op):
                cp = pltpu.make_async_copy(sc.zbuf.at[pl.ds(0, n)], sc.hs_ref.at[pl.ds(b, n)], sc.zsem)
                getattr(cp, op)()
    for g in range(N_GROUPS):
        sc.fill_ref[g] = sc.fill_s[g]


def _gather_prev_ffn(tile, n_tiles, segb_ref, segn_ref, ys_ref, q_ref, ybuf, gsem):
    region_rows = ys_ref.shape[0] // N_GROUPS

    def copies(op, t, slot):
        nrows = [segn_ref[t * N_GROUPS + g] for g in range(N_GROUPS)]
        bases = [segb_ref[t * N_GROUPS + g] for g in range(N_GROUPS)]
        src_of = lambda a, b, size: ys_ref.at[pl.ds(b, size)]
        dst_of = lambda a, b, size: ybuf.at[slot, pl.ds(a, size)]
        _run_copies(op, src_of, dst_of, gsem.at[slot], nrows, bases, region_rows)

    def issue(t, slot):
        ybuf[slot, SEQ_TILE:SORT_ROWS, :] = jnp.zeros((SORT_ROWS - SEQ_TILE, D_MODEL), BF16)
        copies("start", t, slot)

    @pl.when(tile == 0)
    def _():
        issue(tile, 0)

    @pl.when(tile + 1 < n_tiles)
    def _():
        issue(tile + 1, (tile + 1) % 2)

    slot = tile % 2
    copies("wait", tile, slot)
    p = _sort_matrix(q_ref[0])
    return lax.dot_general(p, ybuf[slot], (((0,), (0,)), ((), ())), preferred_element_type=F32)


def _load_weight_bf16(w_hbm, w_vmem, stage, sem):
    rows = stage.shape[1]
    n = w_vmem.shape[0] // rows
    copy = lambda c: pltpu.make_async_copy(w_hbm.at[pl.ds(c * rows, rows)], stage.at[c % 2],
                                           sem.at[c % 2])
    copy(0).start()
    for c in range(n):
        if c + 1 < n:
            copy(c + 1).start()
        copy(c).wait()
        w_vmem[c * rows:(c + 1) * rows, :] = stage[c % 2].astype(BF16)


def _weight_scratch(k, n):
    return [pltpu.VMEM((k, n), BF16), pltpu.VMEM((2, W_STAGE_ROWS, n), F32),
            pltpu.SemaphoreType.DMA((2,))]


def _hybrid_kernel(x_ref, mod_ref, win_hbm, wout_hbm, gmix_ref, gffn_ref, lng_ref, lnb_ref, sw_ref,
                   sb_ref, cw_ref, rw_ref, rb_ref,
                   x1_ref, hs_ref, q_ref, segb_ref, segn_ref, fill_ref,
                   pbuf, win_ref, win_stage, win_sem, wout_ref, wout_stage, wout_sem,
                   h2buf, rowbuf, tri, zbuf, fill_s, pend_s, ssem, zsem):
    ts = x_ref.shape[1]
    tile = pl.program_id(0) * pl.num_programs(1) + pl.program_id(1)
    n_tiles = pl.num_programs(0) * pl.num_programs(1)
    sc = _ScatterRefs(gffn_ref, rw_ref, rb_ref, hs_ref, q_ref, segb_ref, segn_ref, fill_ref,
                      h2buf, rowbuf, tri, zbuf, fill_s, pend_s, ssem, zsem)

    @pl.when(tile == 0)
    def _():
        _load_weight_bf16(win_hbm.at[0], win_ref, win_stage, win_sem)
        _load_weight_bf16(wout_hbm.at[0], wout_ref, wout_stage, wout_sem)

    @pl.when(pl.program_id(1) == 0)
    def _():
        pbuf[0:B_HALO, :] = jnp.zeros((B_HALO, D_B), F32)

    _sc_init(sc, tile)
    slot = tile % 2
    _sc_pending(sc, "wait", slot)

    prev = {}
    active = (tile > 0).astype(I32)

    x = x_ref[0]
    mod = mod_ref[0]
    sh1, sc1, g1 = mod[0:1], mod[1:2], mod[2:3]
    h = _rms_mod(x, gmix_ref[...], sh1, sc1)
    hb16 = h.astype(BF16)
    zcol = lambda j: jnp.dot(hb16, win_ref[:, j * D_A:(j + 1) * D_A], preferred_element_type=F32)

    v = _layer_norm(_gelu(zcol(1)), lng_ref[...], lnb_ref[...]).astype(BF16)
    _sc_route(sc, prev)
    u = _gelu(zcol(0))
    tril = (lax.broadcasted_iota(I32, (CHUNK, CHUNK), 0)
            >= lax.broadcasted_iota(I32, (CHUNK, CHUNK), 1))
    chunks = []
    for n in range(ts // CHUNK):
        heads = []
        for hd in range(A_HEADS):
            w = jnp.where(tril, sw_ref[hd], 0.0).astype(BF16)
            vb = v[n * CHUNK:(n + 1) * CHUNK, hd * A_HEAD_DIM:(hd + 1) * A_HEAD_DIM]
            heads.append(jnp.dot(w, vb, preferred_element_type=F32) + sb_ref[hd])
        chunks.append(jnp.concatenate(heads, axis=1))
    y_a = u * jnp.concatenate(chunks, axis=0)
    _sc_rank(sc, prev, active)

    p = zcol(3) * zcol(4)
    pbuf[B_HALO:B_HALO + ts, :] = p
    cw = cw_ref[...]
    conv = (cw[0:1] * pbuf[B_HALO - 2:B_HALO - 2 + ts, :]
            + cw[1:2] * pbuf[B_HALO - 1:B_HALO - 1 + ts, :]
            + cw[2:3] * p)
    pbuf[0:B_HALO, :] = p[ts - B_HALO:ts, :]
    y_b = zcol(2) * conv
    _sc_sort(sc, prev, slot)

    ycat = jnp.concatenate([y_a, y_b], axis=1).astype(BF16)
    y = jnp.dot(ycat, wout_ref[...], preferred_element_type=F32)
    x1 = x + g1 * y
    x1_ref[0] = x1
    h2buf[...] = _prenorm(x1, mod, gffn_ref)
    _sc_start(sc, prev, jnp.maximum(tile - 1, 0), slot)

    @pl.when(tile == n_tiles - 1)
    def _():
        _sc_pending(sc, "wait", 1 - slot)
        last = {}
        _sc_route(sc, last)
        _sc_rank(sc, last, 1)
        _sc_sort(sc, last, 1 - slot)
        _sc_start(sc, last, tile, 1 - slot)
        _sc_finish(sc)


def _band_matrix():
    shape = (CONV_ROWS, SUBLANES * S_ROWS)
    t = lax.broadcasted_iota(I32, shape, 0)
    c = lax.broadcasted_iota(I32, shape, 1)
    hit = c == t + S_LEAD
    for r in range(1, SUBLANES):
        hit = hit | (c == t + S_LEAD + r * (S_ROWS - 1))
    return jnp.where(hit, 1.0, 0.0).astype(BF16)


def _causal_dw_conv(cbuf, w_ref, obuf, band, sbuf, row0, nrows, col0, ncols, first_slot):
    n_a = -(-C_CONV // SUBLANES)
    cols = slice(col0, col0 + ncols)
    for i, t0 in enumerate(range(row0, row0 + nrows, CONV_ROWS)):
        stack = sbuf.at[(first_slot + i) % sbuf.shape[0]]
        for r in range(SUBLANES):
            part = None
            for a in range(n_a):
                k = C_CONV - 1 - SUBLANES * a - r
                if k < 0:
                    continue
                r0 = t0 + C_HALO - S_LEAD - SUBLANES * a
                term = w_ref[k:k + 1, cols] * cbuf[r0:r0 + S_ROWS, cols]
                part = term if part is None else part + term
            stack[r * S_ROWS:(r + 1) * S_ROWS, :] = part.astype(BF16)
        obuf[t0:t0 + CONV_ROWS, cols] = jnp.dot(band[...], stack[...], preferred_element_type=F32)


def _conformer_kernel(psegb_ref, psegn_ref,
                      xp_ref, ys_ref, pq_ref, pmod_ref, mod_ref, w1_hbm, w2_hbm, gmix_ref, gffn_ref,
                      b1_ref, dww_ref, dwb_ref, lng_ref, lnb_ref, b2_ref, rw_ref, rb_ref,
                      x1_ref, hs_ref, q_ref, segb_ref, segn_ref, fill_ref,
                      cbuf, obuf, band, sbuf, ybuf, gsem,
                      w1_ref, w1_stage, w1_sem, w2_ref, w2_stage, w2_sem,
                      h2buf, rowbuf, tri, zbuf, fill_s, pend_s, ssem, zsem):
    ts = xp_ref.shape[1]
    tile = pl.program_id(0) * pl.num_programs(1) + pl.program_id(1)
    n_tiles = pl.num_programs(0) * pl.num_programs(1)
    sc = _ScatterRefs(gffn_ref, rw_ref, rb_ref, hs_ref, q_ref, segb_ref, segn_ref, fill_ref,
                      h2buf, rowbuf, tri, zbuf, fill_s, pend_s, ssem, zsem)

    @pl.when(tile == 0)
    def _():
        _load_weight_bf16(w1_hbm.at[0], w1_ref, w1_stage, w1_sem)
        _load_weight_bf16(w2_hbm.at[0], w2_ref, w2_stage, w2_sem)
        band[...] = _band_matrix()

    @pl.when(pl.program_id(1) == 0)
    def _():
        cbuf[0:C_HALO, :] = jnp.zeros((C_HALO, D_C), F32)

    _sc_init(sc, tile)
    slot = tile % 2
    _sc_pending(sc, "wait", slot)
    y_prev = _gather_prev_ffn(tile, n_tiles, psegb_ref, psegn_ref, ys_ref, pq_ref, ybuf, gsem)

    prev = {}
    active = (tile > 0).astype(I32)
    deferred = [lambda: _sc_route(sc, prev), lambda: _sc_rank(sc, prev, active),
                lambda: _sc_sort(sc, prev, slot)]

    x = xp_ref[0] + pmod_ref[0][5:6] * y_prev
    mod = mod_ref[0]
    sh1, sc1, g1 = mod[0:1], mod[1:2], mod[2:3]
    h = _rms_mod(x, gmix_ref[...], sh1, sc1)
    hb = h.astype(BF16)
    def glu(c0):
        ca, cg = slice(c0, c0 + GLU_COLS), slice(D_C + c0, D_C + c0 + GLU_COLS)
        za = jnp.dot(hb, w1_ref[:, ca], preferred_element_type=F32) + b1_ref[:, ca]
        zg = jnp.dot(hb, w1_ref[:, cg], preferred_element_type=F32) + b1_ref[:, cg]
        cbuf[C_HALO:C_HALO + ts, ca] = za * jax.nn.sigmoid(zg)

    glu(0)
    for k, c0 in enumerate(range(0, D_C, GLU_COLS)):
        if c0 + GLU_COLS < D_C:
            glu(c0 + GLU_COLS)
        if k < len(deferred):
            deferred[k]()
        _causal_dw_conv(cbuf, dww_ref, obuf, band, sbuf, 0, ts, c0, GLU_COLS,
                        k * (ts // CONV_ROWS))
    cbuf[0:C_HALO, :] = cbuf[ts:ts + C_HALO, :]

    zc = _layer_norm(obuf[...] + dwb_ref[...], lng_ref[...], lnb_ref[...])
    zc = zc * jax.nn.sigmoid(zc)
    y = jnp.dot(zc.astype(BF16), w2_ref[...], preferred_element_type=F32) + b2_ref[...]
    x1 = x + g1 * y
    x1_ref[0] = x1
    h2buf[...] = _prenorm(x1, mod, gffn_ref)
    _sc_start(sc, prev, jnp.maximum(tile - 1, 0), slot)

    @pl.when(tile == n_tiles - 1)
    def _():
        _sc_pending(sc, "wait", 1 - slot)
        last = {}
        _sc_route(sc, last)
        _sc_rank(sc, last, 1)
        _sc_sort(sc, last, 1 - slot)
        _sc_start(sc, last, tile, 1 - slot)
        _sc_finish(sc)


def _final_kernel(psegb_ref, psegn_ref, xp_ref, ys_ref, pq_ref, pmod_ref, fg_ref, o_ref, ybuf, gsem):
    tile = pl.program_id(0) * pl.num_programs(1) + pl.program_id(1)
    n_tiles = pl.num_programs(0) * pl.num_programs(1)
    y_prev = _gather_prev_ffn(tile, n_tiles, psegb_ref, psegn_ref, ys_ref, pq_ref, ybuf, gsem)
    x = xp_ref[0] + pmod_ref[0][5:6] * y_prev
    o_ref[0] = _rms(x, fg_ref[...])


def _scatter_outputs(bsz, seq, d):
    t = bsz * seq
    n_tiles = t // SEQ_TILE
    smem = pl.BlockSpec(memory_space=pltpu.SMEM)
    specs = [pl.BlockSpec((1, SEQ_TILE, d), lambda b, s, *_: (b, s, 0)),
             pl.BlockSpec(memory_space=pl.ANY),
             pl.BlockSpec((n_tiles, 1, SEQ_TILE), lambda b, s, *_: (0, 0, 0)),
             smem, smem, smem]
    shapes = [jax.ShapeDtypeStruct((bsz, seq, d), F32),
              jax.ShapeDtypeStruct((N_GROUPS * _region_rows(t), ROW_W), BF16),
              jax.ShapeDtypeStruct((n_tiles, 1, SEQ_TILE), I32),
              jax.ShapeDtypeStruct((n_tiles * N_GROUPS,), I32),
              jax.ShapeDtypeStruct((n_tiles * N_GROUPS,), I32),
              jax.ShapeDtypeStruct((N_GROUPS,), I32)]
    scratch = [pltpu.VMEM((SEQ_TILE, d), F32),
               pltpu.VMEM((2, SORT_ROWS, ROW_W), BF16),
               pltpu.VMEM((SEQ_TILE, SEQ_TILE), BF16),
               pltpu.VMEM((MOE_TILE, ROW_W), BF16),
               pltpu.SMEM((N_GROUPS,), I32),
               pltpu.SMEM((4 * N_GROUPS,), I32),
               pltpu.SemaphoreType.DMA((2,)),
               pltpu.SemaphoreType.DMA(())]
    return specs, shapes, scratch


def _gather_inputs(bsz, seq, d):
    nseq = seq // SEQ_TILE
    specs = [pl.BlockSpec((1, SEQ_TILE, d), lambda b, s, *_: (b, s, 0)),
             pl.BlockSpec(memory_space=pl.ANY),
             pl.BlockSpec((1, 1, SEQ_TILE), lambda b, s, *_: (b * nseq + s, 0, 0)),
             pl.BlockSpec((1, N_MOD, d), lambda b, s, *_: (b, 0, 0))]
    scratch = [pltpu.VMEM((2, SORT_ROWS, d), BF16), pltpu.SemaphoreType.DMA((2,))]
    return specs, scratch


def _const_specs(consts):
    return [pl.BlockSpec(a.shape, lambda b, s, *_, nd=a.ndim: (0,) * nd) for a in consts]


_PARAMS = pltpu.CompilerParams(dimension_semantics=("arbitrary", "arbitrary"),
                               vmem_limit_bytes=VMEM_LIMIT)


def _hybrid_call(x, mod_l, w_in, w_out, consts):
    bsz, seq, d = x.shape
    out_specs, out_shapes, sc_scratch = _scatter_outputs(bsz, seq, d)
    hbm = pl.BlockSpec(memory_space=pl.ANY)
    return pl.pallas_call(
        _hybrid_kernel,
        grid=(bsz, seq // SEQ_TILE),
        in_specs=[pl.BlockSpec((1, SEQ_TILE, d), lambda b, s: (b, s, 0)),
                  pl.BlockSpec((1, N_MOD, d), lambda b, s: (b, 0, 0)), hbm, hbm]
                 + _const_specs(consts),
        out_specs=out_specs,
        out_shape=out_shapes,
        scratch_shapes=[pltpu.VMEM((SEQ_TILE + B_HALO, D_B), F32)]
                       + _weight_scratch(*w_in.shape[1:]) + _weight_scratch(*w_out.shape[1:])
                       + sc_scratch,
        compiler_params=_PARAMS,
        name="mixer_hybrid",
    )(x, mod_l, w_in, w_out, *consts)


def _conformer_call(prev, mod_prev, mod_l, w1, w2, consts):
    x1p, ysp, qp, segbp, segnp = prev
    bsz, seq, d = x1p.shape
    g_specs, g_scratch = _gather_inputs(bsz, seq, d)
    out_specs, out_shapes, sc_scratch = _scatter_outputs(bsz, seq, d)
    hbm = pl.BlockSpec(memory_space=pl.ANY)
    grid_spec = pltpu.PrefetchScalarGridSpec(
        num_scalar_prefetch=2,
        grid=(bsz, seq // SEQ_TILE),
        in_specs=g_specs + [pl.BlockSpec((1, N_MOD, d), lambda b, s, *_: (b, 0, 0)), hbm, hbm]
                 + _const_specs(consts),
        out_specs=out_specs,
        scratch_shapes=[pltpu.VMEM((SEQ_TILE + C_HALO, D_C), F32), pltpu.VMEM((SEQ_TILE, D_C), F32),
                        pltpu.VMEM((CONV_ROWS, SUBLANES * S_ROWS), BF16),
                        pltpu.VMEM((S_SLOTS, SUBLANES * S_ROWS, GLU_COLS), BF16)]
                       + g_scratch + _weight_scratch(*w1.shape[1:]) + _weight_scratch(*w2.shape[1:])
                       + sc_scratch,
    )
    return pl.pallas_call(
        _conformer_kernel, grid_spec=grid_spec, out_shape=out_shapes,
        compiler_params=_PARAMS, name="mixer_conformer",
    )(segbp, segnp, x1p, ysp, qp, mod_prev, mod_l, w1, w2, *consts)


def _final_call(prev, mod_prev, final_g):
    x1p, ysp, qp, segbp, segnp = prev
    bsz, seq, d = x1p.shape
    g_specs, g_scratch = _gather_inputs(bsz, seq, d)
    grid_spec = pltpu.PrefetchScalarGridSpec(
        num_scalar_prefetch=2,
        grid=(bsz, seq // SEQ_TILE),
        in_specs=g_specs + [pl.BlockSpec((1, d), lambda b, s, *_: (0, 0))],
        out_specs=pl.BlockSpec((1, SEQ_TILE, d), lambda b, s, *_: (b, s, 0)),
        scratch_shapes=g_scratch,
    )
    return pl.pallas_call(
        _final_kernel, grid_spec=grid_spec,
        out_shape=jax.ShapeDtypeStruct((bsz, seq, d), F32),
        compiler_params=_PARAMS, name="final_norm",
    )(segbp, segnp, x1p, ysp, qp, mod_prev, final_g.reshape(1, d))


def _moe_kernel(blk_ref, grp_ref, valid_ref, fresh_ref, hs_ref, wg_ref, wu_ref, wd_ref, ys_ref,
                wg_bf, wu_bf, wd_bf):
    i = pl.program_id(0)

    @pl.when(fresh_ref[i] != 0)
    def _():
        wg_bf[...] = wg_ref[0].astype(BF16)
        wu_bf[...] = wu_ref[0].astype(BF16)
        wd_bf[...] = wd_ref[0].astype(BF16)

    @pl.when(valid_ref[i] != 0)
    def _():
        rows = hs_ref[...]
        h = rows[:, :D_MODEL]
        m = rows[:, D_MODEL:].astype(F32)
        ne = EXPERTS_PER_GROUP
        y = None
        for j in range(ne):
            gate = m[:, j:j + 1] + (m[:, ne + j:ne + j + 1] + m[:, 2 * ne + j:2 * ne + j + 1])
            a = jnp.dot(h, wg_bf[j], preferred_element_type=F32)
            a = a * jax.nn.sigmoid(a) * jnp.dot(h, wu_bf[j], preferred_element_type=F32)
            yj = jnp.dot((a * gate).astype(BF16), wd_bf[j], preferred_element_type=F32)
            y = yj if y is None else y + yj
        ys_ref[...] = y.astype(BF16)


def _moe_items(fill, region_rows, n_items):
    tiles = (fill + (MOE_TILE - 1)) // MOE_TILE
    ends = jnp.cumsum(tiles)
    i = jnp.minimum(jnp.arange(n_items, dtype=I32), jnp.maximum(ends[-1] - 1, 0))
    grp = jnp.sum((i[:, None] >= ends[None, :]).astype(I32), axis=1)
    blk = grp * (region_rows // MOE_TILE) + i - (ends - tiles)[grp]
    valid = (jnp.arange(n_items, dtype=I32) < ends[-1]).astype(I32)
    fresh = jnp.concatenate([jnp.ones((1,), I32), (grp[1:] != grp[:-1]).astype(I32)])
    return blk.astype(I32), grp.astype(I32), valid, fresh


def _moe_call(hs, fill, layer, wg, wu, wd, t):
    region_rows = hs.shape[0] // N_GROUPS
    d = D_MODEL
    ne = EXPERTS_PER_GROUP
    n_items = -(-(t + (t // SEQ_TILE) * N_GROUPS * (ROW_ALIGN - 1)) // MOE_TILE) + N_GROUPS
    items = _moe_items(fill, region_rows, n_items)
    w_idx = lambda i, blk, grp, valid, fresh: (layer * N_GROUPS + grp[i], 0, 0, 0)
    row_idx = lambda i, blk, grp, valid, fresh: (blk[i], 0)
    grid_spec = pltpu.PrefetchScalarGridSpec(
        num_scalar_prefetch=len(items),
        grid=(n_items,),
        in_specs=[pl.BlockSpec((MOE_TILE, ROW_W), row_idx),
                  pl.BlockSpec((1, ne, d, D_EXPERT), w_idx),
                  pl.BlockSpec((1, ne, d, D_EXPERT), w_idx),
                  pl.BlockSpec((1, ne, D_EXPERT, d), w_idx)],
        out_specs=pl.BlockSpec((MOE_TILE, d), row_idx),
        scratch_shapes=[pltpu.VMEM((ne, d, D_EXPERT), BF16), pltpu.VMEM((ne, d, D_EXPERT), BF16),
                        pltpu.VMEM((ne, D_EXPERT, d), BF16)],
    )
    return pl.pallas_call(
        _moe_kernel, grid_spec=grid_spec,
        out_shape=jax.ShapeDtypeStruct((hs.shape[0], d), BF16),
        compiler_params=pltpu.CompilerParams(dimension_semantics=("arbitrary",),
                                             vmem_limit_bytes=VMEM_LIMIT),
        name="moe",
    )(*items, hs, wg, wu, wd)


def kernel(x, c, ada_w, ada_b, norm_mix_g, norm_ffn_g, hyb_w_in, sgu_ln_g, sgu_ln_b, sgu_w, sgu_b,
           sconv_w, hyb_w_out, conf_w1, conf_b1, conf_dw_w, conf_dw_b, conf_ln_g, conf_ln_b,
           conf_w2, conf_b2, router_w, router_bias, moe_w_gate, moe_w_up, moe_w_down, final_norm_g):
    depth = ada_w.shape[0]
    bsz, seq, d = x.shape
    t = bsz * seq
    assert depth == 2 and seq % SEQ_TILE == 0
    mod = _ada_mod(c, ada_w, ada_b).reshape(depth, bsz, N_MOD, d)

    perm = jnp.arange(N_EXPERTS).reshape(N_GROUPS, EXPERTS_PER_GROUP).T.reshape(-1)
    rw_t = router_w.T[perm]
    rb_t = router_bias[perm].reshape(N_EXPERTS, 1)

    ne = EXPERTS_PER_GROUP
    wg = moe_w_gate.reshape(depth * N_GROUPS, ne, d, D_EXPERT)
    wu = moe_w_up.reshape(depth * N_GROUPS, ne, d, D_EXPERT)
    wd = moe_w_down.reshape(depth * N_GROUPS, ne, D_EXPERT, d)

    row = lambda a: a.reshape(1, -1)
    consts0 = [row(norm_mix_g[0]), row(norm_ffn_g[0]), row(sgu_ln_g[0]), row(sgu_ln_b[0]), sgu_w[0],
               sgu_b[0].reshape(A_HEADS, CHUNK, 1), sconv_w[0], rw_t, rb_t]
    x1, hs, q, segb, segn, fill = _hybrid_call(x, mod[0], hyb_w_in, hyb_w_out, consts0)
    ys = _moe_call(hs, fill, 0, wg, wu, wd, t)

    consts1 = [row(norm_mix_g[1]), row(norm_ffn_g[1]), row(conf_b1[0]), conf_dw_w[0],
               row(conf_dw_b[0]), row(conf_ln_g[0]), row(conf_ln_b[0]), row(conf_b2[0]), rw_t, rb_t]
    x1, hs, q1, segb1, segn1, fill1 = _conformer_call((x1, ys, q, segb, segn), mod[0], mod[1],
                                                      conf_w1, conf_w2, consts1)
    ys = _moe_call(hs, fill1, 1, wg, wu, wd, t)
    return _final_call((x1, ys, q1, segb1, segn1), mod[1], final_norm_g)
```

```python
import functools

import jax
import jax.numpy as jnp
from jax import lax
from jax.experimental import pallas as pl
from jax.experimental.pallas import tpu as pltpu

D_MODEL = 1024
CHUNK = 128
A_HEADS = 4
A_HEAD_DIM = 128
D_A = A_HEADS * A_HEAD_DIM
D_B = 512
B_CONV = 3
D_IN_HYB = 2 * D_A + 3 * D_B
D_C = D_MODEL
C_CONV = 31
N_EXPERTS = 16
N_GROUPS = 4
EXPERTS_PER_GROUP = N_EXPERTS // N_GROUPS
D_EXPERT = 256
N_MOD = 6
EPS = 1e-6

SUBLANES = 8
LANES = 128
ROW_ALIGN = 16

SEQ_TILE = 512
MOE_TILE = 512
MOD_TILE = 2048
W_STAGE_ROWS = 128
C_HALO = 40
B_HALO = 8
CONV_ROWS = 128
GLU_COLS = 256
S_LEAD = ROW_ALIGN
S_ROWS = CONV_ROWS + S_LEAD
S_SLOTS = 8
SORT_ROWS = SEQ_TILE + N_GROUPS * ROW_ALIGN
ROW_W = D_MODEL + LANES
VMEM_LIMIT = 56 * 1024 * 1024

F32 = jnp.float32
BF16 = jnp.bfloat16
I32 = jnp.int32


def _round_up(n, m):
    return -(-n // m) * m


def _moe_steps(t):
    return -(-(t + (t // SEQ_TILE) * N_GROUPS * (ROW_ALIGN - 1)) // MOE_TILE) + N_GROUPS


def _region_rows(t):
    return _round_up(t + (t // SEQ_TILE) * ROW_ALIGN, MOE_TILE)


def _split_bf16(a):
    hi = a.astype(BF16)
    lo = (a - hi.astype(F32)).astype(BF16)
    return hi, lo


def _dot3(a, b):
    ah, al = _split_bf16(a)
    bh, bl = _split_bf16(b)
    d = functools.partial(jnp.dot, preferred_element_type=F32)
    return d(ah, bh) + (d(ah, bl) + d(al, bh))


def _rms(x, g):
    return x * lax.rsqrt(jnp.mean(x * x, axis=-1, keepdims=True) + EPS) * g


def _rms_mod(x, g, shift, scale):
    return x * lax.rsqrt(jnp.mean(x * x, axis=-1, keepdims=True) + EPS) * (g * (1.0 + scale)) + shift


def _gelu(x):
    return 0.5 * x * (1.0 + lax.erf(x * (2.0 ** -0.5)))


def _layer_norm(x, g, b):
    mu = jnp.mean(x, axis=-1, keepdims=True)
    xc = x - mu
    var = jnp.mean(xc * xc, axis=-1, keepdims=True)
    return xc * lax.rsqrt(var + EPS) * g + b


def _mod_kernel(c_ref, w_ref, b_ref, o_ref):
    c = c_ref[...]
    c_act = c * jax.nn.sigmoid(c)
    o_ref[0] = _dot3(c_act, w_ref[0]) + b_ref[0]


def _ada_mod(c, ada_w, ada_b):
    depth, d, n = ada_w.shape
    bsz = c.shape[0]
    return pl.pallas_call(
        _mod_kernel,
        grid=(depth, n // MOD_TILE),
        in_specs=[
            pl.BlockSpec((bsz, d), lambda l, j: (0, 0)),
            pl.BlockSpec((1, d, MOD_TILE), lambda l, j: (l, 0, j)),
            pl.BlockSpec((1, 1, MOD_TILE), lambda l, j: (l, 0, j)),
        ],
        out_specs=pl.BlockSpec((1, bsz, MOD_TILE), lambda l, j: (l, 0, j)),
        out_shape=jax.ShapeDtypeStruct((depth, bsz, n), F32),
        compiler_params=pltpu.CompilerParams(
            dimension_semantics=("arbitrary", "arbitrary"), vmem_limit_bytes=VMEM_LIMIT),
        name="ada_mod",
    )(c, ada_w, ada_b.reshape(depth, 1, n))


def _route(h2, rw_ref, rb_ref):
    wh, wl = _split_bf16(rw_ref[...])
    hh, hl = _split_bf16(h2)
    nt = functools.partial(lax.dot_general, dimension_numbers=(((1,), (1,)), ((), ())),
                           preferred_element_type=F32)
    logits = nt(wh, hh) + (nt(wh, hl) + nt(wl, hh))
    scores = jax.nn.sigmoid(logits)
    biased = scores + rb_ref[...]
    ng, ne = N_GROUPS, EXPERTS_PER_GROUP
    b = [biased[j * ng:(j + 1) * ng] for j in range(ne)]
    s = [scores[j * ng:(j + 1) * ng] for j in range(ne)]
    gscore = None
    for i in range(ne):
        for j in range(i + 1, ne):
            pair = b[i] + b[j]
            gscore = pair if gscore is None else jnp.maximum(gscore, pair)
    gid = lax.broadcasted_iota(I32, gscore.shape, 0)
    gmax = jnp.max(gscore, axis=0, keepdims=True)
    best = jnp.min(jnp.where(gscore == gmax, gid, ng), axis=0, keepdims=True)
    onb = gid == best
    cb = [jnp.sum(jnp.where(onb, b[j], 0.0), axis=0, keepdims=True) for j in range(ne)]
    cs = [jnp.sum(jnp.where(onb, s[j], 0.0), axis=0, keepdims=True) for j in range(ne)]
    sel = []
    for j in range(ne):
        rank = jnp.zeros_like(best)
        for i in range(ne):
            if i == j:
                continue
            ahead = (cb[i] >= cb[j]) if i < j else (cb[i] > cb[j])
            rank = rank + ahead.astype(I32)
        sel.append(rank < 2)
    wsum = sum(jnp.where(sel[j], cs[j], 0.0) for j in range(ne))
    gates = jnp.concatenate([jnp.where(sel[j], cs[j], 0.0) / wsum for j in range(ne)], axis=0)
    return gates, jnp.where(onb, 1.0, 0.0)


def _sort_matrix(q):
    rows = lax.broadcasted_iota(I32, (SORT_ROWS, q.shape[1]), 0)
    return jnp.where(rows == q, 1.0, 0.0).astype(BF16)


def _run_copies(op, src_of, dst_of, sem, nrows, bases, region_rows):
    if op == "wait":
        total = pl.multiple_of(sum(nrows[1:], nrows[0]), ROW_ALIGN)

        @pl.when(total > 0)
        def _():
            pltpu.make_async_copy(src_of(0, 0, total), dst_of(0, 0, total), sem).wait()
        return
    tile_off = 0
    for g in range(N_GROUPS):
        n = pl.multiple_of(nrows[g], ROW_ALIGN)
        a = pl.multiple_of(tile_off, ROW_ALIGN)
        b = pl.multiple_of(g * region_rows + bases[g], ROW_ALIGN)

        @pl.when(n > 0)
        def _(a=a, b=b, n=n):
            pltpu.make_async_copy(src_of(a, b, n), dst_of(a, b, n), sem).start()

        tile_off = tile_off + nrows[g]


class _ScatterRefs:
    def __init__(self, gffn_ref, rw_ref, rb_ref, hs_ref, q_ref, segb_ref, segn_ref, items_ref,
                 h2buf, rowbuf, tri, zbuf, fill_s, pend_s, ssem, zsem):
        self.__dict__.update(locals())
        self.region_rows = hs_ref.shape[0] // N_GROUPS
        self.dst_of = lambda a, b, size: hs_ref.at[pl.ds(b, size)]


def _sc_init(sc, tile):
    @pl.when(tile == 0)
    def _():
        for g in range(N_GROUPS):
            sc.fill_s[g] = 0
        for i in range(sc.pend_s.shape[0]):
            sc.pend_s[i] = 0
        ts = sc.tri.shape[0]
        k = lax.broadcasted_iota(I32, (ts, ts), 0)
        t = lax.broadcasted_iota(I32, (ts, ts), 1)
        sc.tri[...] = jnp.where(k <= t, 1.0, 0.0).astype(BF16)
        sc.h2buf[...] = jnp.zeros_like(sc.h2buf)


def _sc_pending(sc, op, slot):
    rec = slot * (2 * N_GROUPS)
    src_of = lambda a, b, size: sc.rowbuf.at[slot, pl.ds(a, size)]
    _run_copies(op, src_of, sc.dst_of, sc.ssem.at[slot], [sc.pend_s[rec + g] for g in range(N_GROUPS)],
                [sc.pend_s[rec + N_GROUPS + g] for g in range(N_GROUPS)], sc.region_rows)


def _prenorm(x1, mod, gffn_ref):
    sh2, sc2 = mod[3:4], mod[4:5]
    return _rms_mod(x1, gffn_ref[...], sh2, sc2)


def _sc_route(sc, st):
    st["h2"] = sc.h2buf[...]
    st["gates"], st["onehot"] = _route(st["h2"], sc.rw_ref, sc.rb_ref)


def _sc_rank(sc, st, active):
    onehot = st["onehot"]
    ts = onehot.shape[1]
    oh8 = jnp.concatenate([onehot, jnp.zeros((SUBLANES - N_GROUPS, ts), F32)], axis=0).astype(BF16)
    csum = jnp.dot(oh8, sc.tri[...], preferred_element_type=F32)[:N_GROUPS]
    counts = [csum[g, ts - 1].astype(I32) * active for g in range(N_GROUPS)]
    st["nrows"] = [(c + (ROW_ALIGN - 1)) & ~(ROW_ALIGN - 1) for c in counts]
    q = jnp.zeros((1, ts), F32)
    tile_off = 0
    for g in range(N_GROUPS):
        q = q + onehot[g:g + 1] * (csum[g:g + 1] - 1.0 + tile_off.astype(F32)
                                   if g else csum[g:g + 1] - 1.0)
        tile_off = tile_off + st["nrows"][g]
    st["q"] = q.astype(I32)


def _sc_sort(sc, st, slot):
    gates = st["gates"]
    ts = gates.shape[1]
    hi = gates.astype(BF16).astype(F32)
    mid = (gates - hi).astype(BF16).astype(F32)
    lo = (gates - hi - mid).astype(BF16).astype(F32)
    meta_t = jnp.concatenate(
        [hi, mid, lo, jnp.zeros((LANES - 3 * EXPERTS_PER_GROUP, ts), F32)], axis=0)
    rows = jnp.concatenate([st["h2"].astype(BF16), meta_t.T.astype(BF16)], axis=1)
    sorted_rows = jnp.dot(_sort_matrix(st["q"]), rows, preferred_element_type=F32)
    sc.rowbuf[slot] = sorted_rows.astype(BF16)


def _sc_start(sc, st, t_idx, slot):
    sc.q_ref[t_idx] = st["q"]
    nrows = st["nrows"]
    rec = slot * (2 * N_GROUPS)
    for g in range(N_GROUPS):
        base = sc.fill_s[g]
        sc.pend_s[rec + g] = nrows[g]
        sc.pend_s[rec + N_GROUPS + g] = base
        sc.segn_ref[t_idx * N_GROUPS + g] = nrows[g]
        sc.segb_ref[t_idx * N_GROUPS + g] = base
        sc.fill_s[g] = base + nrows[g]
    _sc_pending(sc, "start", slot)


def _write_moe_items(sc):
    n_items = sc.items_ref.shape[0] // 4
    per_region = sc.region_rows // MOE_TILE
    ends = []
    total = 0
    for g in range(N_GROUPS):
        total = total + (sc.fill_s[g] + (MOE_TILE - 1)) // MOE_TILE
        ends.append(total)
    last = jnp.maximum(total - 1, 0)
    prev_grp = None
    for i in range(n_items):
        ic = jnp.minimum(i, last)
        grp = sum((ic >= e).astype(I32) for e in ends[:-1])
        start = jnp.where(grp == 0, 0, jnp.where(grp == 1, ends[0], jnp.where(grp == 2, ends[1], ends[2])))
        fresh = jnp.ones((), I32) if prev_grp is None else (grp != prev_grp).astype(I32)
        sc.items_ref[i] = grp * per_region + ic - start
        sc.items_ref[n_items + i] = grp
        sc.items_ref[2 * n_items + i] = (i < total).astype(I32)
        sc.items_ref[3 * n_items + i] = fresh
        prev_grp = grp


def _sc_finish(sc):
    _sc_pending(sc, "wait", 0)
    _sc_pending(sc, "wait", 1)
    sc.zbuf[...] = jnp.zeros_like(sc.zbuf)
    for op in ("start", "wait"):
        for g in range(N_GROUPS):
            fill = sc.fill_s[g]
            n = pl.multiple_of((-fill) & (MOE_TILE - 1), ROW_ALIGN)
            b = pl.multiple_of(g * sc.region_rows + fill, ROW_ALIGN)

            @pl.when(n > 0)
            def _(n=n, b=b, op=op):
                cp = pltpu.make_async_copy(sc.zbuf.at[pl.ds(0, n)], sc.hs_ref.at[pl.ds(b, n)], sc.zsem)
                getattr(cp, op)()
    _write_moe_items(sc)


def _gather_prev_ffn(tile, n_tiles, segb_ref, segn_ref, ys_ref, q_ref, ybuf, gsem):
    region_rows = ys_ref.shape[0] // N_GROUPS

    def copies(op, t, slot):
        nrows = [segn_ref[t * N_GROUPS + g] for g in range(N_GROUPS)]
        bases = [segb_ref[t * N_GROUPS + g] for g in range(N_GROUPS)]
        src_of = lambda a, b, size: ys_ref.at[pl.ds(b, size)]
        dst_of = lambda a, b, size: ybuf.at[slot, pl.ds(a, size)]
        _run_copies(op, src_of, dst_of, gsem.at[slot], nrows, bases, region_rows)

    def issue(t, slot):
        ybuf[slot, SEQ_TILE:SORT_ROWS, :] = jnp.zeros((SORT_ROWS - SEQ_TILE, D_MODEL), BF16)
        copies("start", t, slot)

    @pl.when(tile == 0)
    def _():
        issue(tile, 0)

    @pl.when(tile + 1 < n_tiles)
    def _():
        issue(tile + 1, (tile + 1) % 2)

    slot = tile % 2
    copies("wait", tile, slot)
    p = _sort_matrix(q_ref[0])
    return lax.dot_general(p, ybuf[slot], (((0,), (0,)), ((), ())), preferred_element_type=F32)


def _load_weight_bf16(w_hbm, w_vmem, stage, sem):
    rows = stage.shape[1]
    n = w_vmem.shape[0] // rows
    copy = lambda c: pltpu.make_async_copy(w_hbm.at[pl.ds(c * rows, rows)], stage.at[c % 2],
                                           sem.at[c % 2])
    copy(0).start()
    for c in range(n):
        if c + 1 < n:
            copy(c + 1).start()
        copy(c).wait()
        w_vmem[c * rows:(c + 1) * rows, :] = stage[c % 2].astype(BF16)


def _weight_scratch(k, n):
    return [pltpu.VMEM((k, n), BF16), pltpu.VMEM((2, W_STAGE_ROWS, n), F32),
            pltpu.SemaphoreType.DMA((2,))]


def _hybrid_kernel(x_ref, mod_ref, win_hbm, wout_hbm, gmix_ref, gffn_ref, lng_ref, lnb_ref, sw_ref,
                   sb_ref, cw_ref, rw_ref, rb_ref,
                   x1_ref, hs_ref, q_ref, segb_ref, segn_ref, items_ref,
                   pbuf, win_ref, win_stage, win_sem, wout_ref, wout_stage, wout_sem,
                   h2buf, rowbuf, tri, zbuf, fill_s, pend_s, ssem, zsem):
    ts = x_ref.shape[1]
    tile = pl.program_id(0) * pl.num_programs(1) + pl.program_id(1)
    n_tiles = pl.num_programs(0) * pl.num_programs(1)
    sc = _ScatterRefs(gffn_ref, rw_ref, rb_ref, hs_ref, q_ref, segb_ref, segn_ref, items_ref,
                      h2buf, rowbuf, tri, zbuf, fill_s, pend_s, ssem, zsem)

    @pl.when(tile == 0)
    def _():
        _load_weight_bf16(win_hbm.at[0], win_ref, win_stage, win_sem)
        _load_weight_bf16(wout_hbm.at[0], wout_ref, wout_stage, wout_sem)

    @pl.when(pl.program_id(1) == 0)
    def _():
        pbuf[0:B_HALO, :] = jnp.zeros((B_HALO, D_B), F32)

    _sc_init(sc, tile)
    slot = tile % 2
    _sc_pending(sc, "wait", slot)

    prev = {}
    active = (tile > 0).astype(I32)

    x = x_ref[0]
    mod = mod_ref[0, 0]
    sh1, sc1, g1 = mod[0:1], mod[1:2], mod[2:3]
    h = _rms_mod(x, gmix_ref[...], sh1, sc1)
    hb16 = h.astype(BF16)
    zcol = lambda j: jnp.dot(hb16, win_ref[:, j * D_A:(j + 1) * D_A], preferred_element_type=F32)

    v = _layer_norm(_gelu(zcol(1)), lng_ref[...], lnb_ref[...]).astype(BF16)
    _sc_route(sc, prev)
    u = _gelu(zcol(0))
    tril = (lax.broadcasted_iota(I32, (CHUNK, CHUNK), 0)
            >= lax.broadcasted_iota(I32, (CHUNK, CHUNK), 1))
    chunks = []
    for n in range(ts // CHUNK):
        heads = []
        for hd in range(A_HEADS):
            w = jnp.where(tril, sw_ref[hd], 0.0).astype(BF16)
            vb = v[n * CHUNK:(n + 1) * CHUNK, hd * A_HEAD_DIM:(hd + 1) * A_HEAD_DIM]
            heads.append(jnp.dot(w, vb, preferred_element_type=F32) + sb_ref[hd])
        chunks.append(jnp.concatenate(heads, axis=1))
    y_a = u * jnp.concatenate(chunks, axis=0)
    _sc_rank(sc, prev, active)

    p = zcol(3) * zcol(4)
    pbuf[B_HALO:B_HALO + ts, :] = p
    cw = cw_ref[...]
    conv = (cw[0:1] * pbuf[B_HALO - 2:B_HALO - 2 + ts, :]
            + cw[1:2] * pbuf[B_HALO - 1:B_HALO - 1 + ts, :]
            + cw[2:3] * p)
    pbuf[0:B_HALO, :] = p[ts - B_HALO:ts, :]
    y_b = zcol(2) * conv
    _sc_sort(sc, prev, slot)

    ycat = jnp.concatenate([y_a, y_b], axis=1).astype(BF16)
    y = jnp.dot(ycat, wout_ref[...], preferred_element_type=F32)
    x1 = x + g1 * y
    x1_ref[0] = x1
    h2buf[...] = _prenorm(x1, mod, gffn_ref)
    _sc_start(sc, prev, jnp.maximum(tile - 1, 0), slot)

    @pl.when(tile == n_tiles - 1)
    def _():
        _sc_pending(sc, "wait", 1 - slot)
        last = {}
        _sc_route(sc, last)
        _sc_rank(sc, last, 1)
        _sc_sort(sc, last, 1 - slot)
        _sc_start(sc, last, tile, 1 - slot)
        _sc_finish(sc)


def _band_matrix():
    shape = (CONV_ROWS, SUBLANES * S_ROWS)
    t = lax.broadcasted_iota(I32, shape, 0)
    c = lax.broadcasted_iota(I32, shape, 1)
    hit = c == t + S_LEAD
    for r in range(1, SUBLANES):
        hit = hit | (c == t + S_LEAD + r * (S_ROWS - 1))
    return jnp.where(hit, 1.0, 0.0).astype(BF16)


def _causal_dw_conv(cbuf, w_ref, obuf, band, sbuf, row0, nrows, col0, ncols, first_slot):
    n_a = -(-C_CONV // SUBLANES)
    cols = slice(col0, col0 + ncols)
    for i, t0 in enumerate(range(row0, row0 + nrows, CONV_ROWS)):
        stack = sbuf.at[(first_slot + i) % sbuf.shape[0]]
        for r in range(SUBLANES):
            part = None
            for a in range(n_a):
                k = C_CONV - 1 - SUBLANES * a - r
                if k < 0:
                    continue
                r0 = t0 + C_HALO - S_LEAD - SUBLANES * a
                term = w_ref[k:k + 1, cols] * cbuf[r0:r0 + S_ROWS, cols]
                part = term if part is None else part + term
            stack[r * S_ROWS:(r + 1) * S_ROWS, :] = part.astype(BF16)
        obuf[t0:t0 + CONV_ROWS, cols] = jnp.dot(band[...], stack[...], preferred_element_type=F32)


def _conformer_kernel(psegb_ref, psegn_ref,
                      xp_ref, ys_ref, pq_ref, pmod_ref, mod_ref, w1_hbm, w2_hbm, gmix_ref, gffn_ref,
                      b1_ref, dww_ref, dwb_ref, lng_ref, lnb_ref, b2_ref, rw_ref, rb_ref,
                      x1_ref, hs_ref, q_ref, segb_ref, segn_ref, items_ref,
                      cbuf, obuf, band, sbuf, ybuf, gsem,
                      w1_ref, w1_stage, w1_sem, w2_ref, w2_stage, w2_sem,
                      h2buf, rowbuf, tri, zbuf, fill_s, pend_s, ssem, zsem):
    ts = xp_ref.shape[1]
    tile = pl.program_id(0) * pl.num_programs(1) + pl.program_id(1)
    n_tiles = pl.num_programs(0) * pl.num_programs(1)
    sc = _ScatterRefs(gffn_ref, rw_ref, rb_ref, hs_ref, q_ref, segb_ref, segn_ref, items_ref,
                      h2buf, rowbuf, tri, zbuf, fill_s, pend_s, ssem, zsem)

    @pl.when(tile == 0)
    def _():
        _load_weight_bf16(w1_hbm.at[0], w1_ref, w1_stage, w1_sem)
        _load_weight_bf16(w2_hbm.at[0], w2_ref, w2_stage, w2_sem)
        band[...] = _band_matrix()

    @pl.when(pl.program_id(1) == 0)
    def _():
        cbuf[0:C_HALO, :] = jnp.zeros((C_HALO, D_C), F32)

    _sc_init(sc, tile)
    slot = tile % 2
    _sc_pending(sc, "wait", slot)
    y_prev = _gather_prev_ffn(tile, n_tiles, psegb_ref, psegn_ref, ys_ref, pq_ref, ybuf, gsem)

    prev = {}
    active = (tile > 0).astype(I32)
    deferred = [lambda: _sc_route(sc, prev), lambda: _sc_rank(sc, prev, active),
                lambda: _sc_sort(sc, prev, slot)]

    x = xp_ref[0] + pmod_ref[0, 0][5:6] * y_prev
    mod = mod_ref[0, 0]
    sh1, sc1, g1 = mod[0:1], mod[1:2], mod[2:3]
    h = _rms_mod(x, gmix_ref[...], sh1, sc1)
    hb = h.astype(BF16)
    def glu(c0):
        ca, cg = slice(c0, c0 + GLU_COLS), slice(D_C + c0, D_C + c0 + GLU_COLS)
        za = jnp.dot(hb, w1_ref[:, ca], preferred_element_type=F32) + b1_ref[:, ca]
        zg = jnp.dot(hb, w1_ref[:, cg], preferred_element_type=F32) + b1_ref[:, cg]
        cbuf[C_HALO:C_HALO + ts, ca] = za * jax.nn.sigmoid(zg)

    glu(0)
    for k, c0 in enumerate(range(0, D_C, GLU_COLS)):
        if c0 + GLU_COLS < D_C:
            glu(c0 + GLU_COLS)
        if k < len(deferred):
            deferred[k]()
        _causal_dw_conv(cbuf, dww_ref, obuf, band, sbuf, 0, ts, c0, GLU_COLS,
                        k * (ts // CONV_ROWS))
    cbuf[0:C_HALO, :] = cbuf[ts:ts + C_HALO, :]

    zc = _layer_norm(obuf[...] + dwb_ref[...], lng_ref[...], lnb_ref[...])
    zc = zc * jax.nn.sigmoid(zc)
    y = jnp.dot(zc.astype(BF16), w2_ref[...], preferred_element_type=F32) + b2_ref[...]
    x1 = x + g1 * y
    x1_ref[0] = x1
    h2buf[...] = _prenorm(x1, mod, gffn_ref)
    _sc_start(sc, prev, jnp.maximum(tile - 1, 0), slot)

    @pl.when(tile == n_tiles - 1)
    def _():
        _sc_pending(sc, "wait", 1 - slot)
        last = {}
        _sc_route(sc, last)
        _sc_rank(sc, last, 1)
        _sc_sort(sc, last, 1 - slot)
        _sc_start(sc, last, tile, 1 - slot)
        _sc_finish(sc)


def _final_kernel(psegb_ref, psegn_ref, xp_ref, ys_ref, pq_ref, pmod_ref, fg_ref, o_ref, ybuf, gsem):
    tile = pl.program_id(0) * pl.num_programs(1) + pl.program_id(1)
    n_tiles = pl.num_programs(0) * pl.num_programs(1)
    y_prev = _gather_prev_ffn(tile, n_tiles, psegb_ref, psegn_ref, ys_ref, pq_ref, ybuf, gsem)
    x = xp_ref[0] + pmod_ref[0, 0][5:6] * y_prev
    o_ref[0] = _rms(x, fg_ref[...])


def _scatter_outputs(bsz, seq, d):
    t = bsz * seq
    n_tiles = t // SEQ_TILE
    smem = pl.BlockSpec(memory_space=pltpu.SMEM)
    specs = [pl.BlockSpec((1, SEQ_TILE, d), lambda b, s, *_: (b, s, 0)),
             pl.BlockSpec(memory_space=pl.ANY),
             pl.BlockSpec((n_tiles, 1, SEQ_TILE), lambda b, s, *_: (0, 0, 0)),
             smem, smem, smem]
    shapes = [jax.ShapeDtypeStruct((bsz, seq, d), F32),
              jax.ShapeDtypeStruct((N_GROUPS * _region_rows(t), ROW_W), BF16),
              jax.ShapeDtypeStruct((n_tiles, 1, SEQ_TILE), I32),
              jax.ShapeDtypeStruct((n_tiles * N_GROUPS,), I32),
              jax.ShapeDtypeStruct((n_tiles * N_GROUPS,), I32),
              jax.ShapeDtypeStruct((4 * _moe_steps(t),), I32)]
    scratch = [pltpu.VMEM((SEQ_TILE, d), F32),
               pltpu.VMEM((2, SORT_ROWS, ROW_W), BF16),
               pltpu.VMEM((SEQ_TILE, SEQ_TILE), BF16),
               pltpu.VMEM((MOE_TILE, ROW_W), BF16),
               pltpu.SMEM((N_GROUPS,), I32),
               pltpu.SMEM((4 * N_GROUPS,), I32),
               pltpu.SemaphoreType.DMA((2,)),
               pltpu.SemaphoreType.DMA(())]
    return specs, shapes, scratch


def _mod_spec(d, layer):
    return pl.BlockSpec((1, 1, N_MOD, d), lambda b, s, *_: (layer, b, 0, 0))


def _gather_inputs(bsz, seq, d, prev_layer):
    nseq = seq // SEQ_TILE
    specs = [pl.BlockSpec((1, SEQ_TILE, d), lambda b, s, *_: (b, s, 0)),
             pl.BlockSpec(memory_space=pl.ANY),
             pl.BlockSpec((1, 1, SEQ_TILE), lambda b, s, *_: (b * nseq + s, 0, 0)),
             _mod_spec(d, prev_layer)]
    scratch = [pltpu.VMEM((2, SORT_ROWS, d), BF16), pltpu.SemaphoreType.DMA((2,))]
    return specs, scratch


def _const_specs(consts):
    return [pl.BlockSpec(a.shape, lambda b, s, *_, nd=a.ndim: (0,) * nd) for a in consts]


_PARAMS = pltpu.CompilerParams(dimension_semantics=("arbitrary", "arbitrary"),
                               vmem_limit_bytes=VMEM_LIMIT)


def _hybrid_call(x, mod, layer, w_in, w_out, consts):
    bsz, seq, d = x.shape
    out_specs, out_shapes, sc_scratch = _scatter_outputs(bsz, seq, d)
    hbm = pl.BlockSpec(memory_space=pl.ANY)
    return pl.pallas_call(
        _hybrid_kernel,
        grid=(bsz, seq // SEQ_TILE),
        in_specs=[pl.BlockSpec((1, SEQ_TILE, d), lambda b, s: (b, s, 0)),
                  _mod_spec(d, layer), hbm, hbm]
                 + _const_specs(consts),
        out_specs=out_specs,
        out_shape=out_shapes,
        scratch_shapes=[pltpu.VMEM((SEQ_TILE + B_HALO, D_B), F32)]
                       + _weight_scratch(*w_in.shape[1:]) + _weight_scratch(*w_out.shape[1:])
                       + sc_scratch,
        compiler_params=_PARAMS,
        name="mixer_hybrid",
    )(x, mod, w_in, w_out, *consts)


def _conformer_call(prev, mod, layer, w1, w2, consts):
    x1p, ysp, qp, segbp, segnp = prev
    bsz, seq, d = x1p.shape
    g_specs, g_scratch = _gather_inputs(bsz, seq, d, layer - 1)
    out_specs, out_shapes, sc_scratch = _scatter_outputs(bsz, seq, d)
    hbm = pl.BlockSpec(memory_space=pl.ANY)
    grid_spec = pltpu.PrefetchScalarGridSpec(
        num_scalar_prefetch=2,
        grid=(bsz, seq // SEQ_TILE),
        in_specs=g_specs + [_mod_spec(d, layer), hbm, hbm]
                 + _const_specs(consts),
        out_specs=out_specs,
        scratch_shapes=[pltpu.VMEM((SEQ_TILE + C_HALO, D_C), F32), pltpu.VMEM((SEQ_TILE, D_C), F32),
                        pltpu.VMEM((CONV_ROWS, SUBLANES * S_ROWS), BF16),
                        pltpu.VMEM((S_SLOTS, SUBLANES * S_ROWS, GLU_COLS), BF16)]
                       + g_scratch + _weight_scratch(*w1.shape[1:]) + _weight_scratch(*w2.shape[1:])
                       + sc_scratch,
    )
    return pl.pallas_call(
        _conformer_kernel, grid_spec=grid_spec, out_shape=out_shapes,
        compiler_params=_PARAMS, name="mixer_conformer",
    )(segbp, segnp, x1p, ysp, qp, mod, mod, w1, w2, *consts)


def _final_call(prev, mod, prev_layer, final_g):
    x1p, ysp, qp, segbp, segnp = prev
    bsz, seq, d = x1p.shape
    g_specs, g_scratch = _gather_inputs(bsz, seq, d, prev_layer)
    grid_spec = pltpu.PrefetchScalarGridSpec(
        num_scalar_prefetch=2,
        grid=(bsz, seq // SEQ_TILE),
        in_specs=g_specs + [pl.BlockSpec((1, d), lambda b, s, *_: (0, 0))],
        out_specs=pl.BlockSpec((1, SEQ_TILE, d), lambda b, s, *_: (b, s, 0)),
        scratch_shapes=g_scratch,
    )
    return pl.pallas_call(
        _final_kernel, grid_spec=grid_spec,
        out_shape=jax.ShapeDtypeStruct((bsz, seq, d), F32),
        compiler_params=_PARAMS, name="final_norm",
    )(segbp, segnp, x1p, ysp, qp, mod, final_g.reshape(1, d))


def _moe_kernel(items_ref, hs_ref, wg_ref, wu_ref, wd_ref, ys_ref, wg_bf, wu_bf, wd_bf):
    i = pl.program_id(0)
    n_items = pl.num_programs(0)

    @pl.when(items_ref[3 * n_items + i] != 0)
    def _():
        wg_bf[...] = wg_ref[0].astype(BF16)
        wu_bf[...] = wu_ref[0].astype(BF16)
        wd_bf[...] = wd_ref[0].astype(BF16)

    @pl.when(items_ref[2 * n_items + i] != 0)
    def _():
        rows = hs_ref[...]
        h = rows[:, :D_MODEL]
        m = rows[:, D_MODEL:].astype(F32)
        ne = EXPERTS_PER_GROUP
        y = None
        for j in range(ne):
            gate = m[:, j:j + 1] + (m[:, ne + j:ne + j + 1] + m[:, 2 * ne + j:2 * ne + j + 1])
            a = jnp.dot(h, wg_bf[j], preferred_element_type=F32)
            a = a * jax.nn.sigmoid(a) * jnp.dot(h, wu_bf[j], preferred_element_type=F32)
            yj = jnp.dot((a * gate).astype(BF16), wd_bf[j], preferred_element_type=F32)
            y = yj if y is None else y + yj
        ys_ref[...] = y.astype(BF16)


def _moe_call(hs, items, layer, wg, wu, wd):
    d = D_MODEL
    ne = EXPERTS_PER_GROUP
    n_items = items.shape[0] // 4
    w_idx = lambda i, items: (layer * N_GROUPS + items[n_items + i], 0, 0, 0)
    row_idx = lambda i, items: (items[i], 0)
    grid_spec = pltpu.PrefetchScalarGridSpec(
        num_scalar_prefetch=1,
        grid=(n_items,),
        in_specs=[pl.BlockSpec((MOE_TILE, ROW_W), row_idx),
                  pl.BlockSpec((1, ne, d, D_EXPERT), w_idx),
                  pl.BlockSpec((1, ne, d, D_EXPERT), w_idx),
                  pl.BlockSpec((1, ne, D_EXPERT, d), w_idx)],
        out_specs=pl.BlockSpec((MOE_TILE, d), row_idx),
        scratch_shapes=[pltpu.VMEM((ne, d, D_EXPERT), BF16), pltpu.VMEM((ne, d, D_EXPERT), BF16),
                        pltpu.VMEM((ne, D_EXPERT, d), BF16)],
    )
    return pl.pallas_call(
        _moe_kernel, grid_spec=grid_spec,
        out_shape=jax.ShapeDtypeStruct((hs.shape[0], d), BF16),
        compiler_params=pltpu.CompilerParams(dimension_semantics=("arbitrary",),
                                             vmem_limit_bytes=VMEM_LIMIT),
        name="moe",
    )(items, hs, wg, wu, wd)


def kernel(x, c, ada_w, ada_b, norm_mix_g, norm_ffn_g, hyb_w_in, sgu_ln_g, sgu_ln_b, sgu_w, sgu_b,
           sconv_w, hyb_w_out, conf_w1, conf_b1, conf_dw_w, conf_dw_b, conf_ln_g, conf_ln_b,
           conf_w2, conf_b2, router_w, router_bias, moe_w_gate, moe_w_up, moe_w_down, final_norm_g):
    depth = ada_w.shape[0]
    bsz, seq, d = x.shape
    t = bsz * seq
    assert depth == 2 and seq % SEQ_TILE == 0
    mod = _ada_mod(c, ada_w, ada_b).reshape(depth, bsz, N_MOD, d)

    perm = jnp.arange(N_EXPERTS).reshape(N_GROUPS, EXPERTS_PER_GROUP).T.reshape(-1)
    rw_t = router_w.T[perm]
    rb_t = router_bias[perm].reshape(N_EXPERTS, 1)

    ne = EXPERTS_PER_GROUP
    wg = moe_w_gate.reshape(depth * N_GROUPS, ne, d, D_EXPERT)
    wu = moe_w_up.reshape(depth * N_GROUPS, ne, d, D_EXPERT)
    wd = moe_w_down.reshape(depth * N_GROUPS, ne, D_EXPERT, d)

    row = lambda a: a.reshape(1, -1)
    consts0 = [row(norm_mix_g[0]), row(norm_ffn_g[0]), row(sgu_ln_g[0]), row(sgu_ln_b[0]), sgu_w[0],
               sgu_b[0].reshape(A_HEADS, CHUNK, 1), sconv_w[0], rw_t, rb_t]
    x1, hs, q, segb, segn, items = _hybrid_call(x, mod, 0, hyb_w_in, hyb_w_out, consts0)
    ys = _moe_call(hs, items, 0, wg, wu, wd)

    consts1 = [row(norm_mix_g[1]), row(norm_ffn_g[1]), row(conf_b1[0]), conf_dw_w[0],
               row(conf_dw_b[0]), row(conf_ln_g[0]), row(conf_ln_b[0]), row(conf_b2[0]), rw_t, rb_t]
    x1, hs, q1, segb1, segn1, items = _conformer_call((x1, ys, q, segb, segn), mod, 1,
                                                      conf_w1, conf_w2, consts1)
    ys = _moe_call(hs, items, 1, wg, wu, wd)
    return _final_call((x1, ys, q1, segb1, segn1), mod, 1, final_norm_g)
```

```python
import functools

import jax
import jax.numpy as jnp
from jax import lax
from jax.experimental import pallas as pl
from jax.experimental.pallas import tpu as pltpu

D_MODEL = 1024
CHUNK = 128
A_HEADS = 4
A_HEAD_DIM = 128
D_A = A_HEADS * A_HEAD_DIM
D_B = 512
B_CONV = 3
D_IN_HYB = 2 * D_A + 3 * D_B
D_C = D_MODEL
C_CONV = 31
N_EXPERTS = 16
N_GROUPS = 4
EXPERTS_PER_GROUP = N_EXPERTS // N_GROUPS
D_EXPERT = 256
N_MOD = 6
EPS = 1e-6

SUBLANES = 8
LANES = 128
ROW_ALIGN = 16

SEQ_TILE = 512
MOE_TILE = 1024
MOD_TILE = 2048
W_STAGE_ROWS = 128
C_HALO = 40
B_HALO = 8
CONV_ROWS = 128
GLU_COLS = 256
S_LEAD = ROW_ALIGN
S_ROWS = CONV_ROWS + S_LEAD
S_SLOTS = 2
SORT_ROWS = SEQ_TILE + N_GROUPS * ROW_ALIGN
ROW_W = D_MODEL + LANES
VMEM_LIMIT = 56 * 1024 * 1024

F32 = jnp.float32
BF16 = jnp.bfloat16
I32 = jnp.int32


def _round_up(n, m):
    return -(-n // m) * m


def _moe_steps(t):
    return -(-(t + (t // SEQ_TILE) * N_GROUPS * (ROW_ALIGN - 1)) // MOE_TILE) + N_GROUPS


def _region_rows(t):
    return _round_up(t + (t // SEQ_TILE) * ROW_ALIGN, MOE_TILE)


def _split_bf16(a):
    hi = a.astype(BF16)
    lo = (a - hi.astype(F32)).astype(BF16)
    return hi, lo


def _dot3(a, b):
    ah, al = _split_bf16(a)
    bh, bl = _split_bf16(b)
    d = functools.partial(jnp.dot, preferred_element_type=F32)
    return d(ah, bh) + (d(ah, bl) + d(al, bh))


def _rms(x, g):
    return x * lax.rsqrt(jnp.mean(x * x, axis=-1, keepdims=True) + EPS) * g


def _rms_mod(x, g, shift, scale):
    return x * lax.rsqrt(jnp.mean(x * x, axis=-1, keepdims=True) + EPS) * (g * (1.0 + scale)) + shift


def _gelu(x):
    return 0.5 * x * (1.0 + lax.erf(x * (2.0 ** -0.5)))


def _layer_norm(x, g, b):
    mu = jnp.mean(x, axis=-1, keepdims=True)
    xc = x - mu
    var = jnp.mean(xc * xc, axis=-1, keepdims=True)
    return xc * lax.rsqrt(var + EPS) * g + b


def _mod_kernel(c_ref, w_ref, b_ref, o_ref):
    c = c_ref[...]
    c_act = c * jax.nn.sigmoid(c)
    o_ref[0] = _dot3(c_act, w_ref[0]) + b_ref[0]


def _ada_mod(c, ada_w, ada_b):
    depth, d, n = ada_w.shape
    bsz = c.shape[0]
    return pl.pallas_call(
        _mod_kernel,
        grid=(depth, n // MOD_TILE),
        in_specs=[
            pl.BlockSpec((bsz, d), lambda l, j: (0, 0)),
            pl.BlockSpec((1, d, MOD_TILE), lambda l, j: (l, 0, j)),
            pl.BlockSpec((1, 1, MOD_TILE), lambda l, j: (l, 0, j)),
        ],
        out_specs=pl.BlockSpec((1, bsz, MOD_TILE), lambda l, j: (l, 0, j)),
        out_shape=jax.ShapeDtypeStruct((depth, bsz, n), F32),
        compiler_params=pltpu.CompilerParams(
            dimension_semantics=("arbitrary", "arbitrary"), vmem_limit_bytes=VMEM_LIMIT),
        name="ada_mod",
    )(c, ada_w, ada_b.reshape(depth, 1, n))


def _route(h2, rw_ref, rb_ref):
    wh, wl = _split_bf16(rw_ref[...])
    hh, hl = _split_bf16(h2)
    nt = functools.partial(lax.dot_general, dimension_numbers=(((1,), (1,)), ((), ())),
                           preferred_element_type=F32)
    logits = nt(wh, hh) + (nt(wh, hl) + nt(wl, hh))
    scores = jax.nn.sigmoid(logits)
    biased = scores + rb_ref[...]
    ng, ne = N_GROUPS, EXPERTS_PER_GROUP
    b = [biased[j * ng:(j + 1) * ng] for j in range(ne)]
    s = [scores[j * ng:(j + 1) * ng] for j in range(ne)]
    gscore = None
    for i in range(ne):
        for j in range(i + 1, ne):
            pair = b[i] + b[j]
            gscore = pair if gscore is None else jnp.maximum(gscore, pair)
    gid = lax.broadcasted_iota(I32, gscore.shape, 0)
    gmax = jnp.max(gscore, axis=0, keepdims=True)
    best = jnp.min(jnp.where(gscore == gmax, gid, ng), axis=0, keepdims=True)
    onb = gid == best
    cb = [jnp.sum(jnp.where(onb, b[j], 0.0), axis=0, keepdims=True) for j in range(ne)]
    cs = [jnp.sum(jnp.where(onb, s[j], 0.0), axis=0, keepdims=True) for j in range(ne)]
    sel = []
    for j in range(ne):
        rank = jnp.zeros_like(best)
        for i in range(ne):
            if i == j:
                continue
            ahead = (cb[i] >= cb[j]) if i < j else (cb[i] > cb[j])
            rank = rank + ahead.astype(I32)
        sel.append(rank < 2)
    wsum = sum(jnp.where(sel[j], cs[j], 0.0) for j in range(ne))
    gates = jnp.concatenate([jnp.where(sel[j], cs[j], 0.0) / wsum for j in range(ne)], axis=0)
    return gates, jnp.where(onb, 1.0, 0.0)


def _sort_matrix(q):
    rows = lax.broadcasted_iota(I32, (SORT_ROWS, q.shape[1]), 0)
    return jnp.where(rows == q, 1.0, 0.0).astype(BF16)


def _run_copies(op, src_of, dst_of, sem, nrows, bases, region_rows):
    if op == "wait":
        total = pl.multiple_of(sum(nrows[1:], nrows[0]), ROW_ALIGN)

        @pl.when(total > 0)
        def _():
            pltpu.make_async_copy(src_of(0, 0, total), dst_of(0, 0, total), sem).wait()
        return
    tile_off = 0
    for g in range(N_GROUPS):
        n = pl.multiple_of(nrows[g], ROW_ALIGN)
        a = pl.multiple_of(tile_off, ROW_ALIGN)
        b = pl.multiple_of(g * region_rows + bases[g], ROW_ALIGN)

        @pl.when(n > 0)
        def _(a=a, b=b, n=n):
            pltpu.make_async_copy(src_of(a, b, n), dst_of(a, b, n), sem).start()

        tile_off = tile_off + nrows[g]


class _ScatterRefs:
    def __init__(self, gffn_ref, rw_ref, rb_ref, hs_ref, q_ref, segb_ref, segn_ref, items_ref,
                 h2buf, rowbuf, tri, zbuf, fill_s, pend_s, ssem, zsem):
        self.__dict__.update(locals())
        self.region_rows = hs_ref.shape[0] // N_GROUPS
        self.dst_of = lambda a, b, size: hs_ref.at[pl.ds(b, size)]


def _sc_init(sc, tile):
    @pl.when(tile == 0)
    def _():
        for g in range(N_GROUPS):
            sc.fill_s[g] = 0
        for i in range(sc.pend_s.shape[0]):
            sc.pend_s[i] = 0
        ts = sc.tri.shape[0]
        k = lax.broadcasted_iota(I32, (ts, ts), 0)
        t = lax.broadcasted_iota(I32, (ts, ts), 1)
        sc.tri[...] = jnp.where(k <= t, 1.0, 0.0).astype(BF16)
        sc.h2buf[...] = jnp.zeros_like(sc.h2buf)


def _sc_pending(sc, op, slot):
    rec = slot * (2 * N_GROUPS)
    src_of = lambda a, b, size: sc.rowbuf.at[slot, pl.ds(a, size)]
    _run_copies(op, src_of, sc.dst_of, sc.ssem.at[slot], [sc.pend_s[rec + g] for g in range(N_GROUPS)],
                [sc.pend_s[rec + N_GROUPS + g] for g in range(N_GROUPS)], sc.region_rows)


def _prenorm(x1, mod, gffn_ref):
    sh2, sc2 = mod[3:4], mod[4:5]
    return _rms_mod(x1, gffn_ref[...], sh2, sc2)


def _sc_route(sc, st):
    st["h2"] = sc.h2buf[...]
    st["gates"], st["onehot"] = _route(st["h2"], sc.rw_ref, sc.rb_ref)


def _sc_rank(sc, st, active):
    onehot = st["onehot"]
    ts = onehot.shape[1]
    oh8 = jnp.concatenate([onehot, jnp.zeros((SUBLANES - N_GROUPS, ts), F32)], axis=0).astype(BF16)
    csum = jnp.dot(oh8, sc.tri[...], preferred_element_type=F32)[:N_GROUPS]
    counts = [csum[g, ts - 1].astype(I32) * active for g in range(N_GROUPS)]
    st["nrows"] = [(c + (ROW_ALIGN - 1)) & ~(ROW_ALIGN - 1) for c in counts]
    q = jnp.zeros((1, ts), F32)
    tile_off = 0
    for g in range(N_GROUPS):
        q = q + onehot[g:g + 1] * (csum[g:g + 1] - 1.0 + tile_off.astype(F32)
                                   if g else csum[g:g + 1] - 1.0)
        tile_off = tile_off + st["nrows"][g]
    st["q"] = q.astype(I32)


def _sc_sort(sc, st, slot):
    gates = st["gates"]
    ts = gates.shape[1]
    hi = gates.astype(BF16).astype(F32)
    mid = (gates - hi).astype(BF16).astype(F32)
    lo = (gates - hi - mid).astype(BF16).astype(F32)
    meta_t = jnp.concatenate(
        [hi, mid, lo, jnp.zeros((LANES - 3 * EXPERTS_PER_GROUP, ts), F32)], axis=0)
    rows = jnp.concatenate([st["h2"].astype(BF16), meta_t.T.astype(BF16)], axis=1)
    sorted_rows = jnp.dot(_sort_matrix(st["q"]), rows, preferred_element_type=F32)
    sc.rowbuf[slot] = sorted_rows.astype(BF16)


def _sc_start(sc, st, t_idx, slot):
    sc.q_ref[t_idx] = st["q"]
    nrows = st["nrows"]
    rec = slot * (2 * N_GROUPS)
    for g in range(N_GROUPS):
        base = sc.fill_s[g]
        sc.pend_s[rec + g] = nrows[g]
        sc.pend_s[rec + N_GROUPS + g] = base
        sc.segn_ref[t_idx * N_GROUPS + g] = nrows[g]
        sc.segb_ref[t_idx * N_GROUPS + g] = base
        sc.fill_s[g] = base + nrows[g]
    _sc_pending(sc, "start", slot)


def _write_moe_items(sc):
    n_items = sc.items_ref.shape[0] // 4
    per_region = sc.region_rows // MOE_TILE
    ends = []
    total = 0
    for g in range(N_GROUPS):
        total = total + (sc.fill_s[g] + (MOE_TILE - 1)) // MOE_TILE
        ends.append(total)
    last = jnp.maximum(total - 1, 0)
    prev_grp = None
    for i in range(n_items):
        ic = jnp.minimum(i, last)
        grp = sum((ic >= e).astype(I32) for e in ends[:-1])
        start = jnp.where(grp == 0, 0, jnp.where(grp == 1, ends[0], jnp.where(grp == 2, ends[1], ends[2])))
        fresh = jnp.ones((), I32) if prev_grp is None else (grp != prev_grp).astype(I32)
        sc.items_ref[i] = grp * per_region + ic - start
        sc.items_ref[n_items + i] = grp
        sc.items_ref[2 * n_items + i] = (i < total).astype(I32)
        sc.items_ref[3 * n_items + i] = fresh
        prev_grp = grp


def _sc_finish(sc):
    _sc_pending(sc, "wait", 0)
    _sc_pending(sc, "wait", 1)
    sc.zbuf[...] = jnp.zeros_like(sc.zbuf)
    for op in ("start", "wait"):
        for g in range(N_GROUPS):
            fill = sc.fill_s[g]
            n = pl.multiple_of((-fill) & (MOE_TILE - 1), ROW_ALIGN)
            b = pl.multiple_of(g * sc.region_rows + fill, ROW_ALIGN)

            @pl.when(n > 0)
            def _(n=n, b=b, op=op):
                cp = pltpu.make_async_copy(sc.zbuf.at[pl.ds(0, n)], sc.hs_ref.at[pl.ds(b, n)], sc.zsem)
                getattr(cp, op)()
    _write_moe_items(sc)


def _gather_prev_ffn(tile, n_tiles, segb_ref, segn_ref, ys_ref, q_ref, ybuf, gsem):
    region_rows = ys_ref.shape[0] // N_GROUPS

    def copies(op, t, slot):
        nrows = [segn_ref[t * N_GROUPS + g] for g in range(N_GROUPS)]
        bases = [segb_ref[t * N_GROUPS + g] for g in range(N_GROUPS)]
        src_of = lambda a, b, size: ys_ref.at[pl.ds(b, size)]
        dst_of = lambda a, b, size: ybuf.at[slot, pl.ds(a, size)]
        _run_copies(op, src_of, dst_of, gsem.at[slot], nrows, bases, region_rows)

    def issue(t, slot):
        ybuf[slot, SEQ_TILE:SORT_ROWS, :] = jnp.zeros((SORT_ROWS - SEQ_TILE, D_MODEL), BF16)
        copies("start", t, slot)

    @pl.when(tile == 0)
    def _():
        issue(tile, 0)

    @pl.when(tile + 1 < n_tiles)
    def _():
        issue(tile + 1, (tile + 1) % 2)

    slot = tile % 2
    copies("wait", tile, slot)
    p = _sort_matrix(q_ref[0])
    return lax.dot_general(p, ybuf[slot], (((0,), (0,)), ((), ())), preferred_element_type=F32)


def _load_weight_bf16(w_hbm, w_vmem, stage, sem):
    rows = stage.shape[1]
    n = w_vmem.shape[0] // rows
    copy = lambda c: pltpu.make_async_copy(w_hbm.at[pl.ds(c * rows, rows)], stage.at[c % 2],
                                           sem.at[c % 2])
    copy(0).start()
    for c in range(n):
        if c + 1 < n:
            copy(c + 1).start()
        copy(c).wait()
        w_vmem[c * rows:(c + 1) * rows, :] = stage[c % 2].astype(BF16)


def _weight_scratch(k, n):
    return [pltpu.VMEM((k, n), BF16), pltpu.VMEM((2, W_STAGE_ROWS, n), F32),
            pltpu.SemaphoreType.DMA((2,))]


def _hybrid_kernel(x_ref, mod_ref, win_hbm, wout_hbm, gmix_ref, gffn_ref, lng_ref, lnb_ref, sw_ref,
                   sb_ref, cw_ref, rw_ref, rb_ref,
                   x1_ref, hs_ref, q_ref, segb_ref, segn_ref, items_ref,
                   pbuf, win_ref, win_stage, win_sem, wout_ref, wout_stage, wout_sem,
                   h2buf, rowbuf, tri, zbuf, fill_s, pend_s, ssem, zsem):
    ts = x_ref.shape[1]
    tile = pl.program_id(0) * pl.num_programs(1) + pl.program_id(1)
    n_tiles = pl.num_programs(0) * pl.num_programs(1)
    sc = _ScatterRefs(gffn_ref, rw_ref, rb_ref, hs_ref, q_ref, segb_ref, segn_ref, items_ref,
                      h2buf, rowbuf, tri, zbuf, fill_s, pend_s, ssem, zsem)

    @pl.when(tile == 0)
    def _():
        _load_weight_bf16(win_hbm.at[0], win_ref, win_stage, win_sem)
        _load_weight_bf16(wout_hbm.at[0], wout_ref, wout_stage, wout_sem)

    @pl.when(pl.program_id(1) == 0)
    def _():
        pbuf[0:B_HALO, :] = jnp.zeros((B_HALO, D_B), F32)

    _sc_init(sc, tile)
    slot = tile % 2
    _sc_pending(sc, "wait", slot)

    prev = {}
    active = (tile > 0).astype(I32)

    x = x_ref[0]
    mod = mod_ref[0, 0]
    sh1, sc1, g1 = mod[0:1], mod[1:2], mod[2:3]
    h = _rms_mod(x, gmix_ref[...], sh1, sc1)
    hb16 = h.astype(BF16)
    zcol = lambda j: jnp.dot(hb16, win_ref[:, j * D_A:(j + 1) * D_A], preferred_element_type=F32)

    v = _layer_norm(_gelu(zcol(1)), lng_ref[...], lnb_ref[...]).astype(BF16)
    _sc_route(sc, prev)
    u = _gelu(zcol(0))
    tril = (lax.broadcasted_iota(I32, (CHUNK, CHUNK), 0)
            >= lax.broadcasted_iota(I32, (CHUNK, CHUNK), 1))
    chunks = []
    for n in range(ts // CHUNK):
        heads = []
        for hd in range(A_HEADS):
            w = jnp.where(tril, sw_ref[hd], 0.0).astype(BF16)
            vb = v[n * CHUNK:(n + 1) * CHUNK, hd * A_HEAD_DIM:(hd + 1) * A_HEAD_DIM]
            heads.append(jnp.dot(w, vb, preferred_element_type=F32) + sb_ref[hd])
        chunks.append(jnp.concatenate(heads, axis=1))
    y_a = u * jnp.concatenate(chunks, axis=0)
    _sc_rank(sc, prev, active)

    p = zcol(3) * zcol(4)
    pbuf[B_HALO:B_HALO + ts, :] = p
    cw = cw_ref[...]
    conv = (cw[0:1] * pbuf[B_HALO - 2:B_HALO - 2 + ts, :]
            + cw[1:2] * pbuf[B_HALO - 1:B_HALO - 1 + ts, :]
            + cw[2:3] * p)
    pbuf[0:B_HALO, :] = p[ts - B_HALO:ts, :]
    y_b = zcol(2) * conv
    _sc_sort(sc, prev, slot)

    ycat = jnp.concatenate([y_a, y_b], axis=1).astype(BF16)
    y = jnp.dot(ycat, wout_ref[...], preferred_element_type=F32)
    x1 = x + g1 * y
    x1_ref[0] = x1
    h2buf[...] = _prenorm(x1, mod, gffn_ref)
    _sc_start(sc, prev, jnp.maximum(tile - 1, 0), slot)

    @pl.when(tile == n_tiles - 1)
    def _():
        _sc_pending(sc, "wait", 1 - slot)
        last = {}
        _sc_route(sc, last)
        _sc_rank(sc, last, 1)
        _sc_sort(sc, last, 1 - slot)
        _sc_start(sc, last, tile, 1 - slot)
        _sc_finish(sc)


def _band_matrix():
    shape = (CONV_ROWS, SUBLANES * S_ROWS)
    t = lax.broadcasted_iota(I32, shape, 0)
    c = lax.broadcasted_iota(I32, shape, 1)
    hit = c == t + S_LEAD
    for r in range(1, SUBLANES):
        hit = hit | (c == t + S_LEAD + r * (S_ROWS - 1))
    return jnp.where(hit, 1.0, 0.0).astype(BF16)


def _causal_dw_conv(cbuf, w_ref, obuf, band, sbuf, row0, nrows, col0, ncols, first_slot):
    n_a = -(-C_CONV // SUBLANES)
    cols = slice(col0, col0 + ncols)
    for i, t0 in enumerate(range(row0, row0 + nrows, CONV_ROWS)):
        stack = sbuf.at[(first_slot + i) % sbuf.shape[0]]
        for r in range(SUBLANES):
            part = None
            for a in range(n_a):
                k = C_CONV - 1 - SUBLANES * a - r
                if k < 0:
                    continue
                r0 = t0 + C_HALO - S_LEAD - SUBLANES * a
                term = w_ref[k:k + 1, cols] * cbuf[r0:r0 + S_ROWS, cols]
                part = term if part is None else part + term
            stack[r * S_ROWS:(r + 1) * S_ROWS, :] = part.astype(BF16)
        obuf[t0:t0 + CONV_ROWS, cols] = jnp.dot(band[...], stack[...], preferred_element_type=F32)


def _conformer_kernel(psegb_ref, psegn_ref,
                      xp_ref, ys_ref, pq_ref, pmod_ref, mod_ref, w1_hbm, w2_hbm, gmix_ref, gffn_ref,
                      b1_ref, dww_ref, dwb_ref, lng_ref, lnb_ref, b2_ref, rw_ref, rb_ref,
                      x1_ref, hs_ref, q_ref, segb_ref, segn_ref, items_ref,
                      cbuf, obuf, band, sbuf, ybuf, gsem,
                      w1_ref, w1_stage, w1_sem, w2_ref, w2_stage, w2_sem,
                      h2buf, rowbuf, tri, zbuf, fill_s, pend_s, ssem, zsem):
    ts = xp_ref.shape[1]
    tile = pl.program_id(0) * pl.num_programs(1) + pl.program_id(1)
    n_tiles = pl.num_programs(0) * pl.num_programs(1)
    sc = _ScatterRefs(gffn_ref, rw_ref, rb_ref, hs_ref, q_ref, segb_ref, segn_ref, items_ref,
                      h2buf, rowbuf, tri, zbuf, fill_s, pend_s, ssem, zsem)

    @pl.when(tile == 0)
    def _():
        _load_weight_bf16(w1_hbm.at[0], w1_ref, w1_stage, w1_sem)
        _load_weight_bf16(w2_hbm.at[0], w2_ref, w2_stage, w2_sem)
        band[...] = _band_matrix()

    @pl.when(pl.program_id(1) == 0)
    def _():
        cbuf[0:C_HALO, :] = jnp.zeros((C_HALO, D_C), F32)

    _sc_init(sc, tile)
    slot = tile % 2
    _sc_pending(sc, "wait", slot)
    y_prev = _gather_prev_ffn(tile, n_tiles, psegb_ref, psegn_ref, ys_ref, pq_ref, ybuf, gsem)

    prev = {}
    active = (tile > 0).astype(I32)
    deferred = [lambda: _sc_route(sc, prev), lambda: _sc_rank(sc, prev, active),
                lambda: _sc_sort(sc, prev, slot)]

    x = xp_ref[0] + pmod_ref[0, 0][5:6] * y_prev
    mod = mod_ref[0, 0]
    sh1, sc1, g1 = mod[0:1], mod[1:2], mod[2:3]
    h = _rms_mod(x, gmix_ref[...], sh1, sc1)
    hb = h.astype(BF16)
    def glu(c0):
        ca, cg = slice(c0, c0 + GLU_COLS), slice(D_C + c0, D_C + c0 + GLU_COLS)
        za = jnp.dot(hb, w1_ref[:, ca], preferred_element_type=F32) + b1_ref[:, ca]
        zg = jnp.dot(hb, w1_ref[:, cg], preferred_element_type=F32) + b1_ref[:, cg]
        cbuf[C_HALO:C_HALO + ts, ca] = za * jax.nn.sigmoid(zg)

    glu(0)
    for k, c0 in enumerate(range(0, D_C, GLU_COLS)):
        if c0 + GLU_COLS < D_C:
            glu(c0 + GLU_COLS)
        if k < len(deferred):
            deferred[k]()
        _causal_dw_conv(cbuf, dww_ref, obuf, band, sbuf, 0, ts, c0, GLU_COLS,
                        k * (ts // CONV_ROWS))
    cbuf[0:C_HALO, :] = cbuf[ts:ts + C_HALO, :]

    zc = _layer_norm(obuf[...] + dwb_ref[...], lng_ref[...], lnb_ref[...])
    zc = zc * jax.nn.sigmoid(zc)
    y = jnp.dot(zc.astype(BF16), w2_ref[...], preferred_element_type=F32) + b2_ref[...]
    x1 = x + g1 * y
    x1_ref[0] = x1
    h2buf[...] = _prenorm(x1, mod, gffn_ref)
    _sc_start(sc, prev, jnp.maximum(tile - 1, 0), slot)

    @pl.when(tile == n_tiles - 1)
    def _():
        _sc_pending(sc, "wait", 1 - slot)
        last = {}
        _sc_route(sc, last)
        _sc_rank(sc, last, 1)
        _sc_sort(sc, last, 1 - slot)
        _sc_start(sc, last, tile, 1 - slot)
        _sc_finish(sc)


def _final_kernel(psegb_ref, psegn_ref, xp_ref, ys_ref, pq_ref, pmod_ref, fg_ref, o_ref, ybuf, gsem):
    tile = pl.program_id(0) * pl.num_programs(1) + pl.program_id(1)
    n_tiles = pl.num_programs(0) * pl.num_programs(1)
    y_prev = _gather_prev_ffn(tile, n_tiles, psegb_ref, psegn_ref, ys_ref, pq_ref, ybuf, gsem)
    x = xp_ref[0] + pmod_ref[0, 0][5:6] * y_prev
    o_ref[0] = _rms(x, fg_ref[...])


def _scatter_outputs(bsz, seq, d):
    t = bsz * seq
    n_tiles = t // SEQ_TILE
    smem = pl.BlockSpec(memory_space=pltpu.SMEM)
    specs = [pl.BlockSpec((1, SEQ_TILE, d), lambda b, s, *_: (b, s, 0)),
             pl.BlockSpec(memory_space=pl.ANY),
             pl.BlockSpec((n_tiles, 1, SEQ_TILE), lambda b, s, *_: (0, 0, 0)),
             smem, smem, smem]
    shapes = [jax.ShapeDtypeStruct((bsz, seq, d), F32),
              jax.ShapeDtypeStruct((N_GROUPS * _region_rows(t), ROW_W), BF16),
              jax.ShapeDtypeStruct((n_tiles, 1, SEQ_TILE), I32),
              jax.ShapeDtypeStruct((n_tiles * N_GROUPS,), I32),
              jax.ShapeDtypeStruct((n_tiles * N_GROUPS,), I32),
              jax.ShapeDtypeStruct((4 * _moe_steps(t),), I32)]
    scratch = [pltpu.VMEM((SEQ_TILE, d), F32),
               pltpu.VMEM((2, SORT_ROWS, ROW_W), BF16),
               pltpu.VMEM((SEQ_TILE, SEQ_TILE), BF16),
               pltpu.VMEM((MOE_TILE, ROW_W), BF16),
               pltpu.SMEM((N_GROUPS,), I32),
               pltpu.SMEM((4 * N_GROUPS,), I32),
               pltpu.SemaphoreType.DMA((2,)),
               pltpu.SemaphoreType.DMA(())]
    return specs, shapes, scratch


def _mod_spec(d, layer):
    return pl.BlockSpec((1, 1, N_MOD, d), lambda b, s, *_: (layer, b, 0, 0))


def _gather_inputs(bsz, seq, d, prev_layer):
    nseq = seq // SEQ_TILE
    specs = [pl.BlockSpec((1, SEQ_TILE, d), lambda b, s, *_: (b, s, 0)),
             pl.BlockSpec(memory_space=pl.ANY),
             pl.BlockSpec((1, 1, SEQ_TILE), lambda b, s, *_: (b * nseq + s, 0, 0)),
             _mod_spec(d, prev_layer)]
    scratch = [pltpu.VMEM((2, SORT_ROWS, d), BF16), pltpu.SemaphoreType.DMA((2,))]
    return specs, scratch


def _const_specs(consts):
    return [pl.BlockSpec(a.shape, lambda b, s, *_, nd=a.ndim: (0,) * nd) for a in consts]


_PARAMS = pltpu.CompilerParams(dimension_semantics=("arbitrary", "arbitrary"),
                               vmem_limit_bytes=VMEM_LIMIT)


def _hybrid_call(x, mod, layer, w_in, w_out, consts):
    bsz, seq, d = x.shape
    out_specs, out_shapes, sc_scratch = _scatter_outputs(bsz, seq, d)
    hbm = pl.BlockSpec(memory_space=pl.ANY)
    return pl.pallas_call(
        _hybrid_kernel,
        grid=(bsz, seq // SEQ_TILE),
        in_specs=[pl.BlockSpec((1, SEQ_TILE, d), lambda b, s: (b, s, 0)),
                  _mod_spec(d, layer), hbm, hbm]
                 + _const_specs(consts),
        out_specs=out_specs,
        out_shape=out_shapes,
        scratch_shapes=[pltpu.VMEM((SEQ_TILE + B_HALO, D_B), F32)]
                       + _weight_scratch(*w_in.shape[1:]) + _weight_scratch(*w_out.shape[1:])
                       + sc_scratch,
        compiler_params=_PARAMS,
        name="mixer_hybrid",
    )(x, mod, w_in, w_out, *consts)


def _conformer_call(prev, mod, layer, w1, w2, consts):
    x1p, ysp, qp, segbp, segnp = prev
    bsz, seq, d = x1p.shape
    g_specs, g_scratch = _gather_inputs(bsz, seq, d, layer - 1)
    out_specs, out_shapes, sc_scratch = _scatter_outputs(bsz, seq, d)
    hbm = pl.BlockSpec(memory_space=pl.ANY)
    grid_spec = pltpu.PrefetchScalarGridSpec(
        num_scalar_prefetch=2,
        grid=(bsz, seq // SEQ_TILE),
        in_specs=g_specs + [_mod_spec(d, layer), hbm, hbm]
                 + _const_specs(consts),
        out_specs=out_specs,
        scratch_shapes=[pltpu.VMEM((SEQ_TILE + C_HALO, D_C), F32), pltpu.VMEM((SEQ_TILE, D_C), F32),
                        pltpu.VMEM((CONV_ROWS, SUBLANES * S_ROWS), BF16),
                        pltpu.VMEM((S_SLOTS, SUBLANES * S_ROWS, GLU_COLS), BF16)]
                       + g_scratch + _weight_scratch(*w1.shape[1:]) + _weight_scratch(*w2.shape[1:])
                       + sc_scratch,
    )
    return pl.pallas_call(
        _conformer_kernel, grid_spec=grid_spec, out_shape=out_shapes,
        compiler_params=_PARAMS, name="mixer_conformer",
    )(segbp, segnp, x1p, ysp, qp, mod, mod, w1, w2, *consts)


def _final_call(prev, mod, prev_layer, final_g):
    x1p, ysp, qp, segbp, segnp = prev
    bsz, seq, d = x1p.shape
    g_specs, g_scratch = _gather_inputs(bsz, seq, d, prev_layer)
    grid_spec = pltpu.PrefetchScalarGridSpec(
        num_scalar_prefetch=2,
        grid=(bsz, seq // SEQ_TILE),
        in_specs=g_specs + [pl.BlockSpec((1, d), lambda b, s, *_: (0, 0))],
        out_specs=pl.BlockSpec((1, SEQ_TILE, d), lambda b, s, *_: (b, s, 0)),
        scratch_shapes=g_scratch,
    )
    return pl.pallas_call(
        _final_kernel, grid_spec=grid_spec,
        out_shape=jax.ShapeDtypeStruct((bsz, seq, d), F32),
        compiler_params=_PARAMS, name="final_norm",
    )(segbp, segnp, x1p, ysp, qp, mod, final_g.reshape(1, d))


def _moe_kernel(items_ref, hs_ref, wg_ref, wu_ref, wd_ref, ys_ref, wg_bf, wu_bf, wd_bf):
    i = pl.program_id(0)
    n_items = pl.num_programs(0)

    @pl.when(items_ref[3 * n_items + i] != 0)
    def _():
        wg_bf[...] = wg_ref[0].astype(BF16)
        wu_bf[...] = wu_ref[0].astype(BF16)
        wd_bf[...] = wd_ref[0].astype(BF16)

    @pl.when(items_ref[2 * n_items + i] != 0)
    def _():
        rows = hs_ref[...]
        h = rows[:, :D_MODEL]
        m = rows[:, D_MODEL:].astype(F32)
        ne = EXPERTS_PER_GROUP
        y = None
        for j in range(ne):
            gate = m[:, j:j + 1] + (m[:, ne + j:ne + j + 1] + m[:, 2 * ne + j:2 * ne + j + 1])
            a = jnp.dot(h, wg_bf[j], preferred_element_type=F32)
            a = a * jax.nn.sigmoid(a) * jnp.dot(h, wu_bf[j], preferred_element_type=F32)
            yj = jnp.dot((a * gate).astype(BF16), wd_bf[j], preferred_element_type=F32)
            y = yj if y is None else y + yj
        ys_ref[...] = y.astype(BF16)


def _moe_call(hs, items, layer, wg, wu, wd):
    d = D_MODEL
    ne = EXPERTS_PER_GROUP
    n_items = items.shape[0] // 4
    w_idx = lambda i, items: (layer * N_GROUPS + items[n_items + i], 0, 0, 0)
    row_idx = lambda i, items: (items[i], 0)
    grid_spec = pltpu.PrefetchScalarGridSpec(
        num_scalar_prefetch=1,
        grid=(n_items,),
        in_specs=[pl.BlockSpec((MOE_TILE, ROW_W), row_idx),
                  pl.BlockSpec((1, ne, d, D_EXPERT), w_idx),
                  pl.BlockSpec((1, ne, d, D_EXPERT), w_idx),
                  pl.BlockSpec((1, ne, D_EXPERT, d), w_idx)],
        out_specs=pl.BlockSpec((MOE_TILE, d), row_idx),
        scratch_shapes=[pltpu.VMEM((ne, d, D_EXPERT), BF16), pltpu.VMEM((ne, d, D_EXPERT), BF16),
                        pltpu.VMEM((ne, D_EXPERT, d), BF16)],
    )
    return pl.pallas_call(
        _moe_kernel, grid_spec=grid_spec,
        out_shape=jax.ShapeDtypeStruct((hs.shape[0], d), BF16),
        compiler_params=pltpu.CompilerParams(dimension_semantics=("arbitrary",),
                                             vmem_limit_bytes=VMEM_LIMIT),
        name="moe",
    )(items, hs, wg, wu, wd)


def kernel(x, c, ada_w, ada_b, norm_mix_g, norm_ffn_g, hyb_w_in, sgu_ln_g, sgu_ln_b, sgu_w, sgu_b,
           sconv_w, hyb_w_out, conf_w1, conf_b1, conf_dw_w, conf_dw_b, conf_ln_g, conf_ln_b,
           conf_w2, conf_b2, router_w, router_bias, moe_w_gate, moe_w_up, moe_w_down, final_norm_g):
    depth = ada_w.shape[0]
    bsz, seq, d = x.shape
    t = bsz * seq
    assert depth == 2 and seq % SEQ_TILE == 0
    mod = _ada_mod(c, ada_w, ada_b).reshape(depth, bsz, N_MOD, d)

    perm = jnp.arange(N_EXPERTS).reshape(N_GROUPS, EXPERTS_PER_GROUP).T.reshape(-1)
    rw_t = router_w.T[perm]
    rb_t = router_bias[perm].reshape(N_EXPERTS, 1)

    ne = EXPERTS_PER_GROUP
    wg = moe_w_gate.reshape(depth * N_GROUPS, ne, d, D_EXPERT)
    wu = moe_w_up.reshape(depth * N_GROUPS, ne, d, D_EXPERT)
    wd = moe_w_down.reshape(depth * N_GROUPS, ne, D_EXPERT, d)

    row = lambda a: a.reshape(1, -1)
    consts0 = [row(norm_mix_g[0]), row(norm_ffn_g[0]), row(sgu_ln_g[0]), row(sgu_ln_b[0]), sgu_w[0],
               sgu_b[0].reshape(A_HEADS, CHUNK, 1), sconv_w[0], rw_t, rb_t]
    x1, hs, q, segb, segn, items = _hybrid_call(x, mod, 0, hyb_w_in, hyb_w_out, consts0)
    ys = _moe_call(hs, items, 0, wg, wu, wd)

    consts1 = [row(norm_mix_g[1]), row(norm_ffn_g[1]), row(conf_b1[0]), conf_dw_w[0],
               row(conf_dw_b[0]), row(conf_ln_g[0]), row(conf_ln_b[0]), row(conf_b2[0]), rw_t, rb_t]
    x1, hs, q1, segb1, segn1, items = _conformer_call((x1, ys, q, segb, segn), mod, 1,
                                                      conf_w1, conf_w2, consts1)
    ys = _moe_call(hs, items, 1, wg, wu, wd)
    return _final_call((x1, ys, q1, segb1, segn1), mod, 1, final_norm_g)
```

```python
import functools

import jax
import jax.numpy as jnp
from jax import lax
from jax.experimental import pallas as pl
from jax.experimental.pallas import tpu as pltpu

D_MODEL = 1024
CHUNK = 128
A_HEADS = 4
A_HEAD_DIM = 128
D_A = A_HEADS * A_HEAD_DIM
D_B = 512
B_CONV = 3
D_IN_HYB = 2 * D_A + 3 * D_B
D_C = D_MODEL
C_CONV = 31
N_EXPERTS = 16
N_GROUPS = 4
EXPERTS_PER_GROUP = N_EXPERTS // N_GROUPS
D_EXPERT = 256
N_MOD = 6
EPS = 1e-6

SUBLANES = 8
LANES = 128
ROW_ALIGN = 16

SEQ_TILE = 512
MOE_TILE = 512
ITEM_FIELDS = 5
MOD_TILE = 2048
W_STAGE_ROWS = 128
C_HALO = 40
B_HALO = 8
CONV_ROWS = 128
GLU_COLS = 256
GLU_AHEAD = 4
S_LEAD = ROW_ALIGN
S_ROWS = CONV_ROWS + S_LEAD
S_SLOTS = 8
SORT_ROWS = SEQ_TILE + N_GROUPS * ROW_ALIGN
ROW_W = D_MODEL + LANES
VMEM_LIMIT = 56 * 1024 * 1024

F32 = jnp.float32
BF16 = jnp.bfloat16
I32 = jnp.int32


def _round_up(n, m):
    return -(-n // m) * m


def _moe_steps(t):
    return -(-(t + (t // SEQ_TILE) * N_GROUPS * (ROW_ALIGN - 1)) // MOE_TILE) + N_GROUPS


def _region_rows(t):
    return _round_up(t + (t // SEQ_TILE) * ROW_ALIGN, MOE_TILE)


def _split_bf16(a):
    hi = a.astype(BF16)
    lo = (a - hi.astype(F32)).astype(BF16)
    return hi, lo


def _dot3(a, b):
    ah, al = _split_bf16(a)
    bh, bl = _split_bf16(b)
    d = functools.partial(jnp.dot, preferred_element_type=F32)
    return d(ah, bh) + (d(ah, bl) + d(al, bh))


def _rms(x, g):
    return x * lax.rsqrt(jnp.mean(x * x, axis=-1, keepdims=True) + EPS) * g


def _rms_mod(x, g, shift, scale):
    return x * lax.rsqrt(jnp.mean(x * x, axis=-1, keepdims=True) + EPS) * (g * (1.0 + scale)) + shift


def _gelu(x):
    return 0.5 * x * (1.0 + lax.erf(x * (2.0 ** -0.5)))


def _layer_norm(x, g, b):
    mu = jnp.mean(x, axis=-1, keepdims=True)
    xc = x - mu
    var = jnp.mean(xc * xc, axis=-1, keepdims=True)
    return xc * lax.rsqrt(var + EPS) * g + b


def _mod_kernel(c_ref, w_ref, b_ref, o_ref):
    c = c_ref[...]
    c_act = c * jax.nn.sigmoid(c)
    o_ref[0] = _dot3(c_act, w_ref[0]) + b_ref[0]


def _ada_mod(c, ada_w, ada_b):
    depth, d, n = ada_w.shape
    bsz = c.shape[0]
    return pl.pallas_call(
        _mod_kernel,
        grid=(depth, n // MOD_TILE),
        in_specs=[
            pl.BlockSpec((bsz, d), lambda l, j: (0, 0)),
            pl.BlockSpec((1, d, MOD_TILE), lambda l, j: (l, 0, j)),
            pl.BlockSpec((1, 1, MOD_TILE), lambda l, j: (l, 0, j)),
        ],
        out_specs=pl.BlockSpec((1, bsz, MOD_TILE), lambda l, j: (l, 0, j)),
        out_shape=jax.ShapeDtypeStruct((depth, bsz, n), F32),
        compiler_params=pltpu.CompilerParams(
            dimension_semantics=("arbitrary", "arbitrary"), vmem_limit_bytes=VMEM_LIMIT),
        name="ada_mod",
    )(c, ada_w, ada_b.reshape(depth, 1, n))


def _route(h2, rw_ref, rb_ref):
    wh, wl = _split_bf16(rw_ref[...])
    lane = lax.broadcasted_iota(I32, wh.shape, 1)
    hh, hl = _split_bf16(h2)
    acc = jnp.dot(hh, jnp.where(lane < N_EXPERTS, wh, wl), preferred_element_type=F32)
    low = jnp.dot(hl, wh, preferred_element_type=F32)
    lane = lax.broadcasted_iota(I32, acc.shape, 1)
    acc_t = (acc + jnp.where(lane < N_EXPERTS, low, 0.0)).T
    logits = acc_t[:N_EXPERTS] + acc_t[N_EXPERTS:2 * N_EXPERTS]
    scores = jax.nn.sigmoid(logits)
    biased = scores + rb_ref[...]
    ng, ne = N_GROUPS, EXPERTS_PER_GROUP
    b = [biased[j * ng:(j + 1) * ng] for j in range(ne)]
    s = [scores[j * ng:(j + 1) * ng] for j in range(ne)]
    gscore = None
    for i in range(ne):
        for j in range(i + 1, ne):
            pair = b[i] + b[j]
            gscore = pair if gscore is None else jnp.maximum(gscore, pair)
    gid = lax.broadcasted_iota(I32, gscore.shape, 0)
    gmax = jnp.max(gscore, axis=0, keepdims=True)
    best = jnp.min(jnp.where(gscore == gmax, gid, ng), axis=0, keepdims=True)
    onb = gid == best
    cb = [jnp.sum(jnp.where(onb, b[j], 0.0), axis=0, keepdims=True) for j in range(ne)]
    cs = [jnp.sum(jnp.where(onb, s[j], 0.0), axis=0, keepdims=True) for j in range(ne)]
    sel = []
    for j in range(ne):
        rank = jnp.zeros_like(best)
        for i in range(ne):
            if i == j:
                continue
            ahead = (cb[i] >= cb[j]) if i < j else (cb[i] > cb[j])
            rank = rank + ahead.astype(I32)
        sel.append(rank < 2)
    wsum = sum(jnp.where(sel[j], cs[j], 0.0) for j in range(ne))
    gates = jnp.concatenate([jnp.where(sel[j], cs[j], 0.0) / wsum for j in range(ne)], axis=0)
    return gates, jnp.where(onb, 1.0, 0.0)


def _sort_matrix(q):
    rows = lax.broadcasted_iota(I32, (SORT_ROWS, q.shape[1]), 0)
    return jnp.where(rows == q, 1.0, 0.0).astype(BF16)


def _run_copies(op, src_of, dst_of, sem, nrows, bases, region_rows):
    if op == "wait":
        total = pl.multiple_of(sum(nrows[1:], nrows[0]), ROW_ALIGN)

        @pl.when(total > 0)
        def _():
            pltpu.make_async_copy(src_of(0, 0, total), dst_of(0, 0, total), sem).wait()
        return
    tile_off = 0
    for g in range(N_GROUPS):
        n = pl.multiple_of(nrows[g], ROW_ALIGN)
        a = pl.multiple_of(tile_off, ROW_ALIGN)
        b = pl.multiple_of(g * region_rows + bases[g], ROW_ALIGN)

        @pl.when(n > 0)
        def _(a=a, b=b, n=n):
            pltpu.make_async_copy(src_of(a, b, n), dst_of(a, b, n), sem).start()

        tile_off = tile_off + nrows[g]


class _ScatterRefs:
    def __init__(self, gffn_ref, rw_ref, rb_ref, hs_ref, q_ref, segb_ref, segn_ref, items_ref,
                 h2buf, rowbuf, tri, zbuf, fill_s, pend_s, ssem, zsem):
        self.__dict__.update(locals())
        self.region_rows = hs_ref.shape[0] // N_GROUPS
        self.dst_of = lambda a, b, size: hs_ref.at[pl.ds(b, size)]


def _sc_init(sc, tile):
    @pl.when(tile == 0)
    def _():
        for g in range(N_GROUPS):
            sc.fill_s[g] = 0
        for i in range(sc.pend_s.shape[0]):
            sc.pend_s[i] = 0
        ts = sc.tri.shape[0]
        k = lax.broadcasted_iota(I32, (ts, ts), 0)
        t = lax.broadcasted_iota(I32, (ts, ts), 1)
        sc.tri[...] = jnp.where(k <= t, 1.0, 0.0).astype(BF16)
        sc.h2buf[...] = jnp.zeros_like(sc.h2buf)


def _sc_pending(sc, op, slot):
    rec = slot * (2 * N_GROUPS)
    src_of = lambda a, b, size: sc.rowbuf.at[slot, pl.ds(a, size)]
    _run_copies(op, src_of, sc.dst_of, sc.ssem.at[slot], [sc.pend_s[rec + g] for g in range(N_GROUPS)],
                [sc.pend_s[rec + N_GROUPS + g] for g in range(N_GROUPS)], sc.region_rows)


def _prenorm(x1, mod, gffn_ref):
    sh2, sc2 = mod[3:4], mod[4:5]
    return _rms_mod(x1, gffn_ref[...], sh2, sc2)


def _sc_route(sc, st):
    st["h2"] = sc.h2buf[...]
    st["gates"], st["onehot"] = _route(st["h2"], sc.rw_ref, sc.rb_ref)


def _sc_rank(sc, st, active):
    onehot = st["onehot"]
    ts = onehot.shape[1]
    oh8 = jnp.concatenate([onehot, jnp.zeros((SUBLANES - N_GROUPS, ts), F32)], axis=0).astype(BF16)
    csum = jnp.dot(oh8, sc.tri[...], preferred_element_type=F32)[:N_GROUPS]
    counts = [csum[g, ts - 1].astype(I32) * active for g in range(N_GROUPS)]
    st["nrows"] = [(c + (ROW_ALIGN - 1)) & ~(ROW_ALIGN - 1) for c in counts]
    q = jnp.zeros((1, ts), F32)
    tile_off = 0
    for g in range(N_GROUPS):
        q = q + onehot[g:g + 1] * (csum[g:g + 1] - 1.0 + tile_off.astype(F32)
                                   if g else csum[g:g + 1] - 1.0)
        tile_off = tile_off + st["nrows"][g]
    st["q"] = q.astype(I32)


def _sc_sort(sc, st, slot):
    gates = st["gates"]
    ts = gates.shape[1]
    hi = gates.astype(BF16).astype(F32)
    mid = (gates - hi).astype(BF16).astype(F32)
    lo = (gates - hi - mid).astype(BF16).astype(F32)
    meta_t = jnp.concatenate(
        [hi, mid, lo, jnp.zeros((LANES - 3 * EXPERTS_PER_GROUP, ts), F32)], axis=0)
    rows = jnp.concatenate([st["h2"].astype(BF16), meta_t.T.astype(BF16)], axis=1)
    sorted_rows = jnp.dot(_sort_matrix(st["q"]), rows, preferred_element_type=F32)
    sc.rowbuf[slot] = sorted_rows.astype(BF16)


def _sc_start(sc, st, t_idx, slot):
    sc.q_ref[t_idx] = st["q"]
    nrows = st["nrows"]
    rec = slot * (2 * N_GROUPS)
    for g in range(N_GROUPS):
        base = sc.fill_s[g]
        sc.pend_s[rec + g] = nrows[g]
        sc.pend_s[rec + N_GROUPS + g] = base
        sc.segn_ref[t_idx * N_GROUPS + g] = nrows[g]
        sc.segb_ref[t_idx * N_GROUPS + g] = base
        sc.fill_s[g] = base + nrows[g]
    _sc_pending(sc, "start", slot)


def _write_moe_items(sc):
    n_items = sc.items_ref.shape[0] // ITEM_FIELDS
    per_region = sc.region_rows // MOE_TILE
    ends = []
    total = 0
    for g in range(N_GROUPS):
        total = total + (sc.fill_s[g] + (MOE_TILE - 1)) // MOE_TILE
        ends.append(total)
    nxt = [None] * N_GROUPS
    follow = -1
    for g in reversed(range(N_GROUPS)):
        nxt[g] = follow
        follow = jnp.where(ends[g] > (ends[g - 1] if g else 0), g, follow)
    last = jnp.maximum(total - 1, 0)
    prev_grp = None
    for i in range(n_items):
        ic = jnp.minimum(i, last)
        grp = sum((ic >= e).astype(I32) for e in ends[:-1])
        start = jnp.where(grp == 0, 0, jnp.where(grp == 1, ends[0], jnp.where(grp == 2, ends[1], ends[2])))
        fresh = jnp.ones((), I32) if prev_grp is None else (grp != prev_grp).astype(I32)
        sc.items_ref[i] = grp * per_region + ic - start
        sc.items_ref[n_items + i] = grp
        sc.items_ref[2 * n_items + i] = (i < total).astype(I32)
        sc.items_ref[3 * n_items + i] = fresh
        sc.items_ref[4 * n_items + i] = jnp.where(
            grp == 0, nxt[0], jnp.where(grp == 1, nxt[1], jnp.where(grp == 2, nxt[2], nxt[3])))
        prev_grp = grp


def _sc_finish(sc):
    _sc_pending(sc, "wait", 0)
    _sc_pending(sc, "wait", 1)
    sc.zbuf[...] = jnp.zeros_like(sc.zbuf)
    for op in ("start", "wait"):
        for g in range(N_GROUPS):
            fill = sc.fill_s[g]
            n = pl.multiple_of((-fill) & (MOE_TILE - 1), ROW_ALIGN)
            b = pl.multiple_of(g * sc.region_rows + fill, ROW_ALIGN)

            @pl.when(n > 0)
            def _(n=n, b=b, op=op):
                cp = pltpu.make_async_copy(sc.zbuf.at[pl.ds(0, n)], sc.hs_ref.at[pl.ds(b, n)], sc.zsem)
                getattr(cp, op)()
    _write_moe_items(sc)


def _gather_prev_ffn(tile, n_tiles, segb_ref, segn_ref, ys_ref, q_ref, ybuf, gsem):
    region_rows = ys_ref.shape[0] // N_GROUPS

    def copies(op, t, slot):
        nrows = [segn_ref[t * N_GROUPS + g] for g in range(N_GROUPS)]
        bases = [segb_ref[t * N_GROUPS + g] for g in range(N_GROUPS)]
        src_of = lambda a, b, size: ys_ref.at[pl.ds(b, size)]
        dst_of = lambda a, b, size: ybuf.at[slot, pl.ds(a, size)]
        _run_copies(op, src_of, dst_of, gsem.at[slot], nrows, bases, region_rows)

    def issue(t, slot):
        ybuf[slot, SEQ_TILE:SORT_ROWS, :] = jnp.zeros((SORT_ROWS - SEQ_TILE, D_MODEL), BF16)
        copies("start", t, slot)

    @pl.when(tile == 0)
    def _():
        issue(tile, 0)

    @pl.when(tile + 1 < n_tiles)
    def _():
        issue(tile + 1, (tile + 1) % 2)

    slot = tile % 2
    copies("wait", tile, slot)
    p = _sort_matrix(q_ref[0])
    return lax.dot_general(p, ybuf[slot], (((0,), (0,)), ((), ())), preferred_element_type=F32)


def _load_weight_bf16(w_hbm, w_vmem, stage, sem):
    rows = stage.shape[1]
    n = w_vmem.shape[0] // rows
    copy = lambda c: pltpu.make_async_copy(w_hbm.at[pl.ds(c * rows, rows)], stage.at[c % 2],
                                           sem.at[c % 2])
    copy(0).start()
    for c in range(n):
        if c + 1 < n:
            copy(c + 1).start()
        copy(c).wait()
        w_vmem[c * rows:(c + 1) * rows, :] = stage[c % 2].astype(BF16)


def _weight_scratch(k, n):
    return [pltpu.VMEM((k, n), BF16), pltpu.VMEM((2, W_STAGE_ROWS, n), F32),
            pltpu.SemaphoreType.DMA((2,))]


def _hybrid_kernel(x_ref, mod_ref, win_hbm, wout_hbm, gmix_ref, gffn_ref, lng_ref, lnb_ref, sw_ref,
                   sb_ref, cw_ref, rw_ref, rb_ref,
                   x1_ref, hs_ref, q_ref, segb_ref, segn_ref, items_ref,
                   pbuf, win_ref, win_stage, win_sem, wout_ref, wout_stage, wout_sem,
                   h2buf, rowbuf, tri, zbuf, fill_s, pend_s, ssem, zsem):
    ts = x_ref.shape[1]
    tile = pl.program_id(0) * pl.num_programs(1) + pl.program_id(1)
    n_tiles = pl.num_programs(0) * pl.num_programs(1)
    sc = _ScatterRefs(gffn_ref, rw_ref, rb_ref, hs_ref, q_ref, segb_ref, segn_ref, items_ref,
                      h2buf, rowbuf, tri, zbuf, fill_s, pend_s, ssem, zsem)

    @pl.when(tile == 0)
    def _():
        _load_weight_bf16(win_hbm.at[0], win_ref, win_stage, win_sem)
        _load_weight_bf16(wout_hbm.at[0], wout_ref, wout_stage, wout_sem)

    @pl.when(pl.program_id(1) == 0)
    def _():
        pbuf[0:B_HALO, :] = jnp.zeros((B_HALO, D_B), F32)

    _sc_init(sc, tile)
    slot = tile % 2
    _sc_pending(sc, "wait", slot)

    prev = {}
    active = (tile > 0).astype(I32)

    _sc_route(sc, prev)
    _sc_rank(sc, prev, active)
    _sc_sort(sc, prev, slot)

    x = x_ref[0]
    mod = mod_ref[0, 0]
    sh1, sc1, g1 = mod[0:1], mod[1:2], mod[2:3]
    h = _rms_mod(x, gmix_ref[...], sh1, sc1)
    hb16 = h.astype(BF16)
    zcol = lambda j: jnp.dot(hb16, win_ref[:, j * D_A:(j + 1) * D_A], preferred_element_type=F32)

    v = _layer_norm(_gelu(zcol(1)), lng_ref[...], lnb_ref[...]).astype(BF16)
    u = _gelu(zcol(0))
    tril = (lax.broadcasted_iota(I32, (CHUNK, CHUNK), 0)
            >= lax.broadcasted_iota(I32, (CHUNK, CHUNK), 1))
    chunks = []
    for n in range(ts // CHUNK):
        heads = []
        for hd in range(A_HEADS):
            w = jnp.where(tril, sw_ref[hd], 0.0).astype(BF16)
            vb = v[n * CHUNK:(n + 1) * CHUNK, hd * A_HEAD_DIM:(hd + 1) * A_HEAD_DIM]
            heads.append(jnp.dot(w, vb, preferred_element_type=F32) + sb_ref[hd])
        chunks.append(jnp.concatenate(heads, axis=1))
    y_a = u * jnp.concatenate(chunks, axis=0)

    p = zcol(3) * zcol(4)
    pbuf[B_HALO:B_HALO + ts, :] = p
    cw = cw_ref[...]
    conv = (cw[0:1] * pbuf[B_HALO - 2:B_HALO - 2 + ts, :]
            + cw[1:2] * pbuf[B_HALO - 1:B_HALO - 1 + ts, :]
            + cw[2:3] * p)
    pbuf[0:B_HALO, :] = p[ts - B_HALO:ts, :]
    y_b = zcol(2) * conv

    ycat = jnp.concatenate([y_a, y_b], axis=1).astype(BF16)
    y = jnp.dot(ycat, wout_ref[...], preferred_element_type=F32)
    x1 = x + g1 * y
    x1_ref[0] = x1
    h2buf[...] = _prenorm(x1, mod, gffn_ref)
    _sc_start(sc, prev, jnp.maximum(tile - 1, 0), slot)

    @pl.when(tile == n_tiles - 1)
    def _():
        _sc_pending(sc, "wait", 1 - slot)
        last = {}
        _sc_route(sc, last)
        _sc_rank(sc, last, 1)
        _sc_sort(sc, last, 1 - slot)
        _sc_start(sc, last, tile, 1 - slot)
        _sc_finish(sc)


def _band_matrix():
    shape = (CONV_ROWS, SUBLANES * S_ROWS)
    t = lax.broadcasted_iota(I32, shape, 0)
    c = lax.broadcasted_iota(I32, shape, 1)
    hit = c == t + S_LEAD
    for r in range(1, SUBLANES):
        hit = hit | (c == t + S_LEAD + r * (S_ROWS - 1))
    return jnp.where(hit, 1.0, 0.0).astype(BF16)


def _causal_dw_conv(cbuf, w_ref, obuf, band, sbuf, row0, nrows, col0, ncols, first_slot):
    n_a = -(-C_CONV // SUBLANES)
    cols = slice(col0, col0 + ncols)
    for i, t0 in enumerate(range(row0, row0 + nrows, CONV_ROWS)):
        stack = sbuf.at[(first_slot + i) % sbuf.shape[0]]
        for r in range(SUBLANES):
            part = None
            for a in range(n_a):
                k = C_CONV - 1 - SUBLANES * a - r
                if k < 0:
                    continue
                r0 = t0 + C_HALO - S_LEAD - SUBLANES * a
                term = w_ref[k:k + 1, cols] * cbuf[r0:r0 + S_ROWS, cols]
                part = term if part is None else part + term
            stack[r * S_ROWS:(r + 1) * S_ROWS, :] = part.astype(BF16)
        obuf[t0:t0 + CONV_ROWS, cols] = jnp.dot(band[...], stack[...], preferred_element_type=F32)


def _conformer_kernel(psegb_ref, psegn_ref,
                      xp_ref, ys_ref, pq_ref, pmod_ref, mod_ref, w1_hbm, w2_hbm, gmix_ref, gffn_ref,
                      b1_ref, dww_ref, dwb_ref, lng_ref, lnb_ref, b2_ref, rw_ref, rb_ref,
                      x1_ref, hs_ref, q_ref, segb_ref, segn_ref, items_ref,
                      cbuf, obuf, band, sbuf, ybuf, gsem,
                      w1_ref, w1_stage, w1_sem, w2_ref, w2_stage, w2_sem,
                      h2buf, rowbuf, tri, zbuf, fill_s, pend_s, ssem, zsem):
    ts = xp_ref.shape[1]
    tile = pl.program_id(0) * pl.num_programs(1) + pl.program_id(1)
    n_tiles = pl.num_programs(0) * pl.num_programs(1)
    sc = _ScatterRefs(gffn_ref, rw_ref, rb_ref, hs_ref, q_ref, segb_ref, segn_ref, items_ref,
                      h2buf, rowbuf, tri, zbuf, fill_s, pend_s, ssem, zsem)

    @pl.when(tile == 0)
    def _():
        _load_weight_bf16(w1_hbm.at[0], w1_ref, w1_stage, w1_sem)
        _load_weight_bf16(w2_hbm.at[0], w2_ref, w2_stage, w2_sem)
        band[...] = _band_matrix()

    @pl.when(pl.program_id(1) == 0)
    def _():
        cbuf[0:C_HALO, :] = jnp.zeros((C_HALO, D_C), F32)

    _sc_init(sc, tile)
    slot = tile % 2
    _sc_pending(sc, "wait", slot)
    y_prev = _gather_prev_ffn(tile, n_tiles, psegb_ref, psegn_ref, ys_ref, pq_ref, ybuf, gsem)

    prev = {}
    active = (tile > 0).astype(I32)
    deferred = [lambda: _sc_route(sc, prev), lambda: _sc_rank(sc, prev, active),
                lambda: _sc_sort(sc, prev, slot)]

    x = xp_ref[0] + pmod_ref[0, 0][5:6] * y_prev
    mod = mod_ref[0, 0]
    sh1, sc1, g1 = mod[0:1], mod[1:2], mod[2:3]
    h = _rms_mod(x, gmix_ref[...], sh1, sc1)
    hb = h.astype(BF16)
    def glu(c0):
        ca, cg = slice(c0, c0 + GLU_COLS), slice(D_C + c0, D_C + c0 + GLU_COLS)
        za = jnp.dot(hb, w1_ref[:, ca], preferred_element_type=F32) + b1_ref[:, ca]
        zg = jnp.dot(hb, w1_ref[:, cg], preferred_element_type=F32) + b1_ref[:, cg]
        cbuf[C_HALO:C_HALO + ts, ca] = za * jax.nn.sigmoid(zg)

    for c0 in range(0, min(GLU_AHEAD * GLU_COLS, D_C), GLU_COLS):
        glu(c0)
    for k, c0 in enumerate(range(0, D_C, GLU_COLS)):
        if c0 + GLU_AHEAD * GLU_COLS < D_C:
            glu(c0 + GLU_AHEAD * GLU_COLS)
        if k == 0:
            for piece in deferred:
                piece()
        _causal_dw_conv(cbuf, dww_ref, obuf, band, sbuf, 0, ts, c0, GLU_COLS,
                        k * (ts // CONV_ROWS))
    cbuf[0:C_HALO, :] = cbuf[ts:ts + C_HALO, :]

    zc = _layer_norm(obuf[...] + dwb_ref[...], lng_ref[...], lnb_ref[...])
    zc = zc * jax.nn.sigmoid(zc)
    y = jnp.dot(zc.astype(BF16), w2_ref[...], preferred_element_type=F32) + b2_ref[...]
    x1 = x + g1 * y
    x1_ref[0] = x1
    h2buf[...] = _prenorm(x1, mod, gffn_ref)
    _sc_start(sc, prev, jnp.maximum(tile - 1, 0), slot)

    @pl.when(tile == n_tiles - 1)
    def _():
        _sc_pending(sc, "wait", 1 - slot)
        last = {}
        _sc_route(sc, last)
        _sc_rank(sc, last, 1)
        _sc_sort(sc, last, 1 - slot)
        _sc_start(sc, last, tile, 1 - slot)
        _sc_finish(sc)


def _final_kernel(psegb_ref, psegn_ref, xp_ref, ys_ref, pq_ref, pmod_ref, fg_ref, o_ref, ybuf, gsem):
    tile = pl.program_id(0) * pl.num_programs(1) + pl.program_id(1)
    n_tiles = pl.num_programs(0) * pl.num_programs(1)
    y_prev = _gather_prev_ffn(tile, n_tiles, psegb_ref, psegn_ref, ys_ref, pq_ref, ybuf, gsem)
    x = xp_ref[0] + pmod_ref[0, 0][5:6] * y_prev
    o_ref[0] = _rms(x, fg_ref[...])


def _scatter_outputs(bsz, seq, d):
    t = bsz * seq
    n_tiles = t // SEQ_TILE
    smem = pl.BlockSpec(memory_space=pltpu.SMEM)
    specs = [pl.BlockSpec((1, SEQ_TILE, d), lambda b, s, *_: (b, s, 0)),
             pl.BlockSpec(memory_space=pl.ANY),
             pl.BlockSpec((n_tiles, 1, SEQ_TILE), lambda b, s, *_: (0, 0, 0)),
             smem, smem, smem]
    shapes = [jax.ShapeDtypeStruct((bsz, seq, d), F32),
              jax.ShapeDtypeStruct((N_GROUPS * _region_rows(t), ROW_W), BF16),
              jax.ShapeDtypeStruct((n_tiles, 1, SEQ_TILE), I32),
              jax.ShapeDtypeStruct((n_tiles * N_GROUPS,), I32),
              jax.ShapeDtypeStruct((n_tiles * N_GROUPS,), I32),
              jax.ShapeDtypeStruct((ITEM_FIELDS * _moe_steps(t),), I32)]
    scratch = [pltpu.VMEM((SEQ_TILE, d), F32),
               pltpu.VMEM((2, SORT_ROWS, ROW_W), BF16),
               pltpu.VMEM((SEQ_TILE, SEQ_TILE), BF16),
               pltpu.VMEM((MOE_TILE, ROW_W), BF16),
               pltpu.SMEM((N_GROUPS,), I32),
               pltpu.SMEM((4 * N_GROUPS,), I32),
               pltpu.SemaphoreType.DMA((2,)),
               pltpu.SemaphoreType.DMA(())]
    return specs, shapes, scratch


def _mod_spec(d, layer):
    return pl.BlockSpec((1, 1, N_MOD, d), lambda b, s, *_: (layer, b, 0, 0))


def _gather_inputs(bsz, seq, d, prev_layer):
    nseq = seq // SEQ_TILE
    specs = [pl.BlockSpec((1, SEQ_TILE, d), lambda b, s, *_: (b, s, 0)),
             pl.BlockSpec(memory_space=pl.ANY),
             pl.BlockSpec((1, 1, SEQ_TILE), lambda b, s, *_: (b * nseq + s, 0, 0)),
             _mod_spec(d, prev_layer)]
    scratch = [pltpu.VMEM((2, SORT_ROWS, d), BF16), pltpu.SemaphoreType.DMA((2,))]
    return specs, scratch


def _const_specs(consts):
    return [pl.BlockSpec(a.shape, lambda b, s, *_, nd=a.ndim: (0,) * nd) for a in consts]


_PARAMS = pltpu.CompilerParams(dimension_semantics=("arbitrary", "arbitrary"),
                               vmem_limit_bytes=VMEM_LIMIT)


def _hybrid_call(x, mod, layer, w_in, w_out, consts):
    bsz, seq, d = x.shape
    out_specs, out_shapes, sc_scratch = _scatter_outputs(bsz, seq, d)
    hbm = pl.BlockSpec(memory_space=pl.ANY)
    return pl.pallas_call(
        _hybrid_kernel,
        grid=(bsz, seq // SEQ_TILE),
        in_specs=[pl.BlockSpec((1, SEQ_TILE, d), lambda b, s: (b, s, 0)),
                  _mod_spec(d, layer), hbm, hbm]
                 + _const_specs(consts),
        out_specs=out_specs,
        out_shape=out_shapes,
        scratch_shapes=[pltpu.VMEM((SEQ_TILE + B_HALO, D_B), F32)]
                       + _weight_scratch(*w_in.shape[1:]) + _weight_scratch(*w_out.shape[1:])
                       + sc_scratch,
        compiler_params=_PARAMS,
        name="mixer_hybrid",
    )(x, mod, w_in, w_out, *consts)


def _conformer_call(prev, mod, layer, w1, w2, consts):
    x1p, ysp, qp, segbp, segnp = prev
    bsz, seq, d = x1p.shape
    g_specs, g_scratch = _gather_inputs(bsz, seq, d, layer - 1)
    out_specs, out_shapes, sc_scratch = _scatter_outputs(bsz, seq, d)
    hbm = pl.BlockSpec(memory_space=pl.ANY)
    grid_spec = pltpu.PrefetchScalarGridSpec(
        num_scalar_prefetch=2,
        grid=(bsz, seq // SEQ_TILE),
        in_specs=g_specs + [_mod_spec(d, layer), hbm, hbm]
                 + _const_specs(consts),
        out_specs=out_specs,
        scratch_shapes=[pltpu.VMEM((SEQ_TILE + C_HALO, D_C), F32), pltpu.VMEM((SEQ_TILE, D_C), F32),
                        pltpu.VMEM((CONV_ROWS, SUBLANES * S_ROWS), BF16),
                        pltpu.VMEM((S_SLOTS, SUBLANES * S_ROWS, GLU_COLS), BF16)]
                       + g_scratch + _weight_scratch(*w1.shape[1:]) + _weight_scratch(*w2.shape[1:])
                       + sc_scratch,
    )
    return pl.pallas_call(
        _conformer_kernel, grid_spec=grid_spec, out_shape=out_shapes,
        compiler_params=_PARAMS, name="mixer_conformer",
    )(segbp, segnp, x1p, ysp, qp, mod, mod, w1, w2, *consts)


def _final_call(prev, mod, prev_layer, final_g):
    x1p, ysp, qp, segbp, segnp = prev
    bsz, seq, d = x1p.shape
    g_specs, g_scratch = _gather_inputs(bsz, seq, d, prev_layer)
    grid_spec = pltpu.PrefetchScalarGridSpec(
        num_scalar_prefetch=2,
        grid=(bsz, seq // SEQ_TILE),
        in_specs=g_specs + [pl.BlockSpec((1, d), lambda b, s, *_: (0, 0))],
        out_specs=pl.BlockSpec((1, SEQ_TILE, d), lambda b, s, *_: (b, s, 0)),
        scratch_shapes=g_scratch,
    )
    return pl.pallas_call(
        _final_kernel, grid_spec=grid_spec,
        out_shape=jax.ShapeDtypeStruct((bsz, seq, d), F32),
        compiler_params=_PARAMS, name="final_norm",
    )(segbp, segnp, x1p, ysp, qp, mod, final_g.reshape(1, d))


def _moe_kernel(items_ref, hs_ref, wg_hbm, wu_hbm, wd_hbm, ys_ref,
                wg_st, wu_st, wd_st, wsem, wg_bf, wu_bf, wd_bf, slot_s, *, layer):
    i = pl.program_id(0)
    n_items = pl.num_programs(0)
    grp = items_ref[n_items + i]

    def fetch(g, slot):
        src = layer * N_GROUPS + g
        return [pltpu.make_async_copy(w.at[src], st.at[slot], wsem.at[slot, k])
                for k, (w, st) in enumerate(((wg_hbm, wg_st), (wu_hbm, wu_st), (wd_hbm, wd_st)))]

    @pl.when(i == 0)
    def _():
        slot_s[0] = 0
        for cp in fetch(grp, 0):
            cp.start()

    @pl.when(items_ref[3 * n_items + i] != 0)
    def _():
        slot = slot_s[0]
        for cp in fetch(grp, slot):
            cp.wait()
        wg_bf[...] = wg_st[slot].astype(BF16)
        wu_bf[...] = wu_st[slot].astype(BF16)
        wd_bf[...] = wd_st[slot].astype(BF16)
        nxt = items_ref[4 * n_items + i]

        @pl.when(nxt >= 0)
        def _():
            for cp in fetch(nxt, 1 - slot):
                cp.start()

        slot_s[0] = 1 - slot

    @pl.when(items_ref[2 * n_items + i] != 0)
    def _():
        rows = hs_ref[...]
        h = rows[:, :D_MODEL]
        m = rows[:, D_MODEL:].astype(F32)
        ne = EXPERTS_PER_GROUP
        y = None
        for j in range(ne):
            gate = m[:, j:j + 1] + (m[:, ne + j:ne + j + 1] + m[:, 2 * ne + j:2 * ne + j + 1])
            a = jnp.dot(h, wg_bf[j], preferred_element_type=F32)
            a = a * jax.nn.sigmoid(a) * jnp.dot(h, wu_bf[j], preferred_element_type=F32)
            yj = jnp.dot((a * gate).astype(BF16), wd_bf[j], preferred_element_type=F32)
            y = yj if y is None else y + yj
        ys_ref[...] = y.astype(BF16)


def _moe_call(hs, items, layer, wg, wu, wd):
    d = D_MODEL
    ne = EXPERTS_PER_GROUP
    n_items = items.shape[0] // ITEM_FIELDS
    row_idx = lambda i, items: (items[i], 0)
    hbm = pl.BlockSpec(memory_space=pl.ANY)
    grid_spec = pltpu.PrefetchScalarGridSpec(
        num_scalar_prefetch=1,
        grid=(n_items,),
        in_specs=[pl.BlockSpec((MOE_TILE, ROW_W), row_idx), hbm, hbm, hbm],
        out_specs=pl.BlockSpec((MOE_TILE, d), row_idx),
        scratch_shapes=[pltpu.VMEM((2, ne, d, D_EXPERT), F32), pltpu.VMEM((2, ne, d, D_EXPERT), F32),
                        pltpu.VMEM((2, ne, D_EXPERT, d), F32), pltpu.SemaphoreType.DMA((2, 3)),
                        pltpu.VMEM((ne, d, D_EXPERT), BF16), pltpu.VMEM((ne, d, D_EXPERT), BF16),
                        pltpu.VMEM((ne, D_EXPERT, d), BF16), pltpu.SMEM((1,), I32)],
    )
    return pl.pallas_call(
        functools.partial(_moe_kernel, layer=layer), grid_spec=grid_spec,
        out_shape=jax.ShapeDtypeStruct((hs.shape[0], d), BF16),
        compiler_params=pltpu.CompilerParams(dimension_semantics=("arbitrary",),
                                             vmem_limit_bytes=VMEM_LIMIT),
        name="moe",
    )(items, hs, wg, wu, wd)


def kernel(x, c, ada_w, ada_b, norm_mix_g, norm_ffn_g, hyb_w_in, sgu_ln_g, sgu_ln_b, sgu_w, sgu_b,
           sconv_w, hyb_w_out, conf_w1, conf_b1, conf_dw_w, conf_dw_b, conf_ln_g, conf_ln_b,
           conf_w2, conf_b2, router_w, router_bias, moe_w_gate, moe_w_up, moe_w_down, final_norm_g):
    depth = ada_w.shape[0]
    bsz, seq, d = x.shape
    t = bsz * seq
    assert depth == 2 and seq % SEQ_TILE == 0
    mod = _ada_mod(c, ada_w, ada_b).reshape(depth, bsz, N_MOD, d)

    perm = jnp.arange(N_EXPERTS).reshape(N_GROUPS, EXPERTS_PER_GROUP).T.reshape(-1)
    rw_p = router_w[:, perm]
    rw_t = jnp.concatenate([rw_p, rw_p, jnp.zeros((d, LANES - 2 * N_EXPERTS), F32)], axis=1)
    rb_t = router_bias[perm].reshape(N_EXPERTS, 1)

    ne = EXPERTS_PER_GROUP
    wg = moe_w_gate.reshape(depth * N_GROUPS, ne, d, D_EXPERT)
    wu = moe_w_up.reshape(depth * N_GROUPS, ne, d, D_EXPERT)
    wd = moe_w_down.reshape(depth * N_GROUPS, ne, D_EXPERT, d)

    row = lambda a: a.reshape(1, -1)
    consts0 = [row(norm_mix_g[0]), row(norm_ffn_g[0]), row(sgu_ln_g[0]), row(sgu_ln_b[0]), sgu_w[0],
               sgu_b[0].reshape(A_HEADS, CHUNK, 1), sconv_w[0], rw_t, rb_t]
    x1, hs, q, segb, segn, items = _hybrid_call(x, mod, 0, hyb_w_in, hyb_w_out, consts0)
    ys = _moe_call(hs, items, 0, wg, wu, wd)

    consts1 = [row(norm_mix_g[1]), row(norm_ffn_g[1]), row(conf_b1[0]), conf_dw_w[0],
               row(conf_dw_b[0]), row(conf_ln_g[0]), row(conf_ln_b[0]), row(conf_b2[0]), rw_t, rb_t]
    x1, hs, q1, segb1, segn1, items = _conformer_call((x1, ys, q, segb, segn), mod, 1,
                                                      conf_w1, conf_w2, consts1)
    ys = _moe_call(hs, items, 1, wg, wu, wd)
    return _final_call((x1, ys, q1, segb1, segn1), mod, 1, final_norm_g)
```

```python
import functools

import jax
import jax.numpy as jnp
from jax import lax
from jax.experimental import pallas as pl
from jax.experimental.pallas import tpu as pltpu

D_MODEL = 1024
CHUNK = 128
A_HEADS = 4
A_HEAD_DIM = 128
D_A = A_HEADS * A_HEAD_DIM
D_B = 512
B_CONV = 3
D_IN_HYB = 2 * D_A + 3 * D_B
D_C = D_MODEL
C_CONV = 31
N_EXPERTS = 16
N_GROUPS = 4
EXPERTS_PER_GROUP = N_EXPERTS // N_GROUPS
D_EXPERT = 256
N_MOD = 6
EPS = 1e-6

SUBLANES = 8
LANES = 128
ROW_ALIGN = 16

SEQ_TILE = 512
MOE_TILE = 512
ITEM_FIELDS = 5
MOD_TILE = 2048
W_STAGE_ROWS = 128
C_HALO = 40
B_HALO = 8
CONV_ROWS = 128
GLU_COLS = 256
S_LEAD = ROW_ALIGN
S_ROWS = CONV_ROWS + S_LEAD
S_SLOTS = 2
SORT_ROWS = SEQ_TILE + N_GROUPS * ROW_ALIGN
ROW_W = D_MODEL + LANES
VMEM_LIMIT = 56 * 1024 * 1024

F32 = jnp.float32
BF16 = jnp.bfloat16
I32 = jnp.int32


def _round_up(n, m):
    return -(-n // m) * m


def _moe_steps(t):
    return -(-(t + (t // SEQ_TILE) * N_GROUPS * (ROW_ALIGN - 1)) // MOE_TILE) + N_GROUPS


def _region_rows(t):
    return _round_up(t + (t // SEQ_TILE) * ROW_ALIGN, MOE_TILE)


def _split_bf16(a):
    hi = a.astype(BF16)
    lo = (a - hi.astype(F32)).astype(BF16)
    return hi, lo


def _dot3(a, b):
    ah, al = _split_bf16(a)
    bh, bl = _split_bf16(b)
    d = functools.partial(jnp.dot, preferred_element_type=F32)
    return d(ah, bh) + (d(ah, bl) + d(al, bh))


def _rms(x, g):
    return x * lax.rsqrt(jnp.mean(x * x, axis=-1, keepdims=True) + EPS) * g


def _rms_mod(x, g, shift, scale):
    return x * lax.rsqrt(jnp.mean(x * x, axis=-1, keepdims=True) + EPS) * (g * (1.0 + scale)) + shift


def _gelu(x):
    return 0.5 * x * (1.0 + lax.erf(x * (2.0 ** -0.5)))


def _layer_norm(x, g, b):
    mu = jnp.mean(x, axis=-1, keepdims=True)
    xc = x - mu
    var = jnp.mean(xc * xc, axis=-1, keepdims=True)
    return xc * lax.rsqrt(var + EPS) * g + b


def _mod_kernel(c_ref, w_ref, b_ref, o_ref):
    c = c_ref[...]
    c_act = c * jax.nn.sigmoid(c)
    o_ref[0] = _dot3(c_act, w_ref[0]) + b_ref[0]


def _ada_mod(c, ada_w, ada_b):
    depth, d, n = ada_w.shape
    bsz = c.shape[0]
    return pl.pallas_call(
        _mod_kernel,
        grid=(depth, n // MOD_TILE),
        in_specs=[
            pl.BlockSpec((bsz, d), lambda l, j: (0, 0)),
            pl.BlockSpec((1, d, MOD_TILE), lambda l, j: (l, 0, j)),
            pl.BlockSpec((1, 1, MOD_TILE), lambda l, j: (l, 0, j)),
        ],
        out_specs=pl.BlockSpec((1, bsz, MOD_TILE), lambda l, j: (l, 0, j)),
        out_shape=jax.ShapeDtypeStruct((depth, bsz, n), F32),
        compiler_params=pltpu.CompilerParams(
            dimension_semantics=("arbitrary", "arbitrary"), vmem_limit_bytes=VMEM_LIMIT),
        name="ada_mod",
    )(c, ada_w, ada_b.reshape(depth, 1, n))


def _route(h2, rw_ref, rb_ref):
    wh, wl = _split_bf16(rw_ref[...])
    lane = lax.broadcasted_iota(I32, wh.shape, 1)
    hh, hl = _split_bf16(h2)
    acc = jnp.dot(hh, jnp.where(lane < N_EXPERTS, wh, wl), preferred_element_type=F32)
    low = jnp.dot(hl, wh, preferred_element_type=F32)
    lane = lax.broadcasted_iota(I32, acc.shape, 1)
    acc_t = (acc + jnp.where(lane < N_EXPERTS, low, 0.0)).T
    logits = acc_t[:N_EXPERTS] + acc_t[N_EXPERTS:2 * N_EXPERTS]
    scores = jax.nn.sigmoid(logits)
    biased = scores + rb_ref[...]
    ng, ne = N_GROUPS, EXPERTS_PER_GROUP
    b = [biased[j * ng:(j + 1) * ng] for j in range(ne)]
    s = [scores[j * ng:(j + 1) * ng] for j in range(ne)]
    gscore = None
    for i in range(ne):
        for j in range(i + 1, ne):
            pair = b[i] + b[j]
            gscore = pair if gscore is None else jnp.maximum(gscore, pair)
    gid = lax.broadcasted_iota(I32, gscore.shape, 0)
    gmax = jnp.max(gscore, axis=0, keepdims=True)
    best = jnp.min(jnp.where(gscore == gmax, gid, ng), axis=0, keepdims=True)
    onb = gid == best
    cb = [jnp.sum(jnp.where(onb, b[j], 0.0), axis=0, keepdims=True) for j in range(ne)]
    cs = [jnp.sum(jnp.where(onb, s[j], 0.0), axis=0, keepdims=True) for j in range(ne)]
    sel = []
    for j in range(ne):
        rank = jnp.zeros_like(best)
        for i in range(ne):
            if i == j:
                continue
            ahead = (cb[i] >= cb[j]) if i < j else (cb[i] > cb[j])
            rank = rank + ahead.astype(I32)
        sel.append(rank < 2)
    wsum = sum(jnp.where(sel[j], cs[j], 0.0) for j in range(ne))
    gates = jnp.concatenate([jnp.where(sel[j], cs[j], 0.0) / wsum for j in range(ne)], axis=0)
    return gates, jnp.where(onb, 1.0, 0.0)


def _sort_matrix(q):
    rows = lax.broadcasted_iota(I32, (SORT_ROWS, q.shape[1]), 0)
    return jnp.where(rows == q, 1.0, 0.0).astype(BF16)


def _run_copies(op, src_of, dst_of, sem, nrows, bases, region_rows):
    if op == "wait":
        total = pl.multiple_of(sum(nrows[1:], nrows[0]), ROW_ALIGN)

        @pl.when(total > 0)
        def _():
            pltpu.make_async_copy(src_of(0, 0, total), dst_of(0, 0, total), sem).wait()
        return
    tile_off = 0
    for g in range(N_GROUPS):
        n = pl.multiple_of(nrows[g], ROW_ALIGN)
        a = pl.multiple_of(tile_off, ROW_ALIGN)
        b = pl.multiple_of(g * region_rows + bases[g], ROW_ALIGN)

        @pl.when(n > 0)
        def _(a=a, b=b, n=n):
            pltpu.make_async_copy(src_of(a, b, n), dst_of(a, b, n), sem).start()

        tile_off = tile_off + nrows[g]


class _ScatterRefs:
    def __init__(self, gffn_ref, rw_ref, rb_ref, hs_ref, q_ref, segb_ref, segn_ref, items_ref,
                 h2buf, rowbuf, tri, zbuf, fill_s, pend_s, ssem, zsem):
        self.__dict__.update(locals())
        self.region_rows = hs_ref.shape[0] // N_GROUPS
        self.dst_of = lambda a, b, size: hs_ref.at[pl.ds(b, size)]


def _sc_init(sc, tile):
    @pl.when(tile == 0)
    def _():
        for g in range(N_GROUPS):
            sc.fill_s[g] = 0
        for i in range(sc.pend_s.shape[0]):
            sc.pend_s[i] = 0
        ts = sc.tri.shape[0]
        k = lax.broadcasted_iota(I32, (ts, ts), 0)
        t = lax.broadcasted_iota(I32, (ts, ts), 1)
        sc.tri[...] = jnp.where(k <= t, 1.0, 0.0).astype(BF16)
        sc.h2buf[...] = jnp.zeros_like(sc.h2buf)


def _sc_pending(sc, op, slot):
    rec = slot * (2 * N_GROUPS)
    src_of = lambda a, b, size: sc.rowbuf.at[slot, pl.ds(a, size)]
    _run_copies(op, src_of, sc.dst_of, sc.ssem.at[slot], [sc.pend_s[rec + g] for g in range(N_GROUPS)],
                [sc.pend_s[rec + N_GROUPS + g] for g in range(N_GROUPS)], sc.region_rows)


def _prenorm(x1, mod, gffn_ref):
    sh2, sc2 = mod[3:4], mod[4:5]
    return _rms_mod(x1, gffn_ref[...], sh2, sc2)


def _sc_route(sc, st):
    st["h2"] = sc.h2buf[...]
    st["gates"], st["onehot"] = _route(st["h2"], sc.rw_ref, sc.rb_ref)


def _sc_rank(sc, st, active):
    onehot = st["onehot"]
    ts = onehot.shape[1]
    oh8 = jnp.concatenate([onehot, jnp.zeros((SUBLANES - N_GROUPS, ts), F32)], axis=0).astype(BF16)
    csum = jnp.dot(oh8, sc.tri[...], preferred_element_type=F32)[:N_GROUPS]
    counts = [csum[g, ts - 1].astype(I32) * active for g in range(N_GROUPS)]
    st["nrows"] = [(c + (ROW_ALIGN - 1)) & ~(ROW_ALIGN - 1) for c in counts]
    q = jnp.zeros((1, ts), F32)
    tile_off = 0
    for g in range(N_GROUPS):
        q = q + onehot[g:g + 1] * (csum[g:g + 1] - 1.0 + tile_off.astype(F32)
                                   if g else csum[g:g + 1] - 1.0)
        tile_off = tile_off + st["nrows"][g]
    st["q"] = q.astype(I32)


def _sc_sort(sc, st, slot):
    gates = st["gates"]
    ts = gates.shape[1]
    hi = gates.astype(BF16).astype(F32)
    mid = (gates - hi).astype(BF16).astype(F32)
    lo = (gates - hi - mid).astype(BF16).astype(F32)
    meta_t = jnp.concatenate(
        [hi, mid, lo, jnp.zeros((LANES - 3 * EXPERTS_PER_GROUP, ts), F32)], axis=0)
    rows = jnp.concatenate([st["h2"].astype(BF16), meta_t.T.astype(BF16)], axis=1)
    sorted_rows = jnp.dot(_sort_matrix(st["q"]), rows, preferred_element_type=F32)
    sc.rowbuf[slot] = sorted_rows.astype(BF16)


def _sc_start(sc, st, t_idx, slot):
    sc.q_ref[t_idx] = st["q"]
    nrows = st["nrows"]
    rec = slot * (2 * N_GROUPS)
    for g in range(N_GROUPS):
        base = sc.fill_s[g]
        sc.pend_s[rec + g] = nrows[g]
        sc.pend_s[rec + N_GROUPS + g] = base
        sc.segn_ref[t_idx * N_GROUPS + g] = nrows[g]
        sc.segb_ref[t_idx * N_GROUPS + g] = base
        sc.fill_s[g] = base + nrows[g]
    _sc_pending(sc, "start", slot)


def _write_moe_items(sc):
    n_items = sc.items_ref.shape[0] // ITEM_FIELDS
    per_region = sc.region_rows // MOE_TILE
    ends = []
    total = 0
    for g in range(N_GROUPS):
        total = total + (sc.fill_s[g] + (MOE_TILE - 1)) // MOE_TILE
        ends.append(total)
    nxt = [None] * N_GROUPS
    follow = -1
    for g in reversed(range(N_GROUPS)):
        nxt[g] = follow
        follow = jnp.where(ends[g] > (ends[g - 1] if g else 0), g, follow)
    last = jnp.maximum(total - 1, 0)
    prev_grp = None
    for i in range(n_items):
        ic = jnp.minimum(i, last)
        grp = sum((ic >= e).astype(I32) for e in ends[:-1])
        start = jnp.where(grp == 0, 0, jnp.where(grp == 1, ends[0], jnp.where(grp == 2, ends[1], ends[2])))
        fresh = jnp.ones((), I32) if prev_grp is None else (grp != prev_grp).astype(I32)
        sc.items_ref[i] = grp * per_region + ic - start
        sc.items_ref[n_items + i] = grp
        sc.items_ref[2 * n_items + i] = (i < total).astype(I32)
        sc.items_ref[3 * n_items + i] = fresh
        sc.items_ref[4 * n_items + i] = jnp.where(
            grp == 0, nxt[0], jnp.where(grp == 1, nxt[1], jnp.where(grp == 2, nxt[2], nxt[3])))
        prev_grp = grp


def _sc_finish(sc):
    _sc_pending(sc, "wait", 0)
    _sc_pending(sc, "wait", 1)
    sc.zbuf[...] = jnp.zeros_like(sc.zbuf)
    for op in ("start", "wait"):
        for g in range(N_GROUPS):
            fill = sc.fill_s[g]
            n = pl.multiple_of((-fill) & (MOE_TILE - 1), ROW_ALIGN)
            b = pl.multiple_of(g * sc.region_rows + fill, ROW_ALIGN)

            @pl.when(n > 0)
            def _(n=n, b=b, op=op):
                cp = pltpu.make_async_copy(sc.zbuf.at[pl.ds(0, n)], sc.hs_ref.at[pl.ds(b, n)], sc.zsem)
                getattr(cp, op)()
    _write_moe_items(sc)


def _gather_prev_ffn(tile, n_tiles, segb_ref, segn_ref, ys_ref, q_ref, ybuf, gsem):
    region_rows = ys_ref.shape[0] // N_GROUPS

    def copies(op, t, slot):
        nrows = [segn_ref[t * N_GROUPS + g] for g in range(N_GROUPS)]
        bases = [segb_ref[t * N_GROUPS + g] for g in range(N_GROUPS)]
        src_of = lambda a, b, size: ys_ref.at[pl.ds(b, size)]
        dst_of = lambda a, b, size: ybuf.at[slot, pl.ds(a, size)]
        _run_copies(op, src_of, dst_of, gsem.at[slot], nrows, bases, region_rows)

    def issue(t, slot):
        ybuf[slot, SEQ_TILE:SORT_ROWS, :] = jnp.zeros((SORT_ROWS - SEQ_TILE, D_MODEL), BF16)
        copies("start", t, slot)

    @pl.when(tile == 0)
    def _():
        issue(tile, 0)

    @pl.when(tile + 1 < n_tiles)
    def _():
        issue(tile + 1, (tile + 1) % 2)

    slot = tile % 2
    copies("wait", tile, slot)
    p = _sort_matrix(q_ref[0])
    return lax.dot_general(p, ybuf[slot], (((0,), (0,)), ((), ())), preferred_element_type=F32)


class _GatherAhead:
    def __init__(self, tile, n_tiles, segb_ref, segn_ref, ys_ref, q_ref, qnext_ref, ybuf, gsem, ynext):
        self.__dict__.update(locals())
        self.region_rows = ys_ref.shape[0] // N_GROUPS
        self.slots = ybuf.shape[0]

    def copies(self, op, t):
        slot = t % self.slots
        nrows = [self.segn_ref[t * N_GROUPS + g] for g in range(N_GROUPS)]
        bases = [self.segb_ref[t * N_GROUPS + g] for g in range(N_GROUPS)]
        src_of = lambda a, b, size: self.ys_ref.at[pl.ds(b, size)]
        dst_of = lambda a, b, size: self.ybuf.at[slot, pl.ds(a, size)]
        _run_copies(op, src_of, dst_of, self.gsem.at[slot], nrows, bases, self.region_rows)

    def issue(self, t):
        self.ybuf[t % self.slots, SEQ_TILE:SORT_ROWS, :] = jnp.zeros((SORT_ROWS - SEQ_TILE, D_MODEL), BF16)
        self.copies("start", t)

    def unsort(self, t, q):
        y = lax.dot_general(_sort_matrix(q), self.ybuf[t % self.slots], (((0,), (0,)), ((), ())),
                            preferred_element_type=F32)
        self.ynext[...] = y.astype(BF16)

    def top(self):
        tile, n_tiles = self.tile, self.n_tiles

        @pl.when(tile == 0)
        def _():
            self.issue(tile)

            @pl.when(n_tiles > 1)
            def _():
                self.issue(tile + 1)

            self.copies("wait", tile)
            self.unsort(tile, self.q_ref[0])

        @pl.when(tile + 2 < n_tiles)
        def _():
            self.issue(tile + 2)

        @pl.when(jnp.logical_and(tile >= 1, tile + 1 < n_tiles))
        def _():
            self.copies("wait", tile + 1)

        @pl.when(jnp.logical_and(tile == 0, n_tiles > 1))
        def _():
            self.copies("wait", tile + 1)

    def current(self):
        return self.ynext[...].astype(F32)

    def prepare_next(self):
        self.unsort(jnp.minimum(self.tile + 1, self.n_tiles - 1), self.qnext_ref[0])


def _load_weight_bf16(w_hbm, w_vmem, stage, sem):
    rows = stage.shape[1]
    n = w_vmem.shape[0] // rows
    copy = lambda c: pltpu.make_async_copy(w_hbm.at[pl.ds(c * rows, rows)], stage.at[c % 2],
                                           sem.at[c % 2])
    copy(0).start()
    for c in range(n):
        if c + 1 < n:
            copy(c + 1).start()
        copy(c).wait()
        w_vmem[c * rows:(c + 1) * rows, :] = stage[c % 2].astype(BF16)


def _weight_scratch(k, n):
    return [pltpu.VMEM((k, n), BF16), pltpu.VMEM((2, W_STAGE_ROWS, n), F32),
            pltpu.SemaphoreType.DMA((2,))]


def _hybrid_kernel(x_ref, mod_ref, win_hbm, wout_hbm, gmix_ref, gffn_ref, lng_ref, lnb_ref, sw_ref,
                   sb_ref, cw_ref, rw_ref, rb_ref,
                   x1_ref, hs_ref, q_ref, segb_ref, segn_ref, items_ref,
                   pbuf, win_ref, win_stage, win_sem, wout_ref, wout_stage, wout_sem,
                   h2buf, rowbuf, tri, zbuf, fill_s, pend_s, ssem, zsem):
    ts = x_ref.shape[1]
    tile = pl.program_id(0) * pl.num_programs(1) + pl.program_id(1)
    n_tiles = pl.num_programs(0) * pl.num_programs(1)
    sc = _ScatterRefs(gffn_ref, rw_ref, rb_ref, hs_ref, q_ref, segb_ref, segn_ref, items_ref,
                      h2buf, rowbuf, tri, zbuf, fill_s, pend_s, ssem, zsem)

    @pl.when(tile == 0)
    def _():
        _load_weight_bf16(win_hbm.at[0], win_ref, win_stage, win_sem)
        _load_weight_bf16(wout_hbm.at[0], wout_ref, wout_stage, wout_sem)

    @pl.when(pl.program_id(1) == 0)
    def _():
        pbuf[0:B_HALO, :] = jnp.zeros((B_HALO, D_B), F32)

    _sc_init(sc, tile)
    slot = tile % 2
    _sc_pending(sc, "wait", slot)

    prev = {}
    active = (tile > 0).astype(I32)

    _sc_route(sc, prev)
    _sc_rank(sc, prev, active)
    _sc_sort(sc, prev, slot)

    x = x_ref[0]
    mod = mod_ref[0, 0]
    sh1, sc1, g1 = mod[0:1], mod[1:2], mod[2:3]
    h = _rms_mod(x, gmix_ref[...], sh1, sc1)
    hb16 = h.astype(BF16)
    zcol = lambda j: jnp.dot(hb16, win_ref[:, j * D_A:(j + 1) * D_A], preferred_element_type=F32)

    v = _layer_norm(_gelu(zcol(1)), lng_ref[...], lnb_ref[...]).astype(BF16)
    u = _gelu(zcol(0))
    tril = (lax.broadcasted_iota(I32, (CHUNK, CHUNK), 0)
            >= lax.broadcasted_iota(I32, (CHUNK, CHUNK), 1))
    chunks = []
    for n in range(ts // CHUNK):
        heads = []
        for hd in range(A_HEADS):
            w = jnp.where(tril, sw_ref[hd], 0.0).astype(BF16)
            vb = v[n * CHUNK:(n + 1) * CHUNK, hd * A_HEAD_DIM:(hd + 1) * A_HEAD_DIM]
            heads.append(jnp.dot(w, vb, preferred_element_type=F32) + sb_ref[hd])
        chunks.append(jnp.concatenate(heads, axis=1))
    y_a = u * jnp.concatenate(chunks, axis=0)

    p = zcol(3) * zcol(4)
    pbuf[B_HALO:B_HALO + ts, :] = p
    cw = cw_ref[...]
    conv = (cw[0:1] * pbuf[B_HALO - 2:B_HALO - 2 + ts, :]
            + cw[1:2] * pbuf[B_HALO - 1:B_HALO - 1 + ts, :]
            + cw[2:3] * p)
    pbuf[0:B_HALO, :] = p[ts - B_HALO:ts, :]
    y_b = zcol(2) * conv

    ycat = jnp.concatenate([y_a, y_b], axis=1).astype(BF16)
    y = jnp.dot(ycat, wout_ref[...], preferred_element_type=F32)
    x1 = x + g1 * y
    x1_ref[0] = x1
    h2buf[...] = _prenorm(x1, mod, gffn_ref)
    _sc_start(sc, prev, jnp.maximum(tile - 1, 0), slot)

    @pl.when(tile == n_tiles - 1)
    def _():
        _sc_pending(sc, "wait", 1 - slot)
        last = {}
        _sc_route(sc, last)
        _sc_rank(sc, last, 1)
        _sc_sort(sc, last, 1 - slot)
        _sc_start(sc, last, tile, 1 - slot)
        _sc_finish(sc)


def _band_matrix():
    shape = (CONV_ROWS, SUBLANES * S_ROWS)
    t = lax.broadcasted_iota(I32, shape, 0)
    c = lax.broadcasted_iota(I32, shape, 1)
    hit = c == t + S_LEAD
    for r in range(1, SUBLANES):
        hit = hit | (c == t + S_LEAD + r * (S_ROWS - 1))
    return jnp.where(hit, 1.0, 0.0).astype(BF16)


def _causal_dw_conv(cbuf, w8, obuf, band, sbuf, row0, nrows, col0, ncols, first_slot):
    n_a = -(-C_CONV // SUBLANES)
    cols = slice(col0, col0 + ncols)
    for i, t0 in enumerate(range(row0, row0 + nrows, CONV_ROWS)):
        stack = sbuf.at[(first_slot + i) % sbuf.shape[0]]
        for r in range(SUBLANES):
            part = None
            for a in range(n_a):
                k = C_CONV - 1 - SUBLANES * a - r
                if k < 0:
                    continue
                r0 = t0 + C_HALO - S_LEAD - SUBLANES * a
                z3 = cbuf[r0:r0 + S_ROWS, cols].reshape(S_ROWS // SUBLANES, SUBLANES, ncols)
                term = (z3 * w8[SUBLANES * k:SUBLANES * (k + 1), cols][None]).reshape(S_ROWS, ncols)
                part = term if part is None else part + term
            stack[r * S_ROWS:(r + 1) * S_ROWS, :] = part.astype(BF16)
        obuf[t0:t0 + CONV_ROWS, cols] = jnp.dot(band[...], stack[...], preferred_element_type=F32)


def _conformer_kernel(psegb_ref, psegn_ref,
                      xp_ref, ys_ref, pq_ref, pmod_ref, pqn_ref, mod_ref, w1_hbm, w2_hbm, gmix_ref, gffn_ref,
                      b1_ref, dww_ref, dwb_ref, lng_ref, lnb_ref, b2_ref, rw_ref, rb_ref,
                      x1_ref, hs_ref, q_ref, segb_ref, segn_ref, items_ref,
                      cbuf, obuf, band, sbuf, w8, ynext, ybuf, gsem,
                      w1_ref, w1_stage, w1_sem, w2_ref, w2_stage, w2_sem,
                      h2buf, rowbuf, tri, zbuf, fill_s, pend_s, ssem, zsem):
    ts = xp_ref.shape[1]
    tile = pl.program_id(0) * pl.num_programs(1) + pl.program_id(1)
    n_tiles = pl.num_programs(0) * pl.num_programs(1)
    sc = _ScatterRefs(gffn_ref, rw_ref, rb_ref, hs_ref, q_ref, segb_ref, segn_ref, items_ref,
                      h2buf, rowbuf, tri, zbuf, fill_s, pend_s, ssem, zsem)

    @pl.when(tile == 0)
    def _():
        _load_weight_bf16(w1_hbm.at[0], w1_ref, w1_stage, w1_sem)
        _load_weight_bf16(w2_hbm.at[0], w2_ref, w2_stage, w2_sem)
        band[...] = _band_matrix()
        for k in range(C_CONV):
            w8[SUBLANES * k:SUBLANES * (k + 1), :] = jnp.broadcast_to(dww_ref[k:k + 1, :], (SUBLANES, D_C))

    @pl.when(pl.program_id(1) == 0)
    def _():
        cbuf[0:C_HALO, :] = jnp.zeros((C_HALO, D_C), F32)

    _sc_init(sc, tile)
    slot = tile % 2
    _sc_pending(sc, "wait", slot)
    ga = _GatherAhead(tile, n_tiles, psegb_ref, psegn_ref, ys_ref, pq_ref, pqn_ref, ybuf, gsem, ynext)
    ga.top()
    y_prev = ga.current()

    prev = {}
    active = (tile > 0).astype(I32)

    x = xp_ref[0] + pmod_ref[0, 0][5:6] * y_prev
    mod = mod_ref[0, 0]
    sh1, sc1, g1 = mod[0:1], mod[1:2], mod[2:3]
    h = _rms_mod(x, gmix_ref[...], sh1, sc1)
    hb = h.astype(BF16)
    def glu(c0):
        ca, cg = slice(c0, c0 + GLU_COLS), slice(D_C + c0, D_C + c0 + GLU_COLS)
        za = jnp.dot(hb, w1_ref[:, ca], preferred_element_type=F32) + b1_ref[:, ca]
        zg = jnp.dot(hb, w1_ref[:, cg], preferred_element_type=F32) + b1_ref[:, cg]
        cbuf[C_HALO:C_HALO + ts, ca] = za * jax.nn.sigmoid(zg)

    for c0 in range(0, D_C, GLU_COLS):
        glu(c0)
    _sc_route(sc, prev)
    _sc_rank(sc, prev, active)
    _sc_sort(sc, prev, slot)
    ga.prepare_next()
    for k, c0 in enumerate(range(0, D_C, GLU_COLS)):
        _causal_dw_conv(cbuf, w8, obuf, band, sbuf, 0, ts, c0, GLU_COLS,
                        k * (ts // CONV_ROWS))
    cbuf[0:C_HALO, :] = cbuf[ts:ts + C_HALO, :]

    zc = _layer_norm(obuf[...] + dwb_ref[...], lng_ref[...], lnb_ref[...])
    zc = zc * jax.nn.sigmoid(zc)
    y = jnp.dot(zc.astype(BF16), w2_ref[...], preferred_element_type=F32) + b2_ref[...]
    x1 = x + g1 * y
    x1_ref[0] = x1
    h2buf[...] = _prenorm(x1, mod, gffn_ref)
    _sc_start(sc, prev, jnp.maximum(tile - 1, 0), slot)

    @pl.when(tile == n_tiles - 1)
    def _():
        _sc_pending(sc, "wait", 1 - slot)
        last = {}
        _sc_route(sc, last)
        _sc_rank(sc, last, 1)
        _sc_sort(sc, last, 1 - slot)
        _sc_start(sc, last, tile, 1 - slot)
        _sc_finish(sc)


def _final_kernel(psegb_ref, psegn_ref, xp_ref, ys_ref, pq_ref, pmod_ref, fg_ref, o_ref, ybuf, gsem):
    tile = pl.program_id(0) * pl.num_programs(1) + pl.program_id(1)
    n_tiles = pl.num_programs(0) * pl.num_programs(1)
    y_prev = _gather_prev_ffn(tile, n_tiles, psegb_ref, psegn_ref, ys_ref, pq_ref, ybuf, gsem)
    x = xp_ref[0] + pmod_ref[0, 0][5:6] * y_prev
    o_ref[0] = _rms(x, fg_ref[...])


def _scatter_outputs(bsz, seq, d):
    t = bsz * seq
    n_tiles = t // SEQ_TILE
    smem = pl.BlockSpec(memory_space=pltpu.SMEM)
    specs = [pl.BlockSpec((1, SEQ_TILE, d), lambda b, s, *_: (b, s, 0)),
             pl.BlockSpec(memory_space=pl.ANY),
             pl.BlockSpec((n_tiles, 1, SEQ_TILE), lambda b, s, *_: (0, 0, 0)),
             smem, smem, smem]
    shapes = [jax.ShapeDtypeStruct((bsz, seq, d), F32),
              jax.ShapeDtypeStruct((N_GROUPS * _region_rows(t), ROW_W), BF16),
              jax.ShapeDtypeStruct((n_tiles, 1, SEQ_TILE), I32),
              jax.ShapeDtypeStruct((n_tiles * N_GROUPS,), I32),
              jax.ShapeDtypeStruct((n_tiles * N_GROUPS,), I32),
              jax.ShapeDtypeStruct((ITEM_FIELDS * _moe_steps(t),), I32)]
    scratch = [pltpu.VMEM((SEQ_TILE, d), F32),
               pltpu.VMEM((2, SORT_ROWS, ROW_W), BF16),
               pltpu.VMEM((SEQ_TILE, SEQ_TILE), BF16),
               pltpu.VMEM((MOE_TILE, ROW_W), BF16),
               pltpu.SMEM((N_GROUPS,), I32),
               pltpu.SMEM((4 * N_GROUPS,), I32),
               pltpu.SemaphoreType.DMA((2,)),
               pltpu.SemaphoreType.DMA(())]
    return specs, shapes, scratch


def _mod_spec(d, layer):
    return pl.BlockSpec((1, 1, N_MOD, d), lambda b, s, *_: (layer, b, 0, 0))


def _gather_inputs(bsz, seq, d, prev_layer):
    nseq = seq // SEQ_TILE
    specs = [pl.BlockSpec((1, SEQ_TILE, d), lambda b, s, *_: (b, s, 0)),
             pl.BlockSpec(memory_space=pl.ANY),
             pl.BlockSpec((1, 1, SEQ_TILE), lambda b, s, *_: (b * nseq + s, 0, 0)),
             _mod_spec(d, prev_layer)]
    scratch = [pltpu.VMEM((2, SORT_ROWS, d), BF16), pltpu.SemaphoreType.DMA((2,))]
    return specs, scratch


def _const_specs(consts):
    return [pl.BlockSpec(a.shape, lambda b, s, *_, nd=a.ndim: (0,) * nd) for a in consts]


_PARAMS = pltpu.CompilerParams(dimension_semantics=("arbitrary", "arbitrary"),
                               vmem_limit_bytes=VMEM_LIMIT)


def _hybrid_call(x, mod, layer, w_in, w_out, consts):
    bsz, seq, d = x.shape
    out_specs, out_shapes, sc_scratch = _scatter_outputs(bsz, seq, d)
    hbm = pl.BlockSpec(memory_space=pl.ANY)
    return pl.pallas_call(
        _hybrid_kernel,
        grid=(bsz, seq // SEQ_TILE),
        in_specs=[pl.BlockSpec((1, SEQ_TILE, d), lambda b, s: (b, s, 0)),
                  _mod_spec(d, layer), hbm, hbm]
                 + _const_specs(consts),
        out_specs=out_specs,
        out_shape=out_shapes,
        scratch_shapes=[pltpu.VMEM((SEQ_TILE + B_HALO, D_B), F32)]
                       + _weight_scratch(*w_in.shape[1:]) + _weight_scratch(*w_out.shape[1:])
                       + sc_scratch,
        compiler_params=_PARAMS,
        name="mixer_hybrid",
    )(x, mod, w_in, w_out, *consts)


def _conformer_call(prev, mod, layer, w1, w2, consts):
    x1p, ysp, qp, segbp, segnp = prev
    bsz, seq, d = x1p.shape
    g_specs, _ = _gather_inputs(bsz, seq, d, layer - 1)
    nseq = seq // SEQ_TILE
    last = bsz * nseq - 1
    g_specs = g_specs + [pl.BlockSpec((1, 1, SEQ_TILE),
                                      lambda b, s, *_: (jnp.minimum(b * nseq + s + 1, last), 0, 0))]
    g_scratch = [pltpu.VMEM((SEQ_TILE, d), BF16), pltpu.VMEM((3, SORT_ROWS, d), BF16),
                 pltpu.SemaphoreType.DMA((3,))]
    out_specs, out_shapes, sc_scratch = _scatter_outputs(bsz, seq, d)
    hbm = pl.BlockSpec(memory_space=pl.ANY)
    grid_spec = pltpu.PrefetchScalarGridSpec(
        num_scalar_prefetch=2,
        grid=(bsz, seq // SEQ_TILE),
        in_specs=g_specs + [_mod_spec(d, layer), hbm, hbm]
                 + _const_specs(consts),
        out_specs=out_specs,
        scratch_shapes=[pltpu.VMEM((SEQ_TILE + C_HALO, D_C), F32), pltpu.VMEM((SEQ_TILE, D_C), F32),
                        pltpu.VMEM((CONV_ROWS, SUBLANES * S_ROWS), BF16),
                        pltpu.VMEM((S_SLOTS, SUBLANES * S_ROWS, GLU_COLS), BF16),
                        pltpu.VMEM((C_CONV * SUBLANES, D_C), F32)]
                       + g_scratch + _weight_scratch(*w1.shape[1:]) + _weight_scratch(*w2.shape[1:])
                       + sc_scratch,
    )
    return pl.pallas_call(
        _conformer_kernel, grid_spec=grid_spec, out_shape=out_shapes,
        compiler_params=_PARAMS, name="mixer_conformer",
    )(segbp, segnp, x1p, ysp, qp, mod, qp, mod, w1, w2, *consts)


def _final_call(prev, mod, prev_layer, final_g):
    x1p, ysp, qp, segbp, segnp = prev
    bsz, seq, d = x1p.shape
    g_specs, g_scratch = _gather_inputs(bsz, seq, d, prev_layer)
    grid_spec = pltpu.PrefetchScalarGridSpec(
        num_scalar_prefetch=2,
        grid=(bsz, seq // SEQ_TILE),
        in_specs=g_specs + [pl.BlockSpec((1, d), lambda b, s, *_: (0, 0))],
        out_specs=pl.BlockSpec((1, SEQ_TILE, d), lambda b, s, *_: (b, s, 0)),
        scratch_shapes=g_scratch,
    )
    return pl.pallas_call(
        _final_kernel, grid_spec=grid_spec,
        out_shape=jax.ShapeDtypeStruct((bsz, seq, d), F32),
        compiler_params=_PARAMS, name="final_norm",
    )(segbp, segnp, x1p, ysp, qp, mod, final_g.reshape(1, d))


def _moe_kernel(items_ref, hs_ref, wg_hbm, wu_hbm, wd_hbm, ys_ref,
                wg_st, wu_st, wd_st, wsem, wg_bf, wu_bf, wd_bf, slot_s, *, layer):
    i = pl.program_id(0)
    n_items = pl.num_programs(0)
    grp = items_ref[n_items + i]

    def fetch(g, slot):
        src = layer * N_GROUPS + g
        return [pltpu.make_async_copy(w.at[src], st.at[slot], wsem.at[slot, k])
                for k, (w, st) in enumerate(((wg_hbm, wg_st), (wu_hbm, wu_st), (wd_hbm, wd_st)))]

    @pl.when(i == 0)
    def _():
        slot_s[0] = 0
        for cp in fetch(grp, 0):
            cp.start()

    @pl.when(items_ref[3 * n_items + i] != 0)
    def _():
        slot = slot_s[0]
        for cp in fetch(grp, slot):
            cp.wait()
        wg_bf[...] = wg_st[slot].astype(BF16)
        wu_bf[...] = wu_st[slot].astype(BF16)
        wd_bf[...] = wd_st[slot].astype(BF16)
        nxt = items_ref[4 * n_items + i]

        @pl.when(nxt >= 0)
        def _():
            for cp in fetch(nxt, 1 - slot):
                cp.start()

        slot_s[0] = 1 - slot

    @pl.when(items_ref[2 * n_items + i] != 0)
    def _():
        rows = hs_ref[...]
        h = rows[:, :D_MODEL]
        m = rows[:, D_MODEL:].astype(F32)
        ne = EXPERTS_PER_GROUP
        y = None
        for j in range(ne):
            gate = m[:, j:j + 1] + (m[:, ne + j:ne + j + 1] + m[:, 2 * ne + j:2 * ne + j + 1])
            a = jnp.dot(h, wg_bf[j], preferred_element_type=F32)
            a = a * jax.nn.sigmoid(a) * jnp.dot(h, wu_bf[j], preferred_element_type=F32)
            yj = jnp.dot((a * gate).astype(BF16), wd_bf[j], preferred_element_type=F32)
            y = yj if y is None else y + yj
        ys_ref[...] = y.astype(BF16)


def _moe_call(hs, items, layer, wg, wu, wd):
    d = D_MODEL
    ne = EXPERTS_PER_GROUP
    n_items = items.shape[0] // ITEM_FIELDS
    row_idx = lambda i, items: (items[i], 0)
    hbm = pl.BlockSpec(memory_space=pl.ANY)
    grid_spec = pltpu.PrefetchScalarGridSpec(
        num_scalar_prefetch=1,
        grid=(n_items,),
        in_specs=[pl.BlockSpec((MOE_TILE, ROW_W), row_idx), hbm, hbm, hbm],
        out_specs=pl.BlockSpec((MOE_TILE, d), row_idx),
        scratch_shapes=[pltpu.VMEM((2, ne, d, D_EXPERT), F32), pltpu.VMEM((2, ne, d, D_EXPERT), F32),
                        pltpu.VMEM((2, ne, D_EXPERT, d), F32), pltpu.SemaphoreType.DMA((2, 3)),
                        pltpu.VMEM((ne, d, D_EXPERT), BF16), pltpu.VMEM((ne, d, D_EXPERT), BF16),
                        pltpu.VMEM((ne, D_EXPERT, d), BF16), pltpu.SMEM((1,), I32)],
    )
    return pl.pallas_call(
        functools.partial(_moe_kernel, layer=layer), grid_spec=grid_spec,
        out_shape=jax.ShapeDtypeStruct((hs.shape[0], d), BF16),
        compiler_params=pltpu.CompilerParams(dimension_semantics=("arbitrary",),
                                             vmem_limit_bytes=VMEM_LIMIT),
        name="moe",
    )(items, hs, wg, wu, wd)


def kernel(x, c, ada_w, ada_b, norm_mix_g, norm_ffn_g, hyb_w_in, sgu_ln_g, sgu_ln_b, sgu_w, sgu_b,
           sconv_w, hyb_w_out, conf_w1, conf_b1, conf_dw_w, conf_dw_b, conf_ln_g, conf_ln_b,
           conf_w2, conf_b2, router_w, router_bias, moe_w_gate, moe_w_up, moe_w_down, final_norm_g):
    depth = ada_w.shape[0]
    bsz, seq, d = x.shape
    assert depth == 2 and seq % SEQ_TILE == 0
    mod = _ada_mod(c, ada_w, ada_b).reshape(depth, bsz, N_MOD, d)

    perm = jnp.arange(N_EXPERTS).reshape(N_GROUPS, EXPERTS_PER_GROUP).T.reshape(-1)
    rw_p = router_w[:, perm]
    rw_t = jnp.concatenate([rw_p, rw_p, jnp.zeros((d, LANES - 2 * N_EXPERTS), F32)], axis=1)
    rb_t = router_bias[perm].reshape(N_EXPERTS, 1)

    ne = EXPERTS_PER_GROUP
    wg = moe_w_gate.reshape(depth * N_GROUPS, ne, d, D_EXPERT)
    wu = moe_w_up.reshape(depth * N_GROUPS, ne, d, D_EXPERT)
    wd = moe_w_down.reshape(depth * N_GROUPS, ne, D_EXPERT, d)

    row = lambda a: a.reshape(1, -1)
    consts0 = [row(norm_mix_g[0]), row(norm_ffn_g[0]), row(sgu_ln_g[0]), row(sgu_ln_b[0]), sgu_w[0],
               sgu_b[0].reshape(A_HEADS, CHUNK, 1), sconv_w[0], rw_t, rb_t]
    x1, hs, q, segb, segn, items = _hybrid_call(x, mod, 0, hyb_w_in, hyb_w_out, consts0)
    ys = _moe_call(hs, items, 0, wg, wu, wd)

    consts1 = [row(norm_mix_g[1]), row(norm_ffn_g[1]), row(conf_b1[0]), conf_dw_w[0],
               row(conf_dw_b[0]), row(conf_ln_g[0]), row(conf_ln_b[0]), row(conf_b2[0]), rw_t, rb_t]
    x1, hs, q1, segb1, segn1, items = _conformer_call((x1, ys, q, segb, segn), mod, 1,
                                                      conf_w1, conf_w2, consts1)
    ys = _moe_call(hs, items, 1, wg, wu, wd)
    return _final_call((x1, ys, q1, segb1, segn1), mod, 1, final_norm_g)
```

```python
import functools

import jax
import jax.numpy as jnp
from jax import lax
from jax.experimental import pallas as pl
from jax.experimental.pallas import tpu as pltpu

D_MODEL = 1024
CHUNK = 128
A_HEADS = 4
A_HEAD_DIM = 128
D_A = A_HEADS * A_HEAD_DIM
D_B = 512
B_CONV = 3
D_IN_HYB = 2 * D_A + 3 * D_B
D_C = D_MODEL
C_CONV = 31
N_EXPERTS = 16
N_GROUPS = 4
EXPERTS_PER_GROUP = N_EXPERTS // N_GROUPS
D_EXPERT = 256
N_MOD = 6
EPS = 1e-6

SUBLANES = 8
LANES = 128
ROW_ALIGN = 16

SEQ_TILE = 512
MOE_TILE = 512
ITEM_FIELDS = 5
MOD_TILE = 2048
W_STAGE_ROWS = 128
C_HALO = 40
B_HALO = 8
CONV_ROWS = 128
GLU_COLS = 256
S_LEAD = ROW_ALIGN
S_ROWS = CONV_ROWS + S_LEAD
S_SLOTS = 2
SORT_ROWS = SEQ_TILE + N_GROUPS * ROW_ALIGN
ROW_W = D_MODEL + LANES
VMEM_LIMIT = 56 * 1024 * 1024

F32 = jnp.float32
BF16 = jnp.bfloat16
I32 = jnp.int32


def _round_up(n, m):
    return -(-n // m) * m


def _moe_steps(t):
    return -(-(t + (t // SEQ_TILE) * N_GROUPS * (ROW_ALIGN - 1)) // MOE_TILE) + N_GROUPS


def _region_rows(t):
    return _round_up(t + (t // SEQ_TILE) * ROW_ALIGN, MOE_TILE)


def _split_bf16(a):
    hi = a.astype(BF16)
    lo = (a - hi.astype(F32)).astype(BF16)
    return hi, lo


def _dot3(a, b):
    ah, al = _split_bf16(a)
    bh, bl = _split_bf16(b)
    d = functools.partial(jnp.dot, preferred_element_type=F32)
    return d(ah, bh) + (d(ah, bl) + d(al, bh))


def _rms(x, g):
    return x * lax.rsqrt(jnp.mean(x * x, axis=-1, keepdims=True) + EPS) * g


def _rms_mod(x, g, shift, scale):
    return x * lax.rsqrt(jnp.mean(x * x, axis=-1, keepdims=True) + EPS) * (g * (1.0 + scale)) + shift


def _gelu(x):
    return 0.5 * x * (1.0 + lax.erf(x * (2.0 ** -0.5)))


def _layer_norm(x, g, b):
    mu = jnp.mean(x, axis=-1, keepdims=True)
    xc = x - mu
    var = jnp.mean(xc * xc, axis=-1, keepdims=True)
    return xc * lax.rsqrt(var + EPS) * g + b


def _mod_kernel(c_ref, w_ref, b_ref, o_ref):
    c = c_ref[...]
    c_act = c * jax.nn.sigmoid(c)
    o_ref[0] = _dot3(c_act, w_ref[0]) + b_ref[0]


def _ada_mod(c, ada_w, ada_b):
    depth, d, n = ada_w.shape
    bsz = c.shape[0]
    return pl.pallas_call(
        _mod_kernel,
        grid=(depth, n // MOD_TILE),
        in_specs=[
            pl.BlockSpec((bsz, d), lambda l, j: (0, 0)),
            pl.BlockSpec((1, d, MOD_TILE), lambda l, j: (l, 0, j)),
            pl.BlockSpec((1, 1, MOD_TILE), lambda l, j: (l, 0, j)),
        ],
        out_specs=pl.BlockSpec((1, bsz, MOD_TILE), lambda l, j: (l, 0, j)),
        out_shape=jax.ShapeDtypeStruct((depth, bsz, n), F32),
        compiler_params=pltpu.CompilerParams(
            dimension_semantics=("arbitrary", "arbitrary"), vmem_limit_bytes=VMEM_LIMIT),
        name="ada_mod",
    )(c, ada_w, ada_b.reshape(depth, 1, n))


def _route(h2, rw_ref, rb_ref):
    wh, wl = _split_bf16(rw_ref[...])
    lane = lax.broadcasted_iota(I32, wh.shape, 1)
    hh, hl = _split_bf16(h2)
    acc = jnp.dot(hh, jnp.where(lane < N_EXPERTS, wh, wl), preferred_element_type=F32)
    low = jnp.dot(hl, wh, preferred_element_type=F32)
    lane = lax.broadcasted_iota(I32, acc.shape, 1)
    acc_t = (acc + jnp.where(lane < N_EXPERTS, low, 0.0)).T
    logits = acc_t[:N_EXPERTS] + acc_t[N_EXPERTS:2 * N_EXPERTS]
    scores = jax.nn.sigmoid(logits)
    biased = scores + rb_ref[...]
    ng, ne = N_GROUPS, EXPERTS_PER_GROUP
    b = [biased[j * ng:(j + 1) * ng] for j in range(ne)]
    s = [scores[j * ng:(j + 1) * ng] for j in range(ne)]
    gscore = None
    for i in range(ne):
        for j in range(i + 1, ne):
            pair = b[i] + b[j]
            gscore = pair if gscore is None else jnp.maximum(gscore, pair)
    gid = lax.broadcasted_iota(I32, gscore.shape, 0)
    gmax = jnp.max(gscore, axis=0, keepdims=True)
    best = jnp.min(jnp.where(gscore == gmax, gid, ng), axis=0, keepdims=True)
    onb = gid == best
    cb = [jnp.sum(jnp.where(onb, b[j], 0.0), axis=0, keepdims=True) for j in range(ne)]
    cs = [jnp.sum(jnp.where(onb, s[j], 0.0), axis=0, keepdims=True) for j in range(ne)]
    sel = []
    for j in range(ne):
        rank = jnp.zeros_like(best)
        for i in range(ne):
            if i == j:
                continue
            ahead = (cb[i] >= cb[j]) if i < j else (cb[i] > cb[j])
            rank = rank + ahead.astype(I32)
        sel.append(rank < 2)
    wsum = sum(jnp.where(sel[j], cs[j], 0.0) for j in range(ne))
    gates = jnp.concatenate([jnp.where(sel[j], cs[j], 0.0) / wsum for j in range(ne)], axis=0)
    return gates, jnp.where(onb, 1.0, 0.0)


def _sort_matrix(q):
    rows = lax.broadcasted_iota(I32, (SORT_ROWS, q.shape[1]), 0)
    return jnp.where(rows == q, 1.0, 0.0).astype(BF16)


def _run_copies(op, src_of, dst_of, sem, nrows, bases, region_rows):
    if op == "wait":
        total = pl.multiple_of(sum(nrows[1:], nrows[0]), ROW_ALIGN)

        @pl.when(total > 0)
        def _():
            pltpu.make_async_copy(src_of(0, 0, total), dst_of(0, 0, total), sem).wait()
        return
    tile_off = 0
    for g in range(N_GROUPS):
        n = pl.multiple_of(nrows[g], ROW_ALIGN)
        a = pl.multiple_of(tile_off, ROW_ALIGN)
        b = pl.multiple_of(g * region_rows + bases[g], ROW_ALIGN)

        @pl.when(n > 0)
        def _(a=a, b=b, n=n):
            pltpu.make_async_copy(src_of(a, b, n), dst_of(a, b, n), sem).start()

        tile_off = tile_off + nrows[g]


class _ScatterRefs:
    def __init__(self, gffn_ref, rw_ref, rb_ref, hs_ref, q_ref, segb_ref, segn_ref, items_ref,
                 h2buf, rowbuf, tri, zbuf, fill_s, pend_s, ssem, zsem):
        self.__dict__.update(locals())
        self.region_rows = hs_ref.shape[0] // N_GROUPS
        self.dst_of = lambda a, b, size: hs_ref.at[pl.ds(b, size)]


def _sc_init(sc, tile):
    @pl.when(tile == 0)
    def _():
        for g in range(N_GROUPS):
            sc.fill_s[g] = 0
        for i in range(sc.pend_s.shape[0]):
            sc.pend_s[i] = 0
        ts = sc.tri.shape[0]
        k = lax.broadcasted_iota(I32, (ts, ts), 0)
        t = lax.broadcasted_iota(I32, (ts, ts), 1)
        sc.tri[...] = jnp.where(k <= t, 1.0, 0.0).astype(BF16)
        sc.h2buf[...] = jnp.zeros_like(sc.h2buf)


def _sc_pending(sc, op, slot):
    rec = slot * (2 * N_GROUPS)
    src_of = lambda a, b, size: sc.rowbuf.at[slot, pl.ds(a, size)]
    _run_copies(op, src_of, sc.dst_of, sc.ssem.at[slot], [sc.pend_s[rec + g] for g in range(N_GROUPS)],
                [sc.pend_s[rec + N_GROUPS + g] for g in range(N_GROUPS)], sc.region_rows)


def _prenorm(x1, mod, gffn_ref):
    sh2, sc2 = mod[3:4], mod[4:5]
    return _rms_mod(x1, gffn_ref[...], sh2, sc2)


def _sc_route(sc, st):
    st["h2"] = sc.h2buf[...]
    st["gates"], st["onehot"] = _route(st["h2"], sc.rw_ref, sc.rb_ref)


def _sc_rank(sc, st, active):
    onehot = st["onehot"]
    ts = onehot.shape[1]
    oh8 = jnp.concatenate([onehot, jnp.zeros((SUBLANES - N_GROUPS, ts), F32)], axis=0).astype(BF16)
    csum = jnp.dot(oh8, sc.tri[...], preferred_element_type=F32)[:N_GROUPS]
    counts = [csum[g, ts - 1].astype(I32) * active for g in range(N_GROUPS)]
    st["nrows"] = [(c + (ROW_ALIGN - 1)) & ~(ROW_ALIGN - 1) for c in counts]
    q = jnp.zeros((1, ts), F32)
    tile_off = 0
    for g in range(N_GROUPS):
        q = q + onehot[g:g + 1] * (csum[g:g + 1] - 1.0 + tile_off.astype(F32)
                                   if g else csum[g:g + 1] - 1.0)
        tile_off = tile_off + st["nrows"][g]
    st["q"] = q.astype(I32)


def _sc_sort(sc, st, slot):
    gates = st["gates"]
    ts = gates.shape[1]
    hi = gates.astype(BF16).astype(F32)
    mid = (gates - hi).astype(BF16).astype(F32)
    lo = (gates - hi - mid).astype(BF16).astype(F32)
    meta_t = jnp.concatenate(
        [hi, mid, lo, jnp.zeros((LANES - 3 * EXPERTS_PER_GROUP, ts), F32)], axis=0)
    rows = jnp.concatenate([st["h2"].astype(BF16), meta_t.T.astype(BF16)], axis=1)
    sorted_rows = jnp.dot(_sort_matrix(st["q"]), rows, preferred_element_type=F32)
    sc.rowbuf[slot] = sorted_rows.astype(BF16)


def _sc_start(sc, st, t_idx, slot):
    sc.q_ref[t_idx] = st["q"]
    nrows = st["nrows"]
    rec = slot * (2 * N_GROUPS)
    for g in range(N_GROUPS):
        base = sc.fill_s[g]
        sc.pend_s[rec + g] = nrows[g]
        sc.pend_s[rec + N_GROUPS + g] = base
        sc.segn_ref[t_idx * N_GROUPS + g] = nrows[g]
        sc.segb_ref[t_idx * N_GROUPS + g] = base
        sc.fill_s[g] = base + nrows[g]
    _sc_pending(sc, "start", slot)


def _write_moe_items(sc):
    n_items = sc.items_ref.shape[0] // ITEM_FIELDS
    per_region = sc.region_rows // MOE_TILE
    ends = []
    total = 0
    for g in range(N_GROUPS):
        total = total + (sc.fill_s[g] + (MOE_TILE - 1)) // MOE_TILE
        ends.append(total)
    nxt = [None] * N_GROUPS
    follow = -1
    for g in reversed(range(N_GROUPS)):
        nxt[g] = follow
        follow = jnp.where(ends[g] > (ends[g - 1] if g else 0), g, follow)
    last = jnp.maximum(total - 1, 0)
    prev_grp = None
    for i in range(n_items):
        ic = jnp.minimum(i, last)
        grp = sum((ic >= e).astype(I32) for e in ends[:-1])
        start = jnp.where(grp == 0, 0, jnp.where(grp == 1, ends[0], jnp.where(grp == 2, ends[1], ends[2])))
        fresh = jnp.ones((), I32) if prev_grp is None else (grp != prev_grp).astype(I32)
        sc.items_ref[i] = grp * per_region + ic - start
        sc.items_ref[n_items + i] = grp
        sc.items_ref[2 * n_items + i] = (i < total).astype(I32)
        sc.items_ref[3 * n_items + i] = fresh
        sc.items_ref[4 * n_items + i] = jnp.where(
            grp == 0, nxt[0], jnp.where(grp == 1, nxt[1], jnp.where(grp == 2, nxt[2], nxt[3])))
        prev_grp = grp


def _sc_finish(sc):
    _sc_pending(sc, "wait", 0)
    _sc_pending(sc, "wait", 1)
    sc.zbuf[...] = jnp.zeros_like(sc.zbuf)
    for op in ("start", "wait"):
        for g in range(N_GROUPS):
            fill = sc.fill_s[g]
            n = pl.multiple_of((-fill) & (MOE_TILE - 1), ROW_ALIGN)
            b = pl.multiple_of(g * sc.region_rows + fill, ROW_ALIGN)

            @pl.when(n > 0)
            def _(n=n, b=b, op=op):
                cp = pltpu.make_async_copy(sc.zbuf.at[pl.ds(0, n)], sc.hs_ref.at[pl.ds(b, n)], sc.zsem)
                getattr(cp, op)()
    _write_moe_items(sc)


def _gather_prev_ffn(tile, n_tiles, segb_ref, segn_ref, ys_ref, q_ref, ybuf, gsem):
    region_rows = ys_ref.shape[0] // N_GROUPS

    def copies(op, t, slot):
        nrows = [segn_ref[t * N_GROUPS + g] for g in range(N_GROUPS)]
        bases = [segb_ref[t * N_GROUPS + g] for g in range(N_GROUPS)]
        src_of = lambda a, b, size: ys_ref.at[pl.ds(b, size)]
        dst_of = lambda a, b, size: ybuf.at[slot, pl.ds(a, size)]
        _run_copies(op, src_of, dst_of, gsem.at[slot], nrows, bases, region_rows)

    def issue(t, slot):
        ybuf[slot, SEQ_TILE:SORT_ROWS, :] = jnp.zeros((SORT_ROWS - SEQ_TILE, D_MODEL), BF16)
        copies("start", t, slot)

    @pl.when(tile == 0)
    def _():
        issue(tile, 0)

    @pl.when(tile + 1 < n_tiles)
    def _():
        issue(tile + 1, (tile + 1) % 2)

    slot = tile % 2
    copies("wait", tile, slot)
    p = _sort_matrix(q_ref[0])
    return lax.dot_general(p, ybuf[slot], (((0,), (0,)), ((), ())), preferred_element_type=F32)


class _GatherAhead:
    def __init__(self, tile, n_tiles, segb_ref, segn_ref, ys_ref, q_ref, qnext_ref, ybuf, gsem, ynext):
        self.__dict__.update(locals())
        self.region_rows = ys_ref.shape[0] // N_GROUPS
        self.slots = ybuf.shape[0]

    def copies(self, op, t):
        slot = t % self.slots
        nrows = [self.segn_ref[t * N_GROUPS + g] for g in range(N_GROUPS)]
        bases = [self.segb_ref[t * N_GROUPS + g] for g in range(N_GROUPS)]
        src_of = lambda a, b, size: self.ys_ref.at[pl.ds(b, size)]
        dst_of = lambda a, b, size: self.ybuf.at[slot, pl.ds(a, size)]
        _run_copies(op, src_of, dst_of, self.gsem.at[slot], nrows, bases, self.region_rows)

    def issue(self, t):
        self.ybuf[t % self.slots, SEQ_TILE:SORT_ROWS, :] = jnp.zeros((SORT_ROWS - SEQ_TILE, D_MODEL), BF16)
        self.copies("start", t)

    def unsort(self, t, q):
        y = lax.dot_general(_sort_matrix(q), self.ybuf[t % self.slots], (((0,), (0,)), ((), ())),
                            preferred_element_type=F32)
        self.ynext[...] = y.astype(BF16)

    def top(self):
        tile, n_tiles = self.tile, self.n_tiles

        @pl.when(tile == 0)
        def _():
            self.issue(tile)

            @pl.when(n_tiles > 1)
            def _():
                self.issue(tile + 1)

            self.copies("wait", tile)
            self.unsort(tile, self.q_ref[0])

        @pl.when(tile + 2 < n_tiles)
        def _():
            self.issue(tile + 2)

        @pl.when(jnp.logical_and(tile >= 1, tile + 1 < n_tiles))
        def _():
            self.copies("wait", tile + 1)

        @pl.when(jnp.logical_and(tile == 0, n_tiles > 1))
        def _():
            self.copies("wait", tile + 1)

    def current(self):
        return self.ynext[...].astype(F32)

    def prepare_next(self):
        self.unsort(jnp.minimum(self.tile + 1, self.n_tiles - 1), self.qnext_ref[0])


def _load_weight_bf16(w_hbm, w_vmem, stage, sem):
    rows = stage.shape[1]
    n = w_vmem.shape[0] // rows
    copy = lambda c: pltpu.make_async_copy(w_hbm.at[pl.ds(c * rows, rows)], stage.at[c % 2],
                                           sem.at[c % 2])
    copy(0).start()
    for c in range(n):
        if c + 1 < n:
            copy(c + 1).start()
        copy(c).wait()
        w_vmem[c * rows:(c + 1) * rows, :] = stage[c % 2].astype(BF16)


def _weight_scratch(k, n):
    return [pltpu.VMEM((k, n), BF16), pltpu.VMEM((2, W_STAGE_ROWS, n), F32),
            pltpu.SemaphoreType.DMA((2,))]


def _hybrid_kernel(x_ref, mod_ref, win_hbm, wout_hbm, gmix_ref, gffn_ref, lng_ref, lnb_ref, sw_ref,
                   sb_ref, cw_ref, rw_ref, rb_ref,
                   x1_ref, hs_ref, q_ref, segb_ref, segn_ref, items_ref,
                   pbuf, win_ref, win_stage, win_sem, wout_ref, wout_stage, wout_sem,
                   h2buf, rowbuf, tri, zbuf, fill_s, pend_s, ssem, zsem):
    ts = x_ref.shape[1]
    tile = pl.program_id(0) * pl.num_programs(1) + pl.program_id(1)
    n_tiles = pl.num_programs(0) * pl.num_programs(1)
    sc = _ScatterRefs(gffn_ref, rw_ref, rb_ref, hs_ref, q_ref, segb_ref, segn_ref, items_ref,
                      h2buf, rowbuf, tri, zbuf, fill_s, pend_s, ssem, zsem)

    @pl.when(tile == 0)
    def _():
        _load_weight_bf16(win_hbm.at[0], win_ref, win_stage, win_sem)
        _load_weight_bf16(wout_hbm.at[0], wout_ref, wout_stage, wout_sem)

    @pl.when(pl.program_id(1) == 0)
    def _():
        pbuf[0:B_HALO, :] = jnp.zeros((B_HALO, D_B), F32)

    _sc_init(sc, tile)
    slot = tile % 2
    _sc_pending(sc, "wait", slot)

    prev = {}
    active = (tile > 0).astype(I32)

    _sc_route(sc, prev)
    _sc_rank(sc, prev, active)
    _sc_sort(sc, prev, slot)

    x = x_ref[0]
    mod = mod_ref[0, 0]
    sh1, sc1, g1 = mod[0:1], mod[1:2], mod[2:3]
    h = _rms_mod(x, gmix_ref[...], sh1, sc1)
    hb16 = h.astype(BF16)
    zcol = lambda j: jnp.dot(hb16, win_ref[:, j * D_A:(j + 1) * D_A], preferred_element_type=F32)

    v = _layer_norm(_gelu(zcol(1)), lng_ref[...], lnb_ref[...]).astype(BF16)
    u = _gelu(zcol(0))
    tril = (lax.broadcasted_iota(I32, (CHUNK, CHUNK), 0)
            >= lax.broadcasted_iota(I32, (CHUNK, CHUNK), 1))
    chunks = []
    for n in range(ts // CHUNK):
        heads = []
        for hd in range(A_HEADS):
            w = jnp.where(tril, sw_ref[hd], 0.0).astype(BF16)
            vb = v[n * CHUNK:(n + 1) * CHUNK, hd * A_HEAD_DIM:(hd + 1) * A_HEAD_DIM]
            heads.append(jnp.dot(w, vb, preferred_element_type=F32) + sb_ref[hd])
        chunks.append(jnp.concatenate(heads, axis=1))
    y_a = u * jnp.concatenate(chunks, axis=0)

    p = zcol(3) * zcol(4)
    pbuf[B_HALO:B_HALO + ts, :] = p
    cw = cw_ref[...]
    conv = (cw[0:1] * pbuf[B_HALO - 2:B_HALO - 2 + ts, :]
            + cw[1:2] * pbuf[B_HALO - 1:B_HALO - 1 + ts, :]
            + cw[2:3] * p)
    pbuf[0:B_HALO, :] = p[ts - B_HALO:ts, :]
    y_b = zcol(2) * conv

    ycat = jnp.concatenate([y_a, y_b], axis=1).astype(BF16)
    y = jnp.dot(ycat, wout_ref[...], preferred_element_type=F32)
    x1 = x + g1 * y
    x1_ref[0] = x1
    h2buf[...] = _prenorm(x1, mod, gffn_ref)
    _sc_start(sc, prev, jnp.maximum(tile - 1, 0), slot)

    @pl.when(tile == n_tiles - 1)
    def _():
        _sc_pending(sc, "wait", 1 - slot)
        last = {}
        _sc_route(sc, last)
        _sc_rank(sc, last, 1)
        _sc_sort(sc, last, 1 - slot)
        _sc_start(sc, last, tile, 1 - slot)
        _sc_finish(sc)


def _band_matrix():
    shape = (CONV_ROWS, SUBLANES * S_ROWS)
    t = lax.broadcasted_iota(I32, shape, 0)
    c = lax.broadcasted_iota(I32, shape, 1)
    hit = c == t + S_LEAD
    for r in range(1, SUBLANES):
        hit = hit | (c == t + S_LEAD + r * (S_ROWS - 1))
    return jnp.where(hit, 1.0, 0.0).astype(BF16)


def _causal_dw_conv(cbuf, w8, obuf, band, sbuf, row0, nrows, col0, ncols, first_slot):
    n_a = -(-C_CONV // SUBLANES)
    cols = slice(col0, col0 + ncols)
    for i, t0 in enumerate(range(row0, row0 + nrows, CONV_ROWS)):
        stack = sbuf.at[(first_slot + i) % sbuf.shape[0]]
        for r in range(SUBLANES):
            part = None
            for a in range(n_a):
                k = C_CONV - 1 - SUBLANES * a - r
                if k < 0:
                    continue
                r0 = t0 + C_HALO - S_LEAD - SUBLANES * a
                z3 = cbuf[r0:r0 + S_ROWS, cols].reshape(S_ROWS // SUBLANES, SUBLANES, ncols)
                term = (z3 * w8[SUBLANES * k:SUBLANES * (k + 1), cols][None]).reshape(S_ROWS, ncols)
                part = term if part is None else part + term
            stack[r * S_ROWS:(r + 1) * S_ROWS, :] = part.astype(BF16)
        obuf[t0:t0 + CONV_ROWS, cols] = jnp.dot(band[...], stack[...], preferred_element_type=F32)


def _conformer_kernel(psegb_ref, psegn_ref,
                      xp_ref, ys_ref, pq_ref, pmod_ref, pqn_ref, mod_ref, w1_hbm, w2_hbm, gmix_ref, gffn_ref,
                      b1_ref, dww_ref, dwb_ref, lng_ref, lnb_ref, b2_ref, rw_ref, rb_ref,
                      x1_ref, hs_ref, q_ref, segb_ref, segn_ref, items_ref,
                      cbuf, obuf, band, sbuf, w8, ynext, ybuf, gsem,
                      w1_ref, w1_stage, w1_sem, w2_ref, w2_stage, w2_sem,
                      h2buf, rowbuf, tri, zbuf, fill_s, pend_s, ssem, zsem):
    ts = xp_ref.shape[1]
    tile = pl.program_id(0) * pl.num_programs(1) + pl.program_id(1)
    n_tiles = pl.num_programs(0) * pl.num_programs(1)
    sc = _ScatterRefs(gffn_ref, rw_ref, rb_ref, hs_ref, q_ref, segb_ref, segn_ref, items_ref,
                      h2buf, rowbuf, tri, zbuf, fill_s, pend_s, ssem, zsem)

    @pl.when(tile == 0)
    def _():
        _load_weight_bf16(w1_hbm.at[0], w1_ref, w1_stage, w1_sem)
        _load_weight_bf16(w2_hbm.at[0], w2_ref, w2_stage, w2_sem)
        band[...] = _band_matrix()
        for k in range(C_CONV):
            w8[SUBLANES * k:SUBLANES * (k + 1), :] = jnp.broadcast_to(dww_ref[k:k + 1, :], (SUBLANES, D_C))

    @pl.when(pl.program_id(1) == 0)
    def _():
        cbuf[0:C_HALO, :] = jnp.zeros((C_HALO, D_C), F32)

    _sc_init(sc, tile)
    slot = tile % 2
    _sc_pending(sc, "wait", slot)
    ga = _GatherAhead(tile, n_tiles, psegb_ref, psegn_ref, ys_ref, pq_ref, pqn_ref, ybuf, gsem, ynext)
    ga.top()
    y_prev = ga.current()

    prev = {}
    active = (tile > 0).astype(I32)

    x = xp_ref[0] + pmod_ref[0, 0][5:6] * y_prev
    mod = mod_ref[0, 0]
    sh1, sc1, g1 = mod[0:1], mod[1:2], mod[2:3]
    h = _rms_mod(x, gmix_ref[...], sh1, sc1)
    hb = h.astype(BF16)
    def glu(c0):
        ca, cg = slice(c0, c0 + GLU_COLS), slice(D_C + c0, D_C + c0 + GLU_COLS)
        za = jnp.dot(hb, w1_ref[:, ca], preferred_element_type=F32) + b1_ref[:, ca]
        zg = jnp.dot(hb, w1_ref[:, cg], preferred_element_type=F32) + b1_ref[:, cg]
        cbuf[C_HALO:C_HALO + ts, ca] = za * jax.nn.sigmoid(zg)

    for c0 in range(0, D_C, GLU_COLS):
        glu(c0)
    _sc_route(sc, prev)
    _sc_rank(sc, prev, active)
    _sc_sort(sc, prev, slot)
    for k, c0 in enumerate(range(0, D_C, GLU_COLS)):
        _causal_dw_conv(cbuf, w8, obuf, band, sbuf, 0, ts, c0, GLU_COLS,
                        k * (ts // CONV_ROWS))
        if k == 1:
            ga.prepare_next()
    cbuf[0:C_HALO, :] = cbuf[ts:ts + C_HALO, :]

    zc = _layer_norm(obuf[...] + dwb_ref[...], lng_ref[...], lnb_ref[...])
    zc = zc * jax.nn.sigmoid(zc)
    y = jnp.dot(zc.astype(BF16), w2_ref[...], preferred_element_type=F32) + b2_ref[...]
    x1 = x + g1 * y
    x1_ref[0] = x1
    h2buf[...] = _prenorm(x1, mod, gffn_ref)
    _sc_start(sc, prev, jnp.maximum(tile - 1, 0), slot)

    @pl.when(tile == n_tiles - 1)
    def _():
        _sc_pending(sc, "wait", 1 - slot)
        last = {}
        _sc_route(sc, last)
        _sc_rank(sc, last, 1)
        _sc_sort(sc, last, 1 - slot)
        _sc_start(sc, last, tile, 1 - slot)
        _sc_finish(sc)


def _final_kernel(psegb_ref, psegn_ref, xp_ref, ys_ref, pq_ref, pmod_ref, fg_ref, o_ref, ybuf, gsem):
    tile = pl.program_id(0) * pl.num_programs(1) + pl.program_id(1)
    n_tiles = pl.num_programs(0) * pl.num_programs(1)
    y_prev = _gather_prev_ffn(tile, n_tiles, psegb_ref, psegn_ref, ys_ref, pq_ref, ybuf, gsem)
    x = xp_ref[0] + pmod_ref[0, 0][5:6] * y_prev
    o_ref[0] = _rms(x, fg_ref[...])


def _scatter_outputs(bsz, seq, d):
    t = bsz * seq
    n_tiles = t // SEQ_TILE
    smem = pl.BlockSpec(memory_space=pltpu.SMEM)
    specs = [pl.BlockSpec((1, SEQ_TILE, d), lambda b, s, *_: (b, s, 0)),
             pl.BlockSpec(memory_space=pl.ANY),
             pl.BlockSpec((n_tiles, 1, SEQ_TILE), lambda b, s, *_: (0, 0, 0)),
             smem, smem, smem]
    shapes = [jax.ShapeDtypeStruct((bsz, seq, d), F32),
              jax.ShapeDtypeStruct((N_GROUPS * _region_rows(t), ROW_W), BF16),
              jax.ShapeDtypeStruct((n_tiles, 1, SEQ_TILE), I32),
              jax.ShapeDtypeStruct((n_tiles * N_GROUPS,), I32),
              jax.ShapeDtypeStruct((n_tiles * N_GROUPS,), I32),
              jax.ShapeDtypeStruct((ITEM_FIELDS * _moe_steps(t),), I32)]
    scratch = [pltpu.VMEM((SEQ_TILE, d), F32),
               pltpu.VMEM((2, SORT_ROWS, ROW_W), BF16),
               pltpu.VMEM((SEQ_TILE, SEQ_TILE), BF16),
               pltpu.VMEM((MOE_TILE, ROW_W), BF16),
               pltpu.SMEM((N_GROUPS,), I32),
               pltpu.SMEM((4 * N_GROUPS,), I32),
               pltpu.SemaphoreType.DMA((2,)),
               pltpu.SemaphoreType.DMA(())]
    return specs, shapes, scratch


def _mod_spec(d, layer):
    return pl.BlockSpec((1, 1, N_MOD, d), lambda b, s, *_: (layer, b, 0, 0))


def _gather_inputs(bsz, seq, d, prev_layer):
    nseq = seq // SEQ_TILE
    specs = [pl.BlockSpec((1, SEQ_TILE, d), lambda b, s, *_: (b, s, 0)),
             pl.BlockSpec(memory_space=pl.ANY),
             pl.BlockSpec((1, 1, SEQ_TILE), lambda b, s, *_: (b * nseq + s, 0, 0)),
             _mod_spec(d, prev_layer)]
    scratch = [pltpu.VMEM((2, SORT_ROWS, d), BF16), pltpu.SemaphoreType.DMA((2,))]
    return specs, scratch


def _const_specs(consts):
    return [pl.BlockSpec(a.shape, lambda b, s, *_, nd=a.ndim: (0,) * nd) for a in consts]


_PARAMS = pltpu.CompilerParams(dimension_semantics=("arbitrary", "arbitrary"),
                               vmem_limit_bytes=VMEM_LIMIT)


def _hybrid_call(x, mod, layer, w_in, w_out, consts):
    bsz, seq, d = x.shape
    out_specs, out_shapes, sc_scratch = _scatter_outputs(bsz, seq, d)
    hbm = pl.BlockSpec(memory_space=pl.ANY)
    return pl.pallas_call(
        _hybrid_kernel,
        grid=(bsz, seq // SEQ_TILE),
        in_specs=[pl.BlockSpec((1, SEQ_TILE, d), lambda b, s: (b, s, 0)),
                  _mod_spec(d, layer), hbm, hbm]
                 + _const_specs(consts),
        out_specs=out_specs,
        out_shape=out_shapes,
        scratch_shapes=[pltpu.VMEM((SEQ_TILE + B_HALO, D_B), F32)]
                       + _weight_scratch(*w_in.shape[1:]) + _weight_scratch(*w_out.shape[1:])
                       + sc_scratch,
        compiler_params=_PARAMS,
        name="mixer_hybrid",
    )(x, mod, w_in, w_out, *consts)


def _conformer_call(prev, mod, layer, w1, w2, consts):
    x1p, ysp, qp, segbp, segnp = prev
    bsz, seq, d = x1p.shape
    g_specs, _ = _gather_inputs(bsz, seq, d, layer - 1)
    nseq = seq // SEQ_TILE
    last = bsz * nseq - 1
    g_specs = g_specs + [pl.BlockSpec((1, 1, SEQ_TILE),
                                      lambda b, s, *_: (jnp.minimum(b * nseq + s + 1, last), 0, 0))]
    g_scratch = [pltpu.VMEM((SEQ_TILE, d), BF16), pltpu.VMEM((3, SORT_ROWS, d), BF16),
                 pltpu.SemaphoreType.DMA((3,))]
    out_specs, out_shapes, sc_scratch = _scatter_outputs(bsz, seq, d)
    hbm = pl.BlockSpec(memory_space=pl.ANY)
    grid_spec = pltpu.PrefetchScalarGridSpec(
        num_scalar_prefetch=2,
        grid=(bsz, seq // SEQ_TILE),
        in_specs=g_specs + [_mod_spec(d, layer), hbm, hbm]
                 + _const_specs(consts),
        out_specs=out_specs,
        scratch_shapes=[pltpu.VMEM((SEQ_TILE + C_HALO, D_C), F32), pltpu.VMEM((SEQ_TILE, D_C), F32),
                        pltpu.VMEM((CONV_ROWS, SUBLANES * S_ROWS), BF16),
                        pltpu.VMEM((S_SLOTS, SUBLANES * S_ROWS, GLU_COLS), BF16),
                        pltpu.VMEM((C_CONV * SUBLANES, D_C), F32)]
                       + g_scratch + _weight_scratch(*w1.shape[1:]) + _weight_scratch(*w2.shape[1:])
                       + sc_scratch,
    )
    return pl.pallas_call(
        _conformer_kernel, grid_spec=grid_spec, out_shape=out_shapes,
        compiler_params=_PARAMS, name="mixer_conformer",
    )(segbp, segnp, x1p, ysp, qp, mod, qp, mod, w1, w2, *consts)


def _final_call(prev, mod, prev_layer, final_g):
    x1p, ysp, qp, segbp, segnp = prev
    bsz, seq, d = x1p.shape
    g_specs, g_scratch = _gather_inputs(bsz, seq, d, prev_layer)
    grid_spec = pltpu.PrefetchScalarGridSpec(
        num_scalar_prefetch=2,
        grid=(bsz, seq // SEQ_TILE),
        in_specs=g_specs + [pl.BlockSpec((1, d), lambda b, s, *_: (0, 0))],
        out_specs=pl.BlockSpec((1, SEQ_TILE, d), lambda b, s, *_: (b, s, 0)),
        scratch_shapes=g_scratch,
    )
    return pl.pallas_call(
        _final_kernel, grid_spec=grid_spec,
        out_shape=jax.ShapeDtypeStruct((bsz, seq, d), F32),
        compiler_params=_PARAMS, name="final_norm",
    )(segbp, segnp, x1p, ysp, qp, mod, final_g.reshape(1, d))


def _moe_kernel(items_ref, hs_ref, wg_hbm, wu_hbm, wd_hbm, ys_ref,
                wg_st, wu_st, wd_st, wsem, wg_bf, wu_bf, wd_bf, slot_s, *, layer):
    i = pl.program_id(0)
    n_items = pl.num_programs(0)
    grp = items_ref[n_items + i]

    def fetch(g, slot):
        src = layer * N_GROUPS + g
        return [pltpu.make_async_copy(w.at[src], st.at[slot], wsem.at[slot, k])
                for k, (w, st) in enumerate(((wg_hbm, wg_st), (wu_hbm, wu_st), (wd_hbm, wd_st)))]

    @pl.when(i == 0)
    def _():
        slot_s[0] = 0
        for cp in fetch(grp, 0):
            cp.start()

    @pl.when(items_ref[3 * n_items + i] != 0)
    def _():
        slot = slot_s[0]
        for cp in fetch(grp, slot):
            cp.wait()
        wg_bf[...] = wg_st[slot].astype(BF16)
        wu_bf[...] = wu_st[slot].astype(BF16)
        wd_bf[...] = wd_st[slot].astype(BF16)
        nxt = items_ref[4 * n_items + i]

        @pl.when(nxt >= 0)
        def _():
            for cp in fetch(nxt, 1 - slot):
                cp.start()

        slot_s[0] = 1 - slot

    @pl.when(items_ref[2 * n_items + i] != 0)
    def _():
        rows = hs_ref[...]
        h = rows[:, :D_MODEL]
        m = rows[:, D_MODEL:].astype(F32)
        ne = EXPERTS_PER_GROUP
        y = None
        for j in range(ne):
            gate = m[:, j:j + 1] + (m[:, ne + j:ne + j + 1] + m[:, 2 * ne + j:2 * ne + j + 1])
            a = jnp.dot(h, wg_bf[j], preferred_element_type=F32)
            a = a * jax.nn.sigmoid(a) * jnp.dot(h, wu_bf[j], preferred_element_type=F32)
            yj = jnp.dot((a * gate).astype(BF16), wd_bf[j], preferred_element_type=F32)
            y = yj if y is None else y + yj
        ys_ref[...] = y.astype(BF16)


def _moe_call(hs, items, layer, wg, wu, wd):
    d = D_MODEL
    ne = EXPERTS_PER_GROUP
    n_items = items.shape[0] // ITEM_FIELDS
    row_idx = lambda i, items: (items[i], 0)
    hbm = pl.BlockSpec(memory_space=pl.ANY)
    grid_spec = pltpu.PrefetchScalarGridSpec(
        num_scalar_prefetch=1,
        grid=(n_items,),
        in_specs=[pl.BlockSpec((MOE_TILE, ROW_W), row_idx), hbm, hbm, hbm],
        out_specs=pl.BlockSpec((MOE_TILE, d), row_idx),
        scratch_shapes=[pltpu.VMEM((2, ne, d, D_EXPERT), F32), pltpu.VMEM((2, ne, d, D_EXPERT), F32),
                        pltpu.VMEM((2, ne, D_EXPERT, d), F32), pltpu.SemaphoreType.DMA((2, 3)),
                        pltpu.VMEM((ne, d, D_EXPERT), BF16), pltpu.VMEM((ne, d, D_EXPERT), BF16),
                        pltpu.VMEM((ne, D_EXPERT, d), BF16), pltpu.SMEM((1,), I32)],
    )
    return pl.pallas_call(
        functools.partial(_moe_kernel, layer=layer), grid_spec=grid_spec,
        out_shape=jax.ShapeDtypeStruct((hs.shape[0], d), BF16),
        compiler_params=pltpu.CompilerParams(dimension_semantics=("arbitrary",),
                                             vmem_limit_bytes=VMEM_LIMIT),
        name="moe",
    )(items, hs, wg, wu, wd)


def kernel(x, c, ada_w, ada_b, norm_mix_g, norm_ffn_g, hyb_w_in, sgu_ln_g, sgu_ln_b, sgu_w, sgu_b,
           sconv_w, hyb_w_out, conf_w1, conf_b1, conf_dw_w, conf_dw_b, conf_ln_g, conf_ln_b,
           conf_w2, conf_b2, router_w, router_bias, moe_w_gate, moe_w_up, moe_w_down, final_norm_g):
    depth = ada_w.shape[0]
    bsz, seq, d = x.shape
    assert depth == 2 and seq % SEQ_TILE == 0
    mod = _ada_mod(c, ada_w, ada_b).reshape(depth, bsz, N_MOD, d)

    perm = jnp.arange(N_EXPERTS).reshape(N_GROUPS, EXPERTS_PER_GROUP).T.reshape(-1)
    rw_p = router_w[:, perm]
    rw_t = jnp.concatenate([rw_p, rw_p, jnp.zeros((d, LANES - 2 * N_EXPERTS), F32)], axis=1)
    rb_t = router_bias[perm].reshape(N_EXPERTS, 1)

    ne = EXPERTS_PER_GROUP
    wg = moe_w_gate.reshape(depth * N_GROUPS, ne, d, D_EXPERT)
    wu = moe_w_up.reshape(depth * N_GROUPS, ne, d, D_EXPERT)
    wd = moe_w_down.reshape(depth * N_GROUPS, ne, D_EXPERT, d)

    row = lambda a: a.reshape(1, -1)
    consts0 = [row(norm_mix_g[0]), row(norm_ffn_g[0]), row(sgu_ln_g[0]), row(sgu_ln_b[0]), sgu_w[0],
               sgu_b[0].reshape(A_HEADS, CHUNK, 1), sconv_w[0], rw_t, rb_t]
    x1, hs, q, segb, segn, items = _hybrid_call(x, mod, 0, hyb_w_in, hyb_w_out, consts0)
    ys = _moe_call(hs, items, 0, wg, wu, wd)

    consts1 = [row(norm_mix_g[1]), row(norm_ffn_g[1]), row(conf_b1[0]), conf_dw_w[0],
               row(conf_dw_b[0]), row(conf_ln_g[0]), row(conf_ln_b[0]), row(conf_b2[0]), rw_t, rb_t]
    x1, hs, q1, segb1, segn1, items = _conformer_call((x1, ys, q, segb, segn), mod, 1,
                                                      conf_w1, conf_w2, consts1)
    ys = _moe_call(hs, items, 1, wg, wu, wd)
    return _final_call((x1, ys, q1, segb1, segn1), mod, 1, final_norm_g)
```

```python
import functools

import jax
import jax.numpy as jnp
from jax import lax
from jax.experimental import pallas as pl
from jax.experimental.pallas import tpu as pltpu

D_MODEL = 1024
CHUNK = 128
A_HEADS = 4
A_HEAD_DIM = 128
D_A = A_HEADS * A_HEAD_DIM
D_B = 512
B_CONV = 3
D_IN_HYB = 2 * D_A + 3 * D_B
D_C = D_MODEL
C_CONV = 31
N_EXPERTS = 16
N_GROUPS = 4
EXPERTS_PER_GROUP = N_EXPERTS // N_GROUPS
D_EXPERT = 256
N_MOD = 6
EPS = 1e-6

SUBLANES = 8
LANES = 128
ROW_ALIGN = 16

SEQ_TILE = 512
MOE_TILE = 512
ITEM_FIELDS = 5
MOD_TILE = 2048
W_STAGE_ROWS = 128
C_HALO = 40
B_HALO = 8
CONV_ROWS = 128
GLU_COLS = 256
S_LEAD = ROW_ALIGN
S_ROWS = CONV_ROWS + S_LEAD
S_SLOTS = 2
SORT_ROWS = SEQ_TILE + N_GROUPS * ROW_ALIGN
ROW_W = D_MODEL + LANES
VMEM_LIMIT = 56 * 1024 * 1024

F32 = jnp.float32
BF16 = jnp.bfloat16
I32 = jnp.int32


def _round_up(n, m):
    return -(-n // m) * m


def _moe_steps(t):
    return -(-(t + (t // SEQ_TILE) * N_GROUPS * (ROW_ALIGN - 1)) // MOE_TILE) + N_GROUPS


def _region_rows(t):
    return _round_up(t + (t // SEQ_TILE) * ROW_ALIGN, MOE_TILE)


def _split_bf16(a):
    hi = a.astype(BF16)
    lo = (a - hi.astype(F32)).astype(BF16)
    return hi, lo


def _dot3(a, b):
    ah, al = _split_bf16(a)
    bh, bl = _split_bf16(b)
    d = functools.partial(jnp.dot, preferred_element_type=F32)
    return d(ah, bh) + (d(ah, bl) + d(al, bh))


def _rms(x, g):
    return x * lax.rsqrt(jnp.mean(x * x, axis=-1, keepdims=True) + EPS) * g


def _rms_mod(x, g, shift, scale):
    return x * lax.rsqrt(jnp.mean(x * x, axis=-1, keepdims=True) + EPS) * (g * (1.0 + scale)) + shift


def _gelu(x):
    return 0.5 * x * (1.0 + lax.erf(x * (2.0 ** -0.5)))


def _layer_norm(x, g, b):
    mu = jnp.mean(x, axis=-1, keepdims=True)
    xc = x - mu
    var = jnp.mean(xc * xc, axis=-1, keepdims=True)
    return xc * lax.rsqrt(var + EPS) * g + b


def _mod_kernel(c_ref, w_ref, b_ref, o_ref):
    c = c_ref[...]
    c_act = c * jax.nn.sigmoid(c)
    o_ref[0] = _dot3(c_act, w_ref[0]) + b_ref[0]


def _ada_mod(c, ada_w, ada_b):
    depth, d, n = ada_w.shape
    bsz = c.shape[0]
    return pl.pallas_call(
        _mod_kernel,
        grid=(depth, n // MOD_TILE),
        in_specs=[
            pl.BlockSpec((bsz, d), lambda l, j: (0, 0)),
            pl.BlockSpec((1, d, MOD_TILE), lambda l, j: (l, 0, j)),
            pl.BlockSpec((1, 1, MOD_TILE), lambda l, j: (l, 0, j)),
        ],
        out_specs=pl.BlockSpec((1, bsz, MOD_TILE), lambda l, j: (l, 0, j)),
        out_shape=jax.ShapeDtypeStruct((depth, bsz, n), F32),
        compiler_params=pltpu.CompilerParams(
            dimension_semantics=("arbitrary", "arbitrary"), vmem_limit_bytes=VMEM_LIMIT),
        name="ada_mod",
    )(c, ada_w, ada_b.reshape(depth, 1, n))


def _route(h2, rw_ref, rb_ref):
    wh, wl = _split_bf16(rw_ref[...])
    lane = lax.broadcasted_iota(I32, wh.shape, 1)
    hh, hl = _split_bf16(h2)
    acc = jnp.dot(hh, jnp.where(lane < N_EXPERTS, wh, wl), preferred_element_type=F32)
    low = jnp.dot(hl, wh, preferred_element_type=F32)
    lane = lax.broadcasted_iota(I32, acc.shape, 1)
    acc_t = (acc + jnp.where(lane < N_EXPERTS, low, 0.0)).T
    logits = acc_t[:N_EXPERTS] + acc_t[N_EXPERTS:2 * N_EXPERTS]
    scores = jax.nn.sigmoid(logits)
    biased = scores + rb_ref[...]
    ng, ne = N_GROUPS, EXPERTS_PER_GROUP
    b = [biased[j * ng:(j + 1) * ng] for j in range(ne)]
    s = [scores[j * ng:(j + 1) * ng] for j in range(ne)]
    gscore = None
    for i in range(ne):
        for j in range(i + 1, ne):
            pair = b[i] + b[j]
            gscore = pair if gscore is None else jnp.maximum(gscore, pair)
    gid = lax.broadcasted_iota(I32, gscore.shape, 0)
    gmax = jnp.max(gscore, axis=0, keepdims=True)
    best = jnp.min(jnp.where(gscore == gmax, gid, ng), axis=0, keepdims=True)
    onb = gid == best
    cb = [jnp.sum(jnp.where(onb, b[j], 0.0), axis=0, keepdims=True) for j in range(ne)]
    cs = [jnp.sum(jnp.where(onb, s[j], 0.0), axis=0, keepdims=True) for j in range(ne)]
    sel = []
    for j in range(ne):
        rank = jnp.zeros_like(best)
        for i in range(ne):
            if i == j:
                continue
            ahead = (cb[i] >= cb[j]) if i < j else (cb[i] > cb[j])
            rank = rank + ahead.astype(I32)
        sel.append(rank < 2)
    wsum = sum(jnp.where(sel[j], cs[j], 0.0) for j in range(ne))
    gates = jnp.concatenate([jnp.where(sel[j], cs[j], 0.0) / wsum for j in range(ne)], axis=0)
    return gates, jnp.where(onb, 1.0, 0.0)


def _sort_matrix(q):
    rows = lax.broadcasted_iota(I32, (SORT_ROWS, q.shape[1]), 0)
    return jnp.where(rows == q, 1.0, 0.0).astype(BF16)


def _run_copies(op, src_of, dst_of, sem, nrows, bases, region_rows):
    if op == "wait":
        total = pl.multiple_of(sum(nrows[1:], nrows[0]), ROW_ALIGN)

        @pl.when(total > 0)
        def _():
            pltpu.make_async_copy(src_of(0, 0, total), dst_of(0, 0, total), sem).wait()
        return
    tile_off = 0
    for g in range(N_GROUPS):
        n = pl.multiple_of(nrows[g], ROW_ALIGN)
        a = pl.multiple_of(tile_off, ROW_ALIGN)
        b = pl.multiple_of(g * region_rows + bases[g], ROW_ALIGN)

        @pl.when(n > 0)
        def _(a=a, b=b, n=n):
            pltpu.make_async_copy(src_of(a, b, n), dst_of(a, b, n), sem).start()

        tile_off = tile_off + nrows[g]


class _ScatterRefs:
    def __init__(self, gffn_ref, rw_ref, rb_ref, hs_ref, q_ref, segb_ref, segn_ref, items_ref,
                 h2buf, rowbuf, tri, zbuf, fill_s, pend_s, ssem, zsem):
        self.__dict__.update(locals())
        self.region_rows = hs_ref.shape[0] // N_GROUPS
        self.dst_of = lambda a, b, size: hs_ref.at[pl.ds(b, size)]


def _sc_init(sc, tile):
    @pl.when(tile == 0)
    def _():
        for g in range(N_GROUPS):
            sc.fill_s[g] = 0
        for i in range(sc.pend_s.shape[0]):
            sc.pend_s[i] = 0
        ts = sc.tri.shape[0]
        k = lax.broadcasted_iota(I32, (ts, ts), 0)
        t = lax.broadcasted_iota(I32, (ts, ts), 1)
        sc.tri[...] = jnp.where(k <= t, 1.0, 0.0).astype(BF16)
        sc.h2buf[...] = jnp.zeros_like(sc.h2buf)


def _sc_pending(sc, op, slot):
    rec = slot * (2 * N_GROUPS)
    src_of = lambda a, b, size: sc.rowbuf.at[slot, pl.ds(a, size)]
    _run_copies(op, src_of, sc.dst_of, sc.ssem.at[slot], [sc.pend_s[rec + g] for g in range(N_GROUPS)],
                [sc.pend_s[rec + N_GROUPS + g] for g in range(N_GROUPS)], sc.region_rows)


def _prenorm(x1, mod, gffn_ref):
    sh2, sc2 = mod[3:4], mod[4:5]
    return _rms_mod(x1, gffn_ref[...], sh2, sc2)


def _sc_route(sc, st):
    st["h2"] = sc.h2buf[...]
    st["gates"], st["onehot"] = _route(st["h2"], sc.rw_ref, sc.rb_ref)


def _sc_rank(sc, st, active):
    onehot = st["onehot"]
    ts = onehot.shape[1]
    oh8 = jnp.concatenate([onehot, jnp.zeros((SUBLANES - N_GROUPS, ts), F32)], axis=0).astype(BF16)
    csum = jnp.dot(oh8, sc.tri[...], preferred_element_type=F32)[:N_GROUPS]
    counts = [csum[g, ts - 1].astype(I32) * active for g in range(N_GROUPS)]
    st["nrows"] = [(c + (ROW_ALIGN - 1)) & ~(ROW_ALIGN - 1) for c in counts]
    q = jnp.zeros((1, ts), F32)
    tile_off = 0
    for g in range(N_GROUPS):
        q = q + onehot[g:g + 1] * (csum[g:g + 1] - 1.0 + tile_off.astype(F32)
                                   if g else csum[g:g + 1] - 1.0)
        tile_off = tile_off + st["nrows"][g]
    st["q"] = q.astype(I32)


def _sc_sort(sc, st, slot):
    gates = st["gates"]
    ts = gates.shape[1]
    hi = gates.astype(BF16).astype(F32)
    mid = (gates - hi).astype(BF16).astype(F32)
    lo = (gates - hi - mid).astype(BF16).astype(F32)
    meta_t = jnp.concatenate(
        [hi, mid, lo, jnp.zeros((LANES - 3 * EXPERTS_PER_GROUP, ts), F32)], axis=0)
    rows = jnp.concatenate([st["h2"].astype(BF16), meta_t.T.astype(BF16)], axis=1)
    sorted_rows = jnp.dot(_sort_matrix(st["q"]), rows, preferred_element_type=F32)
    sc.rowbuf[slot] = sorted_rows.astype(BF16)


def _sc_start(sc, st, t_idx, slot):
    sc.q_ref[t_idx] = st["q"]
    nrows = st["nrows"]
    rec = slot * (2 * N_GROUPS)
    for g in range(N_GROUPS):
        base = sc.fill_s[g]
        sc.pend_s[rec + g] = nrows[g]
        sc.pend_s[rec + N_GROUPS + g] = base
        sc.segn_ref[t_idx * N_GROUPS + g] = nrows[g]
        sc.segb_ref[t_idx * N_GROUPS + g] = base
        sc.fill_s[g] = base + nrows[g]
    _sc_pending(sc, "start", slot)


def _write_moe_items(sc):
    n_items = sc.items_ref.shape[0] // ITEM_FIELDS
    per_region = sc.region_rows // MOE_TILE
    ends = []
    total = 0
    for g in range(N_GROUPS):
        total = total + (sc.fill_s[g] + (MOE_TILE - 1)) // MOE_TILE
        ends.append(total)
    nxt = [None] * N_GROUPS
    follow = -1
    for g in reversed(range(N_GROUPS)):
        nxt[g] = follow
        follow = jnp.where(ends[g] > (ends[g - 1] if g else 0), g, follow)
    last = jnp.maximum(total - 1, 0)
    prev_grp = None
    for i in range(n_items):
        ic = jnp.minimum(i, last)
        grp = sum((ic >= e).astype(I32) for e in ends[:-1])
        start = jnp.where(grp == 0, 0, jnp.where(grp == 1, ends[0], jnp.where(grp == 2, ends[1], ends[2])))
        fresh = jnp.ones((), I32) if prev_grp is None else (grp != prev_grp).astype(I32)
        sc.items_ref[i] = grp * per_region + ic - start
        sc.items_ref[n_items + i] = grp
        sc.items_ref[2 * n_items + i] = (i < total).astype(I32)
        sc.items_ref[3 * n_items + i] = fresh
        sc.items_ref[4 * n_items + i] = jnp.where(
            grp == 0, nxt[0], jnp.where(grp == 1, nxt[1], jnp.where(grp == 2, nxt[2], nxt[3])))
        prev_grp = grp


def _sc_finish(sc):
    _sc_pending(sc, "wait", 0)
    _sc_pending(sc, "wait", 1)
    sc.zbuf[...] = jnp.zeros_like(sc.zbuf)
    for op in ("start", "wait"):
        for g in range(N_GROUPS):
            fill = sc.fill_s[g]
            n = pl.multiple_of((-fill) & (MOE_TILE - 1), ROW_ALIGN)
            b = pl.multiple_of(g * sc.region_rows + fill, ROW_ALIGN)

            @pl.when(n > 0)
            def _(n=n, b=b, op=op):
                cp = pltpu.make_async_copy(sc.zbuf.at[pl.ds(0, n)], sc.hs_ref.at[pl.ds(b, n)], sc.zsem)
                getattr(cp, op)()
    _write_moe_items(sc)


def _gather_prev_ffn(tile, n_tiles, segb_ref, segn_ref, ys_ref, q_ref, ybuf, gsem):
    region_rows = ys_ref.shape[0] // N_GROUPS

    def copies(op, t, slot):
        nrows = [segn_ref[t * N_GROUPS + g] for g in range(N_GROUPS)]
        bases = [segb_ref[t * N_GROUPS + g] for g in range(N_GROUPS)]
        src_of = lambda a, b, size: ys_ref.at[pl.ds(b, size)]
        dst_of = lambda a, b, size: ybuf.at[slot, pl.ds(a, size)]
        _run_copies(op, src_of, dst_of, gsem.at[slot], nrows, bases, region_rows)

    def issue(t, slot):
        ybuf[slot, SEQ_TILE:SORT_ROWS, :] = jnp.zeros((SORT_ROWS - SEQ_TILE, D_MODEL), BF16)
        copies("start", t, slot)

    @pl.when(tile == 0)
    def _():
        issue(tile, 0)

    @pl.when(tile + 1 < n_tiles)
    def _():
        issue(tile + 1, (tile + 1) % 2)

    slot = tile % 2
    copies("wait", tile, slot)
    p = _sort_matrix(q_ref[0])
    return lax.dot_general(p, ybuf[slot], (((0,), (0,)), ((), ())), preferred_element_type=F32)


class _GatherAhead:
    def __init__(self, tile, n_tiles, segb_ref, segn_ref, ys_ref, q_ref, qnext_ref, ybuf, gsem, ynext):
        self.__dict__.update(locals())
        self.region_rows = ys_ref.shape[0] // N_GROUPS
        self.slots = ybuf.shape[0]

    def copies(self, op, t):
        slot = t % self.slots
        nrows = [self.segn_ref[t * N_GROUPS + g] for g in range(N_GROUPS)]
        bases = [self.segb_ref[t * N_GROUPS + g] for g in range(N_GROUPS)]
        src_of = lambda a, b, size: self.ys_ref.at[pl.ds(b, size)]
        dst_of = lambda a, b, size: self.ybuf.at[slot, pl.ds(a, size)]
        _run_copies(op, src_of, dst_of, self.gsem.at[slot], nrows, bases, self.region_rows)

    def issue(self, t):
        self.ybuf[t % self.slots, SEQ_TILE:SORT_ROWS, :] = jnp.zeros((SORT_ROWS - SEQ_TILE, D_MODEL), BF16)
        self.copies("start", t)

    def unsort(self, t, q):
        y = lax.dot_general(_sort_matrix(q), self.ybuf[t % self.slots], (((0,), (0,)), ((), ())),
                            preferred_element_type=F32)
        self.ynext[...] = y.astype(BF16)

    def top(self):
        tile, n_tiles = self.tile, self.n_tiles

        @pl.when(tile == 0)
        def _():
            self.issue(tile)

            @pl.when(n_tiles > 1)
            def _():
                self.issue(tile + 1)

            self.copies("wait", tile)
            self.unsort(tile, self.q_ref[0])

        @pl.when(tile + 2 < n_tiles)
        def _():
            self.issue(tile + 2)

        @pl.when(jnp.logical_and(tile >= 1, tile + 1 < n_tiles))
        def _():
            self.copies("wait", tile + 1)

        @pl.when(jnp.logical_and(tile == 0, n_tiles > 1))
        def _():
            self.copies("wait", tile + 1)

    def current(self):
        return self.ynext[...].astype(F32)

    def prepare_next(self):
        self.unsort(jnp.minimum(self.tile + 1, self.n_tiles - 1), self.qnext_ref[0])


def _load_weight_bf16(w_hbm, w_vmem, stage, sem):
    rows = stage.shape[1]
    n = w_vmem.shape[0] // rows
    copy = lambda c: pltpu.make_async_copy(w_hbm.at[pl.ds(c * rows, rows)], stage.at[c % 2],
                                           sem.at[c % 2])
    copy(0).start()
    for c in range(n):
        if c + 1 < n:
            copy(c + 1).start()
        copy(c).wait()
        w_vmem[c * rows:(c + 1) * rows, :] = stage[c % 2].astype(BF16)


def _weight_scratch(k, n):
    return [pltpu.VMEM((k, n), BF16), pltpu.VMEM((2, W_STAGE_ROWS, n), F32),
            pltpu.SemaphoreType.DMA((2,))]


def _hybrid_kernel(x_ref, mod_ref, win_hbm, wout_hbm, gmix_ref, gffn_ref, lng_ref, lnb_ref, sw_ref,
                   sb_ref, cw_ref, rw_ref, rb_ref,
                   x1_ref, hs_ref, q_ref, segb_ref, segn_ref, items_ref,
                   pbuf, win_ref, win_stage, win_sem, wout_ref, wout_stage, wout_sem,
                   h2buf, rowbuf, tri, zbuf, fill_s, pend_s, ssem, zsem):
    ts = x_ref.shape[1]
    tile = pl.program_id(0) * pl.num_programs(1) + pl.program_id(1)
    n_tiles = pl.num_programs(0) * pl.num_programs(1)
    sc = _ScatterRefs(gffn_ref, rw_ref, rb_ref, hs_ref, q_ref, segb_ref, segn_ref, items_ref,
                      h2buf, rowbuf, tri, zbuf, fill_s, pend_s, ssem, zsem)

    @pl.when(tile == 0)
    def _():
        _load_weight_bf16(win_hbm.at[0], win_ref, win_stage, win_sem)
        _load_weight_bf16(wout_hbm.at[0], wout_ref, wout_stage, wout_sem)

    @pl.when(pl.program_id(1) == 0)
    def _():
        pbuf[0:B_HALO, :] = jnp.zeros((B_HALO, D_B), F32)

    _sc_init(sc, tile)
    slot = tile % 2
    _sc_pending(sc, "wait", slot)

    prev = {}
    active = (tile > 0).astype(I32)

    _sc_route(sc, prev)
    _sc_rank(sc, prev, active)
    _sc_sort(sc, prev, slot)

    x = x_ref[0]
    mod = mod_ref[0, 0]
    sh1, sc1, g1 = mod[0:1], mod[1:2], mod[2:3]
    h = _rms_mod(x, gmix_ref[...], sh1, sc1)
    hb16 = h.astype(BF16)
    zcol = lambda j: jnp.dot(hb16, win_ref[:, j * D_A:(j + 1) * D_A], preferred_element_type=F32)

    v = _layer_norm(_gelu(zcol(1)), lng_ref[...], lnb_ref[...]).astype(BF16)
    u = _gelu(zcol(0))
    p = zcol(3) * zcol(4)
    pbuf[B_HALO:B_HALO + ts, :] = p
    cw = cw_ref[...]
    conv = (cw[0:1] * pbuf[B_HALO - 2:B_HALO - 2 + ts, :]
            + cw[1:2] * pbuf[B_HALO - 1:B_HALO - 1 + ts, :]
            + cw[2:3] * p)
    pbuf[0:B_HALO, :] = p[ts - B_HALO:ts, :]
    y_b = zcol(2) * conv

    tril = (lax.broadcasted_iota(I32, (CHUNK, CHUNK), 0)
            >= lax.broadcasted_iota(I32, (CHUNK, CHUNK), 1))
    chunks = []
    for n in range(ts // CHUNK):
        heads = []
        for hd in range(A_HEADS):
            w = jnp.where(tril, sw_ref[hd], 0.0).astype(BF16)
            vb = v[n * CHUNK:(n + 1) * CHUNK, hd * A_HEAD_DIM:(hd + 1) * A_HEAD_DIM]
            heads.append(jnp.dot(w, vb, preferred_element_type=F32) + sb_ref[hd])
        chunks.append(jnp.concatenate(heads, axis=1))
    y_a = u * jnp.concatenate(chunks, axis=0)

    ycat = jnp.concatenate([y_a, y_b], axis=1).astype(BF16)
    y = jnp.dot(ycat, wout_ref[...], preferred_element_type=F32)
    x1 = x + g1 * y
    x1_ref[0] = x1
    h2buf[...] = _prenorm(x1, mod, gffn_ref)
    _sc_start(sc, prev, jnp.maximum(tile - 1, 0), slot)

    @pl.when(tile == n_tiles - 1)
    def _():
        _sc_pending(sc, "wait", 1 - slot)
        last = {}
        _sc_route(sc, last)
        _sc_rank(sc, last, 1)
        _sc_sort(sc, last, 1 - slot)
        _sc_start(sc, last, tile, 1 - slot)
        _sc_finish(sc)


def _band_matrix():
    shape = (CONV_ROWS, SUBLANES * S_ROWS)
    t = lax.broadcasted_iota(I32, shape, 0)
    c = lax.broadcasted_iota(I32, shape, 1)
    hit = c == t + S_LEAD
    for r in range(1, SUBLANES):
        hit = hit | (c == t + S_LEAD + r * (S_ROWS - 1))
    return jnp.where(hit, 1.0, 0.0).astype(BF16)


def _causal_dw_conv(cbuf, w8, obuf, band, sbuf, row0, nrows, col0, ncols, first_slot):
    n_a = -(-C_CONV // SUBLANES)
    cols = slice(col0, col0 + ncols)
    for i, t0 in enumerate(range(row0, row0 + nrows, CONV_ROWS)):
        stack = sbuf.at[(first_slot + i) % sbuf.shape[0]]
        for r in range(SUBLANES):
            part = None
            for a in range(n_a):
                k = C_CONV - 1 - SUBLANES * a - r
                if k < 0:
                    continue
                r0 = t0 + C_HALO - S_LEAD - SUBLANES * a
                z3 = cbuf[r0:r0 + S_ROWS, cols].reshape(S_ROWS // SUBLANES, SUBLANES, ncols)
                term = (z3 * w8[SUBLANES * k:SUBLANES * (k + 1), cols][None]).reshape(S_ROWS, ncols)
                part = term if part is None else part + term
            stack[r * S_ROWS:(r + 1) * S_ROWS, :] = part.astype(BF16)
        obuf[t0:t0 + CONV_ROWS, cols] = jnp.dot(band[...], stack[...], preferred_element_type=F32)


def _conformer_kernel(psegb_ref, psegn_ref,
                      xp_ref, ys_ref, pq_ref, pmod_ref, pqn_ref, mod_ref, w1_hbm, w2_hbm, gmix_ref, gffn_ref,
                      b1_ref, dww_ref, dwb_ref, lng_ref, lnb_ref, b2_ref, rw_ref, rb_ref,
                      x1_ref, hs_ref, q_ref, segb_ref, segn_ref, items_ref,
                      cbuf, obuf, band, sbuf, w8, ynext, ybuf, gsem,
                      w1_ref, w1_stage, w1_sem, w2_ref, w2_stage, w2_sem,
                      h2buf, rowbuf, tri, zbuf, fill_s, pend_s, ssem, zsem):
    ts = xp_ref.shape[1]
    tile = pl.program_id(0) * pl.num_programs(1) + pl.program_id(1)
    n_tiles = pl.num_programs(0) * pl.num_programs(1)
    sc = _ScatterRefs(gffn_ref, rw_ref, rb_ref, hs_ref, q_ref, segb_ref, segn_ref, items_ref,
                      h2buf, rowbuf, tri, zbuf, fill_s, pend_s, ssem, zsem)

    @pl.when(tile == 0)
    def _():
        _load_weight_bf16(w1_hbm.at[0], w1_ref, w1_stage, w1_sem)
        _load_weight_bf16(w2_hbm.at[0], w2_ref, w2_stage, w2_sem)
        band[...] = _band_matrix()
        for k in range(C_CONV):
            w8[SUBLANES * k:SUBLANES * (k + 1), :] = jnp.broadcast_to(dww_ref[k:k + 1, :], (SUBLANES, D_C))

    @pl.when(pl.program_id(1) == 0)
    def _():
        cbuf[0:C_HALO, :] = jnp.zeros((C_HALO, D_C), F32)

    _sc_init(sc, tile)
    slot = tile % 2
    _sc_pending(sc, "wait", slot)
    ga = _GatherAhead(tile, n_tiles, psegb_ref, psegn_ref, ys_ref, pq_ref, pqn_ref, ybuf, gsem, ynext)
    ga.top()
    y_prev = ga.current()

    prev = {}
    active = (tile > 0).astype(I32)

    x = xp_ref[0] + pmod_ref[0, 0][5:6] * y_prev
    mod = mod_ref[0, 0]
    sh1, sc1, g1 = mod[0:1], mod[1:2], mod[2:3]
    h = _rms_mod(x, gmix_ref[...], sh1, sc1)
    hb = h.astype(BF16)
    def glu(c0):
        ca, cg = slice(c0, c0 + GLU_COLS), slice(D_C + c0, D_C + c0 + GLU_COLS)
        za = jnp.dot(hb, w1_ref[:, ca], preferred_element_type=F32) + b1_ref[:, ca]
        zg = jnp.dot(hb, w1_ref[:, cg], preferred_element_type=F32) + b1_ref[:, cg]
        cbuf[C_HALO:C_HALO + ts, ca] = za * jax.nn.sigmoid(zg)

    for c0 in range(0, D_C, GLU_COLS):
        glu(c0)
    _sc_route(sc, prev)
    _sc_rank(sc, prev, active)
    _sc_sort(sc, prev, slot)
    for k, c0 in enumerate(range(0, D_C, GLU_COLS)):
        _causal_dw_conv(cbuf, w8, obuf, band, sbuf, 0, ts, c0, GLU_COLS,
                        k * (ts // CONV_ROWS))
        if k == 1:
            ga.prepare_next()
    cbuf[0:C_HALO, :] = cbuf[ts:ts + C_HALO, :]

    zc = _layer_norm(obuf[...] + dwb_ref[...], lng_ref[...], lnb_ref[...])
    zc = zc * jax.nn.sigmoid(zc)
    y = jnp.dot(zc.astype(BF16), w2_ref[...], preferred_element_type=F32) + b2_ref[...]
    x1 = x + g1 * y
    x1_ref[0] = x1
    h2buf[...] = _prenorm(x1, mod, gffn_ref)
    _sc_start(sc, prev, jnp.maximum(tile - 1, 0), slot)

    @pl.when(tile == n_tiles - 1)
    def _():
        _sc_pending(sc, "wait", 1 - slot)
        last = {}
        _sc_route(sc, last)
        _sc_rank(sc, last, 1)
        _sc_sort(sc, last, 1 - slot)
        _sc_start(sc, last, tile, 1 - slot)
        _sc_finish(sc)


def _final_kernel(psegb_ref, psegn_ref, xp_ref, ys_ref, pq_ref, pmod_ref, fg_ref, o_ref, ybuf, gsem):
    tile = pl.program_id(0) * pl.num_programs(1) + pl.program_id(1)
    n_tiles = pl.num_programs(0) * pl.num_programs(1)
    y_prev = _gather_prev_ffn(tile, n_tiles, psegb_ref, psegn_ref, ys_ref, pq_ref, ybuf, gsem)
    x = xp_ref[0] + pmod_ref[0, 0][5:6] * y_prev
    o_ref[0] = _rms(x, fg_ref[...])


def _scatter_outputs(bsz, seq, d):
    t = bsz * seq
    n_tiles = t // SEQ_TILE
    smem = pl.BlockSpec(memory_space=pltpu.SMEM)
    specs = [pl.BlockSpec((1, SEQ_TILE, d), lambda b, s, *_: (b, s, 0)),
             pl.BlockSpec(memory_space=pl.ANY),
             pl.BlockSpec((n_tiles, 1, SEQ_TILE), lambda b, s, *_: (0, 0, 0)),
             smem, smem, smem]
    shapes = [jax.ShapeDtypeStruct((bsz, seq, d), F32),
              jax.ShapeDtypeStruct((N_GROUPS * _region_rows(t), ROW_W), BF16),
              jax.ShapeDtypeStruct((n_tiles, 1, SEQ_TILE), I32),
              jax.ShapeDtypeStruct((n_tiles * N_GROUPS,), I32),
              jax.ShapeDtypeStruct((n_tiles * N_GROUPS,), I32),
              jax.ShapeDtypeStruct((ITEM_FIELDS * _moe_steps(t),), I32)]
    scratch = [pltpu.VMEM((SEQ_TILE, d), F32),
               pltpu.VMEM((2, SORT_ROWS, ROW_W), BF16),
               pltpu.VMEM((SEQ_TILE, SEQ_TILE), BF16),
               pltpu.VMEM((MOE_TILE, ROW_W), BF16),
               pltpu.SMEM((N_GROUPS,), I32),
               pltpu.SMEM((4 * N_GROUPS,), I32),
               pltpu.SemaphoreType.DMA((2,)),
               pltpu.SemaphoreType.DMA(())]
    return specs, shapes, scratch


def _mod_spec(d, layer):
    return pl.BlockSpec((1, 1, N_MOD, d), lambda b, s, *_: (layer, b, 0, 0))


def _gather_inputs(bsz, seq, d, prev_layer):
    nseq = seq // SEQ_TILE
    specs = [pl.BlockSpec((1, SEQ_TILE, d), lambda b, s, *_: (b, s, 0)),
             pl.BlockSpec(memory_space=pl.ANY),
             pl.BlockSpec((1, 1, SEQ_TILE), lambda b, s, *_: (b * nseq + s, 0, 0)),
             _mod_spec(d, prev_layer)]
    scratch = [pltpu.VMEM((2, SORT_ROWS, d), BF16), pltpu.SemaphoreType.DMA((2,))]
    return specs, scratch


def _const_specs(consts):
    return [pl.BlockSpec(a.shape, lambda b, s, *_, nd=a.ndim: (0,) * nd) for a in consts]


_PARAMS = pltpu.CompilerParams(dimension_semantics=("arbitrary", "arbitrary"),
                               vmem_limit_bytes=VMEM_LIMIT)


def _hybrid_call(x, mod, layer, w_in, w_out, consts):
    bsz, seq, d = x.shape
    out_specs, out_shapes, sc_scratch = _scatter_outputs(bsz, seq, d)
    hbm = pl.BlockSpec(memory_space=pl.ANY)
    return pl.pallas_call(
        _hybrid_kernel,
        grid=(bsz, seq // SEQ_TILE),
        in_specs=[pl.BlockSpec((1, SEQ_TILE, d), lambda b, s: (b, s, 0)),
                  _mod_spec(d, layer), hbm, hbm]
                 + _const_specs(consts),
        out_specs=out_specs,
        out_shape=out_shapes,
        scratch_shapes=[pltpu.VMEM((SEQ_TILE + B_HALO, D_B), F32)]
                       + _weight_scratch(*w_in.shape[1:]) + _weight_scratch(*w_out.shape[1:])
                       + sc_scratch,
        compiler_params=_PARAMS,
        name="mixer_hybrid",
    )(x, mod, w_in, w_out, *consts)


def _conformer_call(prev, mod, layer, w1, w2, consts):
    x1p, ysp, qp, segbp, segnp = prev
    bsz, seq, d = x1p.shape
    g_specs, _ = _gather_inputs(bsz, seq, d, layer - 1)
    nseq = seq // SEQ_TILE
    last = bsz * nseq - 1
    g_specs = g_specs + [pl.BlockSpec((1, 1, SEQ_TILE),
                                      lambda b, s, *_: (jnp.minimum(b * nseq + s + 1, last), 0, 0))]
    g_scratch = [pltpu.VMEM((SEQ_TILE, d), BF16), pltpu.VMEM((3, SORT_ROWS, d), BF16),
                 pltpu.SemaphoreType.DMA((3,))]
    out_specs, out_shapes, sc_scratch = _scatter_outputs(bsz, seq, d)
    hbm = pl.BlockSpec(memory_space=pl.ANY)
    grid_spec = pltpu.PrefetchScalarGridSpec(
        num_scalar_prefetch=2,
        grid=(bsz, seq // SEQ_TILE),
        in_specs=g_specs + [_mod_spec(d, layer), hbm, hbm]
                 + _const_specs(consts),
        out_specs=out_specs,
        scratch_shapes=[pltpu.VMEM((SEQ_TILE + C_HALO, D_C), F32), pltpu.VMEM((SEQ_TILE, D_C), F32),
                        pltpu.VMEM((CONV_ROWS, SUBLANES * S_ROWS), BF16),
                        pltpu.VMEM((S_SLOTS, SUBLANES * S_ROWS, GLU_COLS), BF16),
                        pltpu.VMEM((C_CONV * SUBLANES, D_C), F32)]
                       + g_scratch + _weight_scratch(*w1.shape[1:]) + _weight_scratch(*w2.shape[1:])
                       + sc_scratch,
    )
    return pl.pallas_call(
        _conformer_kernel, grid_spec=grid_spec, out_shape=out_shapes,
        compiler_params=_PARAMS, name="mixer_conformer",
    )(segbp, segnp, x1p, ysp, qp, mod, qp, mod, w1, w2, *consts)


def _final_call(prev, mod, prev_layer, final_g):
    x1p, ysp, qp, segbp, segnp = prev
    bsz, seq, d = x1p.shape
    g_specs, g_scratch = _gather_inputs(bsz, seq, d, prev_layer)
    grid_spec = pltpu.PrefetchScalarGridSpec(
        num_scalar_prefetch=2,
        grid=(bsz, seq // SEQ_TILE),
        in_specs=g_specs + [pl.BlockSpec((1, d), lambda b, s, *_: (0, 0))],
        out_specs=pl.BlockSpec((1, SEQ_TILE, d), lambda b, s, *_: (b, s, 0)),
        scratch_shapes=g_scratch,
    )
    return pl.pallas_call(
        _final_kernel, grid_spec=grid_spec,
        out_shape=jax.ShapeDtypeStruct((bsz, seq, d), F32),
        compiler_params=_PARAMS, name="final_norm",
    )(segbp, segnp, x1p, ysp, qp, mod, final_g.reshape(1, d))


def _moe_kernel(items_ref, hs_ref, wg_hbm, wu_hbm, wd_hbm, ys_ref,
                wg_st, wu_st, wd_st, wsem, wg_bf, wu_bf, wd_bf, slot_s, *, layer):
    i = pl.program_id(0)
    n_items = pl.num_programs(0)
    grp = items_ref[n_items + i]

    def fetch(g, slot):
        src = layer * N_GROUPS + g
        return [pltpu.make_async_copy(w.at[src], st.at[slot], wsem.at[slot, k])
                for k, (w, st) in enumerate(((wg_hbm, wg_st), (wu_hbm, wu_st), (wd_hbm, wd_st)))]

    @pl.when(i == 0)
    def _():
        slot_s[0] = 0
        for cp in fetch(grp, 0):
            cp.start()

    @pl.when(items_ref[3 * n_items + i] != 0)
    def _():
        slot = slot_s[0]
        for cp in fetch(grp, slot):
            cp.wait()
        wg_bf[...] = wg_st[slot].astype(BF16)
        wu_bf[...] = wu_st[slot].astype(BF16)
        wd_bf[...] = wd_st[slot].astype(BF16)
        nxt = items_ref[4 * n_items + i]

        @pl.when(nxt >= 0)
        def _():
            for cp in fetch(nxt, 1 - slot):
                cp.start()

        slot_s[0] = 1 - slot

    @pl.when(items_ref[2 * n_items + i] != 0)
    def _():
        rows = hs_ref[...]
        h = rows[:, :D_MODEL]
        m = rows[:, D_MODEL:].astype(F32)
        ne = EXPERTS_PER_GROUP
        y = None
        for j in range(ne):
            gate = m[:, j:j + 1] + (m[:, ne + j:ne + j + 1] + m[:, 2 * ne + j:2 * ne + j + 1])
            a = jnp.dot(h, wg_bf[j], preferred_element_type=F32)
            a = a * jax.nn.sigmoid(a) * jnp.dot(h, wu_bf[j], preferred_element_type=F32)
            yj = jnp.dot((a * gate).astype(BF16), wd_bf[j], preferred_element_type=F32)
            y = yj if y is None else y + yj
        ys_ref[...] = y.astype(BF16)


def _moe_call(hs, items, layer, wg, wu, wd):
    d = D_MODEL
    ne = EXPERTS_PER_GROUP
    n_items = items.shape[0] // ITEM_FIELDS
    row_idx = lambda i, items: (items[i], 0)
    hbm = pl.BlockSpec(memory_space=pl.ANY)
    grid_spec = pltpu.PrefetchScalarGridSpec(
        num_scalar_prefetch=1,
        grid=(n_items,),
        in_specs=[pl.BlockSpec((MOE_TILE, ROW_W), row_idx), hbm, hbm, hbm],
        out_specs=pl.BlockSpec((MOE_TILE, d), row_idx),
        scratch_shapes=[pltpu.VMEM((2, ne, d, D_EXPERT), F32), pltpu.VMEM((2, ne, d, D_EXPERT), F32),
                        pltpu.VMEM((2, ne, D_EXPERT, d), F32), pltpu.SemaphoreType.DMA((2, 3)),
                        pltpu.VMEM((ne, d, D_EXPERT), BF16), pltpu.VMEM((ne, d, D_EXPERT), BF16),
                        pltpu.VMEM((ne, D_EXPERT, d), BF16), pltpu.SMEM((1,), I32)],
    )
    return pl.pallas_call(
        functools.partial(_moe_kernel, layer=layer), grid_spec=grid_spec,
        out_shape=jax.ShapeDtypeStruct((hs.shape[0], d), BF16),
        compiler_params=pltpu.CompilerParams(dimension_semantics=("arbitrary",),
                                             vmem_limit_bytes=VMEM_LIMIT),
        name="moe",
    )(items, hs, wg, wu, wd)


def kernel(x, c, ada_w, ada_b, norm_mix_g, norm_ffn_g, hyb_w_in, sgu_ln_g, sgu_ln_b, sgu_w, sgu_b,
           sconv_w, hyb_w_out, conf_w1, conf_b1, conf_dw_w, conf_dw_b, conf_ln_g, conf_ln_b,
           conf_w2, conf_b2, router_w, router_bias, moe_w_gate, moe_w_up, moe_w_down, final_norm_g):
    depth = ada_w.shape[0]
    bsz, seq, d = x.shape
    assert depth == 2 and seq % SEQ_TILE == 0
    mod = _ada_mod(c, ada_w, ada_b).reshape(depth, bsz, N_MOD, d)

    perm = jnp.arange(N_EXPERTS).reshape(N_GROUPS, EXPERTS_PER_GROUP).T.reshape(-1)
    rw_p = router_w[:, perm]
    rw_t = jnp.concatenate([rw_p, rw_p, jnp.zeros((d, LANES - 2 * N_EXPERTS), F32)], axis=1)
    rb_t = router_bias[perm].reshape(N_EXPERTS, 1)

    ne = EXPERTS_PER_GROUP
    wg = moe_w_gate.reshape(depth * N_GROUPS, ne, d, D_EXPERT)
    wu = moe_w_up.reshape(depth * N_GROUPS, ne, d, D_EXPERT)
    wd = moe_w_down.reshape(depth * N_GROUPS, ne, D_EXPERT, d)

    row = lambda a: a.reshape(1, -1)
    consts0 = [row(norm_mix_g[0]), row(norm_ffn_g[0]), row(sgu_ln_g[0]), row(sgu_ln_b[0]), sgu_w[0],
               sgu_b[0].reshape(A_HEADS, CHUNK, 1), sconv_w[0], rw_t, rb_t]
    x1, hs, q, segb, segn, items = _hybrid_call(x, mod, 0, hyb_w_in, hyb_w_out, consts0)
    ys = _moe_call(hs, items, 0, wg, wu, wd)

    consts1 = [row(norm_mix_g[1]), row(norm_ffn_g[1]), row(conf_b1[0]), conf_dw_w[0],
               row(conf_dw_b[0]), row(conf_ln_g[0]), row(conf_ln_b[0]), row(conf_b2[0]), rw_t, rb_t]
    x1, hs, q1, segb1, segn1, items = _conformer_call((x1, ys, q, segb, segn), mod, 1,
                                                      conf_w1, conf_w2, consts1)
    ys = _moe_call(hs, items, 1, wg, wu, wd)
    return _final_call((x1, ys, q1, segb1, segn1), mod, 1, final_norm_g)
```

```python
import functools

import jax
import jax.numpy as jnp
from jax import lax
from jax.experimental import pallas as pl
from jax.experimental.pallas import tpu as pltpu

D_MODEL = 1024
CHUNK = 128
A_HEADS = 4
A_HEAD_DIM = 128
D_A = A_HEADS * A_HEAD_DIM
D_B = 512
B_CONV = 3
D_IN_HYB = 2 * D_A + 3 * D_B
D_C = D_MODEL
C_CONV = 31
N_EXPERTS = 16
N_GROUPS = 4
EXPERTS_PER_GROUP = N_EXPERTS // N_GROUPS
D_EXPERT = 256
N_MOD = 6
EPS = 1e-6

SUBLANES = 8
LANES = 128
ROW_ALIGN = 16

SEQ_TILE = 512
MOE_TILE = 512
ITEM_FIELDS = 5
MOD_TILE = 2048
W_STAGE_ROWS = 128
C_HALO = 40
B_HALO = 8
CONV_ROWS = 128
GLU_COLS = 256
S_LEAD = ROW_ALIGN
S_ROWS = CONV_ROWS + S_LEAD
S_SLOTS = 2
SORT_ROWS = SEQ_TILE + N_GROUPS * ROW_ALIGN
ROW_W = D_MODEL + LANES
VMEM_LIMIT = 56 * 1024 * 1024

F32 = jnp.float32
BF16 = jnp.bfloat16
I32 = jnp.int32


def _round_up(n, m):
    return -(-n // m) * m


def _moe_steps(t):
    return -(-(t + (t // SEQ_TILE) * N_GROUPS * (ROW_ALIGN - 1)) // MOE_TILE) + N_GROUPS


def _region_rows(t):
    return _round_up(t + (t // SEQ_TILE) * ROW_ALIGN, MOE_TILE)


def _split_bf16(a):
    hi = a.astype(BF16)
    lo = (a - hi.astype(F32)).astype(BF16)
    return hi, lo


def _dot3(a, b):
    ah, al = _split_bf16(a)
    bh, bl = _split_bf16(b)
    d = functools.partial(jnp.dot, preferred_element_type=F32)
    return d(ah, bh) + (d(ah, bl) + d(al, bh))


def _rms(x, g):
    return x * lax.rsqrt(jnp.mean(x * x, axis=-1, keepdims=True) + EPS) * g


def _rms_mod(x, g, shift, scale):
    return x * lax.rsqrt(jnp.mean(x * x, axis=-1, keepdims=True) + EPS) * (g * (1.0 + scale)) + shift


def _gelu(x):
    return 0.5 * x * (1.0 + lax.erf(x * (2.0 ** -0.5)))


def _layer_norm(x, g, b):
    mu = jnp.mean(x, axis=-1, keepdims=True)
    xc = x - mu
    var = jnp.mean(xc * xc, axis=-1, keepdims=True)
    return xc * lax.rsqrt(var + EPS) * g + b


def _mod_kernel(c_ref, w_ref, b_ref, o_ref):
    c = c_ref[...]
    c_act = c * jax.nn.sigmoid(c)
    o_ref[0] = _dot3(c_act, w_ref[0]) + b_ref[0]


def _ada_mod(c, ada_w, ada_b):
    depth, d, n = ada_w.shape
    bsz = c.shape[0]
    return pl.pallas_call(
        _mod_kernel,
        grid=(depth, n // MOD_TILE),
        in_specs=[
            pl.BlockSpec((bsz, d), lambda l, j: (0, 0)),
            pl.BlockSpec((1, d, MOD_TILE), lambda l, j: (l, 0, j)),
            pl.BlockSpec((1, 1, MOD_TILE), lambda l, j: (l, 0, j)),
        ],
        out_specs=pl.BlockSpec((1, bsz, MOD_TILE), lambda l, j: (l, 0, j)),
        out_shape=jax.ShapeDtypeStruct((depth, bsz, n), F32),
        compiler_params=pltpu.CompilerParams(
            dimension_semantics=("arbitrary", "arbitrary"), vmem_limit_bytes=VMEM_LIMIT),
        name="ada_mod",
    )(c, ada_w, ada_b.reshape(depth, 1, n))


def _route(h2, rw_ref, rb_ref):
    wh, wl = _split_bf16(rw_ref[...])
    lane = lax.broadcasted_iota(I32, wh.shape, 1)
    hh, hl = _split_bf16(h2)
    acc = jnp.dot(hh, jnp.where(lane < N_EXPERTS, wh, wl), preferred_element_type=F32)
    low = jnp.dot(hl, wh, preferred_element_type=F32)
    lane = lax.broadcasted_iota(I32, acc.shape, 1)
    acc_t = (acc + jnp.where(lane < N_EXPERTS, low, 0.0)).T
    logits = acc_t[:N_EXPERTS] + acc_t[N_EXPERTS:2 * N_EXPERTS]
    scores = jax.nn.sigmoid(logits)
    biased = scores + rb_ref[...]
    ng, ne = N_GROUPS, EXPERTS_PER_GROUP
    b = [biased[j * ng:(j + 1) * ng] for j in range(ne)]
    s = [scores[j * ng:(j + 1) * ng] for j in range(ne)]
    gscore = None
    for i in range(ne):
        for j in range(i + 1, ne):
            pair = b[i] + b[j]
            gscore = pair if gscore is None else jnp.maximum(gscore, pair)
    gid = lax.broadcasted_iota(I32, gscore.shape, 0)
    gmax = jnp.max(gscore, axis=0, keepdims=True)
    best = jnp.min(jnp.where(gscore == gmax, gid, ng), axis=0, keepdims=True)
    onb = gid == best
    cb = [jnp.sum(jnp.where(onb, b[j], 0.0), axis=0, keepdims=True) for j in range(ne)]
    cs = [jnp.sum(jnp.where(onb, s[j], 0.0), axis=0, keepdims=True) for j in range(ne)]
    sel = []
    for j in range(ne):
        rank = jnp.zeros_like(best)
        for i in range(ne):
            if i == j:
                continue
            ahead = (cb[i] >= cb[j]) if i < j else (cb[i] > cb[j])
            rank = rank + ahead.astype(I32)
        sel.append(rank < 2)
    wsum = sum(jnp.where(sel[j], cs[j], 0.0) for j in range(ne))
    gates = jnp.concatenate([jnp.where(sel[j], cs[j], 0.0) / wsum for j in range(ne)], axis=0)
    return gates, jnp.where(onb, 1.0, 0.0)


def _sort_matrix(q):
    rows = lax.broadcasted_iota(I32, (SORT_ROWS, q.shape[1]), 0)
    return jnp.where(rows == q, 1.0, 0.0).astype(BF16)


def _run_copies(op, src_of, dst_of, sem, nrows, bases, region_rows):
    if op == "wait":
        total = pl.multiple_of(sum(nrows[1:], nrows[0]), ROW_ALIGN)

        @pl.when(total > 0)
        def _():
            pltpu.make_async_copy(src_of(0, 0, total), dst_of(0, 0, total), sem).wait()
        return
    tile_off = 0
    for g in range(N_GROUPS):
        n = pl.multiple_of(nrows[g], ROW_ALIGN)
        a = pl.multiple_of(tile_off, ROW_ALIGN)
        b = pl.multiple_of(g * region_rows + bases[g], ROW_ALIGN)

        @pl.when(n > 0)
        def _(a=a, b=b, n=n, g=g):
            pltpu.make_async_copy(src_of(a, b, n), dst_of(a, b, n), sem).start(priority=g % 2)

        tile_off = tile_off + nrows[g]


class _ScatterRefs:
    def __init__(self, gffn_ref, rw_ref, rb_ref, hs_ref, q_ref, segb_ref, segn_ref, items_ref,
                 h2buf, rowbuf, tri, zbuf, fill_s, pend_s, ssem, zsem):
        self.__dict__.update(locals())
        self.region_rows = hs_ref.shape[0] // N_GROUPS
        self.dst_of = lambda a, b, size: hs_ref.at[pl.ds(b, size)]


def _sc_init(sc, tile):
    @pl.when(tile == 0)
    def _():
        for g in range(N_GROUPS):
            sc.fill_s[g] = 0
        for i in range(sc.pend_s.shape[0]):
            sc.pend_s[i] = 0
        ts = sc.tri.shape[0]
        k = lax.broadcasted_iota(I32, (ts, ts), 0)
        t = lax.broadcasted_iota(I32, (ts, ts), 1)
        sc.tri[...] = jnp.where(k <= t, 1.0, 0.0).astype(BF16)
        sc.h2buf[...] = jnp.zeros_like(sc.h2buf)


def _sc_pending(sc, op, slot):
    rec = slot * (2 * N_GROUPS)
    src_of = lambda a, b, size: sc.rowbuf.at[slot, pl.ds(a, size)]
    _run_copies(op, src_of, sc.dst_of, sc.ssem.at[slot], [sc.pend_s[rec + g] for g in range(N_GROUPS)],
                [sc.pend_s[rec + N_GROUPS + g] for g in range(N_GROUPS)], sc.region_rows)


def _prenorm(x1, mod, gffn_ref):
    sh2, sc2 = mod[3:4], mod[4:5]
    return _rms_mod(x1, gffn_ref[...], sh2, sc2)


def _sc_route(sc, st):
    st["h2"] = sc.h2buf[...]
    st["gates"], st["onehot"] = _route(st["h2"], sc.rw_ref, sc.rb_ref)


def _sc_rank(sc, st, active):
    onehot = st["onehot"]
    ts = onehot.shape[1]
    oh8 = jnp.concatenate([onehot, jnp.zeros((SUBLANES - N_GROUPS, ts), F32)], axis=0).astype(BF16)
    csum = jnp.dot(oh8, sc.tri[...], preferred_element_type=F32)[:N_GROUPS]
    counts = [csum[g, ts - 1].astype(I32) * active for g in range(N_GROUPS)]
    st["nrows"] = [(c + (ROW_ALIGN - 1)) & ~(ROW_ALIGN - 1) for c in counts]
    q = jnp.zeros((1, ts), F32)
    tile_off = 0
    for g in range(N_GROUPS):
        q = q + onehot[g:g + 1] * (csum[g:g + 1] - 1.0 + tile_off.astype(F32)
                                   if g else csum[g:g + 1] - 1.0)
        tile_off = tile_off + st["nrows"][g]
    st["q"] = q.astype(I32)


def _sc_sort(sc, st, slot):
    gates = st["gates"]
    ts = gates.shape[1]
    hi = gates.astype(BF16).astype(F32)
    mid = (gates - hi).astype(BF16).astype(F32)
    lo = (gates - hi - mid).astype(BF16).astype(F32)
    meta_t = jnp.concatenate(
        [hi, mid, lo, jnp.zeros((LANES - 3 * EXPERTS_PER_GROUP, ts), F32)], axis=0)
    rows = jnp.concatenate([st["h2"].astype(BF16), meta_t.T.astype(BF16)], axis=1)
    sorted_rows = jnp.dot(_sort_matrix(st["q"]), rows, preferred_element_type=F32)
    sc.rowbuf[slot] = sorted_rows.astype(BF16)


def _sc_start(sc, st, t_idx, slot):
    sc.q_ref[t_idx] = st["q"]
    nrows = st["nrows"]
    rec = slot * (2 * N_GROUPS)
    for g in range(N_GROUPS):
        base = sc.fill_s[g]
        sc.pend_s[rec + g] = nrows[g]
        sc.pend_s[rec + N_GROUPS + g] = base
        sc.segn_ref[t_idx * N_GROUPS + g] = nrows[g]
        sc.segb_ref[t_idx * N_GROUPS + g] = base
        sc.fill_s[g] = base + nrows[g]
    _sc_pending(sc, "start", slot)


def _write_moe_items(sc):
    n_items = sc.items_ref.shape[0] // ITEM_FIELDS
    per_region = sc.region_rows // MOE_TILE
    ends = []
    total = 0
    for g in range(N_GROUPS):
        total = total + (sc.fill_s[g] + (MOE_TILE - 1)) // MOE_TILE
        ends.append(total)
    nxt = [None] * N_GROUPS
    follow = -1
    for g in reversed(range(N_GROUPS)):
        nxt[g] = follow
        follow = jnp.where(ends[g] > (ends[g - 1] if g else 0), g, follow)
    last = jnp.maximum(total - 1, 0)
    prev_grp = None
    for i in range(n_items):
        ic = jnp.minimum(i, last)
        grp = sum((ic >= e).astype(I32) for e in ends[:-1])
        start = jnp.where(grp == 0, 0, jnp.where(grp == 1, ends[0], jnp.where(grp == 2, ends[1], ends[2])))
        fresh = jnp.ones((), I32) if prev_grp is None else (grp != prev_grp).astype(I32)
        sc.items_ref[i] = grp * per_region + ic - start
        sc.items_ref[n_items + i] = grp
        sc.items_ref[2 * n_items + i] = (i < total).astype(I32)
        sc.items_ref[3 * n_items + i] = fresh
        sc.items_ref[4 * n_items + i] = jnp.where(
            grp == 0, nxt[0], jnp.where(grp == 1, nxt[1], jnp.where(grp == 2, nxt[2], nxt[3])))
        prev_grp = grp


def _sc_finish(sc):
    _sc_pending(sc, "wait", 0)
    _sc_pending(sc, "wait", 1)
    sc.zbuf[...] = jnp.zeros_like(sc.zbuf)
    for op in ("start", "wait"):
        for g in range(N_GROUPS):
            fill = sc.fill_s[g]
            n = pl.multiple_of((-fill) & (MOE_TILE - 1), ROW_ALIGN)
            b = pl.multiple_of(g * sc.region_rows + fill, ROW_ALIGN)

            @pl.when(n > 0)
            def _(n=n, b=b, op=op):
                cp = pltpu.make_async_copy(sc.zbuf.at[pl.ds(0, n)], sc.hs_ref.at[pl.ds(b, n)], sc.zsem)
                getattr(cp, op)()
    _write_moe_items(sc)


def _gather_prev_ffn(tile, n_tiles, segb_ref, segn_ref, ys_ref, q_ref, ybuf, gsem):
    region_rows = ys_ref.shape[0] // N_GROUPS

    def copies(op, t, slot):
        nrows = [segn_ref[t * N_GROUPS + g] for g in range(N_GROUPS)]
        bases = [segb_ref[t * N_GROUPS + g] for g in range(N_GROUPS)]
        src_of = lambda a, b, size: ys_ref.at[pl.ds(b, size)]
        dst_of = lambda a, b, size: ybuf.at[slot, pl.ds(a, size)]
        _run_copies(op, src_of, dst_of, gsem.at[slot], nrows, bases, region_rows)

    def issue(t, slot):
        ybuf[slot, SEQ_TILE:SORT_ROWS, :] = jnp.zeros((SORT_ROWS - SEQ_TILE, D_MODEL), BF16)
        copies("start", t, slot)

    @pl.when(tile == 0)
    def _():
        issue(tile, 0)

    @pl.when(tile + 1 < n_tiles)
    def _():
        issue(tile + 1, (tile + 1) % 2)

    slot = tile % 2
    copies("wait", tile, slot)
    p = _sort_matrix(q_ref[0])
    return lax.dot_general(p, ybuf[slot], (((0,), (0,)), ((), ())), preferred_element_type=F32)


class _GatherAhead:
    def __init__(self, tile, n_tiles, segb_ref, segn_ref, ys_ref, q_ref, qnext_ref, ybuf, gsem, ynext):
        self.__dict__.update(locals())
        self.region_rows = ys_ref.shape[0] // N_GROUPS
        self.slots = ybuf.shape[0]

    def copies(self, op, t):
        slot = t % self.slots
        nrows = [self.segn_ref[t * N_GROUPS + g] for g in range(N_GROUPS)]
        bases = [self.segb_ref[t * N_GROUPS + g] for g in range(N_GROUPS)]
        src_of = lambda a, b, size: self.ys_ref.at[pl.ds(b, size)]
        dst_of = lambda a, b, size: self.ybuf.at[slot, pl.ds(a, size)]
        _run_copies(op, src_of, dst_of, self.gsem.at[slot], nrows, bases, self.region_rows)

    def issue(self, t):
        self.ybuf[t % self.slots, SEQ_TILE:SORT_ROWS, :] = jnp.zeros((SORT_ROWS - SEQ_TILE, D_MODEL), BF16)
        self.copies("start", t)

    def unsort(self, t, q):
        y = lax.dot_general(_sort_matrix(q), self.ybuf[t % self.slots], (((0,), (0,)), ((), ())),
                            preferred_element_type=F32)
        self.ynext[...] = y.astype(BF16)

    def top(self):
        tile, n_tiles = self.tile, self.n_tiles

        @pl.when(tile == 0)
        def _():
            self.issue(tile)

            @pl.when(n_tiles > 1)
            def _():
                self.issue(tile + 1)

            self.copies("wait", tile)
            self.unsort(tile, self.q_ref[0])

        @pl.when(tile + 2 < n_tiles)
        def _():
            self.issue(tile + 2)

        @pl.when(jnp.logical_and(tile >= 1, tile + 1 < n_tiles))
        def _():
            self.copies("wait", tile + 1)

        @pl.when(jnp.logical_and(tile == 0, n_tiles > 1))
        def _():
            self.copies("wait", tile + 1)

    def current(self):
        return self.ynext[...].astype(F32)

    def prepare_next(self):
        self.unsort(jnp.minimum(self.tile + 1, self.n_tiles - 1), self.qnext_ref[0])


def _load_weight_bf16(w_hbm, w_vmem, stage, sem):
    rows = stage.shape[1]
    n = w_vmem.shape[0] // rows
    copy = lambda c: pltpu.make_async_copy(w_hbm.at[pl.ds(c * rows, rows)], stage.at[c % 2],
                                           sem.at[c % 2])
    copy(0).start()
    for c in range(n):
        if c + 1 < n:
            copy(c + 1).start()
        copy(c).wait()
        w_vmem[c * rows:(c + 1) * rows, :] = stage[c % 2].astype(BF16)


def _weight_scratch(k, n):
    return [pltpu.VMEM((k, n), BF16), pltpu.VMEM((2, W_STAGE_ROWS, n), F32),
            pltpu.SemaphoreType.DMA((2,))]


def _hybrid_kernel(x_ref, mod_ref, win_hbm, wout_hbm, gmix_ref, gffn_ref, lng_ref, lnb_ref, sw_ref,
                   sb_ref, cw_ref, rw_ref, rb_ref,
                   x1_ref, hs_ref, q_ref, segb_ref, segn_ref, items_ref,
                   pbuf, win_ref, win_stage, win_sem, wout_ref, wout_stage, wout_sem,
                   h2buf, rowbuf, tri, zbuf, fill_s, pend_s, ssem, zsem):
    ts = x_ref.shape[1]
    tile = pl.program_id(0) * pl.num_programs(1) + pl.program_id(1)
    n_tiles = pl.num_programs(0) * pl.num_programs(1)
    sc = _ScatterRefs(gffn_ref, rw_ref, rb_ref, hs_ref, q_ref, segb_ref, segn_ref, items_ref,
                      h2buf, rowbuf, tri, zbuf, fill_s, pend_s, ssem, zsem)

    @pl.when(tile == 0)
    def _():
        _load_weight_bf16(win_hbm.at[0], win_ref, win_stage, win_sem)
        _load_weight_bf16(wout_hbm.at[0], wout_ref, wout_stage, wout_sem)

    @pl.when(pl.program_id(1) == 0)
    def _():
        pbuf[0:B_HALO, :] = jnp.zeros((B_HALO, D_B), F32)

    _sc_init(sc, tile)
    slot = tile % 2
    _sc_pending(sc, "wait", slot)

    prev = {}
    active = (tile > 0).astype(I32)

    _sc_route(sc, prev)
    _sc_rank(sc, prev, active)
    _sc_sort(sc, prev, slot)

    x = x_ref[0]
    mod = mod_ref[0, 0]
    sh1, sc1, g1 = mod[0:1], mod[1:2], mod[2:3]
    h = _rms_mod(x, gmix_ref[...], sh1, sc1)
    hb16 = h.astype(BF16)
    zcol = lambda j: jnp.dot(hb16, win_ref[:, j * D_A:(j + 1) * D_A], preferred_element_type=F32)

    v = _layer_norm(_gelu(zcol(1)), lng_ref[...], lnb_ref[...]).astype(BF16)
    u = _gelu(zcol(0))
    p = zcol(3) * zcol(4)
    pbuf[B_HALO:B_HALO + ts, :] = p
    cw = cw_ref[...]
    conv = (cw[0:1] * pbuf[B_HALO - 2:B_HALO - 2 + ts, :]
            + cw[1:2] * pbuf[B_HALO - 1:B_HALO - 1 + ts, :]
            + cw[2:3] * p)
    pbuf[0:B_HALO, :] = p[ts - B_HALO:ts, :]
    y_b = zcol(2) * conv

    tril = (lax.broadcasted_iota(I32, (CHUNK, CHUNK), 0)
            >= lax.broadcasted_iota(I32, (CHUNK, CHUNK), 1))
    chunks = []
    for n in range(ts // CHUNK):
        heads = []
        for hd in range(A_HEADS):
            w = jnp.where(tril, sw_ref[hd], 0.0).astype(BF16)
            vb = v[n * CHUNK:(n + 1) * CHUNK, hd * A_HEAD_DIM:(hd + 1) * A_HEAD_DIM]
            heads.append(jnp.dot(w, vb, preferred_element_type=F32) + sb_ref[hd])
        chunks.append(jnp.concatenate(heads, axis=1))
    y_a = u * jnp.concatenate(chunks, axis=0)

    ycat = jnp.concatenate([y_a, y_b], axis=1).astype(BF16)
    y = jnp.dot(ycat, wout_ref[...], preferred_element_type=F32)
    x1 = x + g1 * y
    x1_ref[0] = x1
    h2buf[...] = _prenorm(x1, mod, gffn_ref)
    _sc_start(sc, prev, jnp.maximum(tile - 1, 0), slot)

    @pl.when(tile == n_tiles - 1)
    def _():
        _sc_pending(sc, "wait", 1 - slot)
        last = {}
        _sc_route(sc, last)
        _sc_rank(sc, last, 1)
        _sc_sort(sc, last, 1 - slot)
        _sc_start(sc, last, tile, 1 - slot)
        _sc_finish(sc)


def _band_matrix():
    shape = (CONV_ROWS, SUBLANES * S_ROWS)
    t = lax.broadcasted_iota(I32, shape, 0)
    c = lax.broadcasted_iota(I32, shape, 1)
    hit = c == t + S_LEAD
    for r in range(1, SUBLANES):
        hit = hit | (c == t + S_LEAD + r * (S_ROWS - 1))
    return jnp.where(hit, 1.0, 0.0).astype(BF16)


def _causal_dw_conv(cbuf, w8, obuf, band, sbuf, row0, nrows, col0, ncols, first_slot):
    n_a = -(-C_CONV // SUBLANES)
    cols = slice(col0, col0 + ncols)
    for i, t0 in enumerate(range(row0, row0 + nrows, CONV_ROWS)):
        stack = sbuf.at[(first_slot + i) % sbuf.shape[0]]
        for r in range(SUBLANES):
            part = None
            for a in range(n_a):
                k = C_CONV - 1 - SUBLANES * a - r
                if k < 0:
                    continue
                r0 = t0 + C_HALO - S_LEAD - SUBLANES * a
                z3 = cbuf[r0:r0 + S_ROWS, cols].reshape(S_ROWS // SUBLANES, SUBLANES, ncols)
                term = (z3 * w8[SUBLANES * k:SUBLANES * (k + 1), cols][None]).reshape(S_ROWS, ncols)
                part = term if part is None else part + term
            stack[r * S_ROWS:(r + 1) * S_ROWS, :] = part.astype(BF16)
        obuf[t0:t0 + CONV_ROWS, cols] = jnp.dot(band[...], stack[...], preferred_element_type=F32)


def _conformer_kernel(psegb_ref, psegn_ref,
                      xp_ref, ys_ref, pq_ref, pmod_ref, pqn_ref, mod_ref, w1_hbm, w2_hbm, gmix_ref, gffn_ref,
                      b1_ref, dww_ref, dwb_ref, lng_ref, lnb_ref, b2_ref, rw_ref, rb_ref,
                      x1_ref, hs_ref, q_ref, segb_ref, segn_ref, items_ref,
                      cbuf, obuf, band, sbuf, w8, ynext, ybuf, gsem,
                      w1_ref, w1_stage, w1_sem, w2_ref, w2_stage, w2_sem,
                      h2buf, rowbuf, tri, zbuf, fill_s, pend_s, ssem, zsem):
    ts = xp_ref.shape[1]
    tile = pl.program_id(0) * pl.num_programs(1) + pl.program_id(1)
    n_tiles = pl.num_programs(0) * pl.num_programs(1)
    sc = _ScatterRefs(gffn_ref, rw_ref, rb_ref, hs_ref, q_ref, segb_ref, segn_ref, items_ref,
                      h2buf, rowbuf, tri, zbuf, fill_s, pend_s, ssem, zsem)

    @pl.when(tile == 0)
    def _():
        _load_weight_bf16(w1_hbm.at[0], w1_ref, w1_stage, w1_sem)
        _load_weight_bf16(w2_hbm.at[0], w2_ref, w2_stage, w2_sem)
        band[...] = _band_matrix()
        for k in range(C_CONV):
            w8[SUBLANES * k:SUBLANES * (k + 1), :] = jnp.broadcast_to(dww_ref[k:k + 1, :], (SUBLANES, D_C))

    @pl.when(pl.program_id(1) == 0)
    def _():
        cbuf[0:C_HALO, :] = jnp.zeros((C_HALO, D_C), F32)

    _sc_init(sc, tile)
    slot = tile % 2
    _sc_pending(sc, "wait", slot)
    ga = _GatherAhead(tile, n_tiles, psegb_ref, psegn_ref, ys_ref, pq_ref, pqn_ref, ybuf, gsem, ynext)
    ga.top()
    y_prev = ga.current()

    prev = {}
    active = (tile > 0).astype(I32)

    x = xp_ref[0] + pmod_ref[0, 0][5:6] * y_prev
    mod = mod_ref[0, 0]
    sh1, sc1, g1 = mod[0:1], mod[1:2], mod[2:3]
    h = _rms_mod(x, gmix_ref[...], sh1, sc1)
    hb = h.astype(BF16)
    def glu(c0):
        ca, cg = slice(c0, c0 + GLU_COLS), slice(D_C + c0, D_C + c0 + GLU_COLS)
        za = jnp.dot(hb, w1_ref[:, ca], preferred_element_type=F32) + b1_ref[:, ca]
        zg = jnp.dot(hb, w1_ref[:, cg], preferred_element_type=F32) + b1_ref[:, cg]
        cbuf[C_HALO:C_HALO + ts, ca] = za * jax.nn.sigmoid(zg)

    for c0 in range(0, D_C, GLU_COLS):
        glu(c0)
    _sc_route(sc, prev)
    _sc_rank(sc, prev, active)
    _sc_sort(sc, prev, slot)
    for k, c0 in enumerate(range(0, D_C, GLU_COLS)):
        _causal_dw_conv(cbuf, w8, obuf, band, sbuf, 0, ts, c0, GLU_COLS,
                        k * (ts // CONV_ROWS))
        if k == 1:
            ga.prepare_next()
    cbuf[0:C_HALO, :] = cbuf[ts:ts + C_HALO, :]

    zc = _layer_norm(obuf[...] + dwb_ref[...], lng_ref[...], lnb_ref[...])
    zc = zc * jax.nn.sigmoid(zc)
    y = jnp.dot(zc.astype(BF16), w2_ref[...], preferred_element_type=F32) + b2_ref[...]
    x1 = x + g1 * y
    x1_ref[0] = x1
    h2buf[...] = _prenorm(x1, mod, gffn_ref)
    _sc_start(sc, prev, jnp.maximum(tile - 1, 0), slot)

    @pl.when(tile == n_tiles - 1)
    def _():
        _sc_pending(sc, "wait", 1 - slot)
        last = {}
        _sc_route(sc, last)
        _sc_rank(sc, last, 1)
        _sc_sort(sc, last, 1 - slot)
        _sc_start(sc, last, tile, 1 - slot)
        _sc_finish(sc)


def _final_kernel(psegb_ref, psegn_ref, xp_ref, ys_ref, pq_ref, pmod_ref, fg_ref, o_ref, ybuf, gsem):
    tile = pl.program_id(0) * pl.num_programs(1) + pl.program_id(1)
    n_tiles = pl.num_programs(0) * pl.num_programs(1)
    y_prev = _gather_prev_ffn(tile, n_tiles, psegb_ref, psegn_ref, ys_ref, pq_ref, ybuf, gsem)
    x = xp_ref[0] + pmod_ref[0, 0][5:6] * y_prev
    o_ref[0] = _rms(x, fg_ref[...])


def _scatter_outputs(bsz, seq, d):
    t = bsz * seq
    n_tiles = t // SEQ_TILE
    smem = pl.BlockSpec(memory_space=pltpu.SMEM)
    specs = [pl.BlockSpec((1, SEQ_TILE, d), lambda b, s, *_: (b, s, 0)),
             pl.BlockSpec(memory_space=pl.ANY),
             pl.BlockSpec((n_tiles, 1, SEQ_TILE), lambda b, s, *_: (0, 0, 0)),
             smem, smem, smem]
    shapes = [jax.ShapeDtypeStruct((bsz, seq, d), F32),
              jax.ShapeDtypeStruct((N_GROUPS * _region_rows(t), ROW_W), BF16),
              jax.ShapeDtypeStruct((n_tiles, 1, SEQ_TILE), I32),
              jax.ShapeDtypeStruct((n_tiles * N_GROUPS,), I32),
              jax.ShapeDtypeStruct((n_tiles * N_GROUPS,), I32),
              jax.ShapeDtypeStruct((ITEM_FIELDS * _moe_steps(t),), I32)]
    scratch = [pltpu.VMEM((SEQ_TILE, d), F32),
               pltpu.VMEM((2, SORT_ROWS, ROW_W), BF16),
               pltpu.VMEM((SEQ_TILE, SEQ_TILE), BF16),
               pltpu.VMEM((MOE_TILE, ROW_W), BF16),
               pltpu.SMEM((N_GROUPS,), I32),
               pltpu.SMEM((4 * N_GROUPS,), I32),
               pltpu.SemaphoreType.DMA((2,)),
               pltpu.SemaphoreType.DMA(())]
    return specs, shapes, scratch


def _mod_spec(d, layer):
    return pl.BlockSpec((1, 1, N_MOD, d), lambda b, s, *_: (layer, b, 0, 0))


def _gather_inputs(bsz, seq, d, prev_layer):
    nseq = seq // SEQ_TILE
    specs = [pl.BlockSpec((1, SEQ_TILE, d), lambda b, s, *_: (b, s, 0)),
             pl.BlockSpec(memory_space=pl.ANY),
             pl.BlockSpec((1, 1, SEQ_TILE), lambda b, s, *_: (b * nseq + s, 0, 0)),
             _mod_spec(d, prev_layer)]
    scratch = [pltpu.VMEM((2, SORT_ROWS, d), BF16), pltpu.SemaphoreType.DMA((2,))]
    return specs, scratch


def _const_specs(consts):
    return [pl.BlockSpec(a.shape, lambda b, s, *_, nd=a.ndim: (0,) * nd) for a in consts]


_PARAMS = pltpu.CompilerParams(dimension_semantics=("arbitrary", "arbitrary"),
                               vmem_limit_bytes=VMEM_LIMIT)


def _hybrid_call(x, mod, layer, w_in, w_out, consts):
    bsz, seq, d = x.shape
    out_specs, out_shapes, sc_scratch = _scatter_outputs(bsz, seq, d)
    hbm = pl.BlockSpec(memory_space=pl.ANY)
    return pl.pallas_call(
        _hybrid_kernel,
        grid=(bsz, seq // SEQ_TILE),
        in_specs=[pl.BlockSpec((1, SEQ_TILE, d), lambda b, s: (b, s, 0)),
                  _mod_spec(d, layer), hbm, hbm]
                 + _const_specs(consts),
        out_specs=out_specs,
        out_shape=out_shapes,
        scratch_shapes=[pltpu.VMEM((SEQ_TILE + B_HALO, D_B), F32)]
                       + _weight_scratch(*w_in.shape[1:]) + _weight_scratch(*w_out.shape[1:])
                       + sc_scratch,
        compiler_params=_PARAMS,
        name="mixer_hybrid",
    )(x, mod, w_in, w_out, *consts)


def _conformer_call(prev, mod, layer, w1, w2, consts):
    x1p, ysp, qp, segbp, segnp = prev
    bsz, seq, d = x1p.shape
    g_specs, _ = _gather_inputs(bsz, seq, d, layer - 1)
    nseq = seq // SEQ_TILE
    last = bsz * nseq - 1
    g_specs = g_specs + [pl.BlockSpec((1, 1, SEQ_TILE),
                                      lambda b, s, *_: (jnp.minimum(b * nseq + s + 1, last), 0, 0))]
    g_scratch = [pltpu.VMEM((SEQ_TILE, d), BF16), pltpu.VMEM((3, SORT_ROWS, d), BF16),
                 pltpu.SemaphoreType.DMA((3,))]
    out_specs, out_shapes, sc_scratch = _scatter_outputs(bsz, seq, d)
    hbm = pl.BlockSpec(memory_space=pl.ANY)
    grid_spec = pltpu.PrefetchScalarGridSpec(
        num_scalar_prefetch=2,
        grid=(bsz, seq // SEQ_TILE),
        in_specs=g_specs + [_mod_spec(d, layer), hbm, hbm]
                 + _const_specs(consts),
        out_specs=out_specs,
        scratch_shapes=[pltpu.VMEM((SEQ_TILE + C_HALO, D_C), F32), pltpu.VMEM((SEQ_TILE, D_C), F32),
                        pltpu.VMEM((CONV_ROWS, SUBLANES * S_ROWS), BF16),
                        pltpu.VMEM((S_SLOTS, SUBLANES * S_ROWS, GLU_COLS), BF16),
                        pltpu.VMEM((C_CONV * SUBLANES, D_C), F32)]
                       + g_scratch + _weight_scratch(*w1.shape[1:]) + _weight_scratch(*w2.shape[1:])
                       + sc_scratch,
    )
    return pl.pallas_call(
        _conformer_kernel, grid_spec=grid_spec, out_shape=out_shapes,
        compiler_params=_PARAMS, name="mixer_conformer",
    )(segbp, segnp, x1p, ysp, qp, mod, qp, mod, w1, w2, *consts)


def _final_call(prev, mod, prev_layer, final_g):
    x1p, ysp, qp, segbp, segnp = prev
    bsz, seq, d = x1p.shape
    g_specs, g_scratch = _gather_inputs(bsz, seq, d, prev_layer)
    grid_spec = pltpu.PrefetchScalarGridSpec(
        num_scalar_prefetch=2,
        grid=(bsz, seq // SEQ_TILE),
        in_specs=g_specs + [pl.BlockSpec((1, d), lambda b, s, *_: (0, 0))],
        out_specs=pl.BlockSpec((1, SEQ_TILE, d), lambda b, s, *_: (b, s, 0)),
        scratch_shapes=g_scratch,
    )
    return pl.pallas_call(
        _final_kernel, grid_spec=grid_spec,
        out_shape=jax.ShapeDtypeStruct((bsz, seq, d), F32),
        compiler_params=_PARAMS, name="final_norm",
    )(segbp, segnp, x1p, ysp, qp, mod, final_g.reshape(1, d))


def _moe_kernel(items_ref, hs_ref, wg_hbm, wu_hbm, wd_hbm, ys_ref,
                wg_st, wu_st, wd_st, wsem, wg_bf, wu_bf, wd_bf, slot_s, *, layer):
    i = pl.program_id(0)
    n_items = pl.num_programs(0)
    grp = items_ref[n_items + i]

    def fetch(g, slot):
        src = layer * N_GROUPS + g
        return [pltpu.make_async_copy(w.at[src], st.at[slot], wsem.at[slot, k])
                for k, (w, st) in enumerate(((wg_hbm, wg_st), (wu_hbm, wu_st), (wd_hbm, wd_st)))]

    @pl.when(i == 0)
    def _():
        slot_s[0] = 0
        for cp in fetch(grp, 0):
            cp.start()

    @pl.when(items_ref[3 * n_items + i] != 0)
    def _():
        slot = slot_s[0]
        for cp in fetch(grp, slot):
            cp.wait()
        wg_bf[...] = wg_st[slot].astype(BF16)
        wu_bf[...] = wu_st[slot].astype(BF16)
        wd_bf[...] = wd_st[slot].astype(BF16)
        nxt = items_ref[4 * n_items + i]

        @pl.when(nxt >= 0)
        def _():
            for cp in fetch(nxt, 1 - slot):
                cp.start()

        slot_s[0] = 1 - slot

    @pl.when(items_ref[2 * n_items + i] != 0)
    def _():
        rows = hs_ref[...]
        h = rows[:, :D_MODEL]
        m = rows[:, D_MODEL:].astype(F32)
        ne = EXPERTS_PER_GROUP
        y = None
        for j in range(ne):
            gate = m[:, j:j + 1] + (m[:, ne + j:ne + j + 1] + m[:, 2 * ne + j:2 * ne + j + 1])
            a = jnp.dot(h, wg_bf[j], preferred_element_type=F32)
            a = a * jax.nn.sigmoid(a) * jnp.dot(h, wu_bf[j], preferred_element_type=F32)
            yj = jnp.dot((a * gate).astype(BF16), wd_bf[j], preferred_element_type=F32)
            y = yj if y is None else y + yj
        ys_ref[...] = y.astype(BF16)


def _moe_call(hs, items, layer, wg, wu, wd):
    d = D_MODEL
    ne = EXPERTS_PER_GROUP
    n_items = items.shape[0] // ITEM_FIELDS
    row_idx = lambda i, items: (items[i], 0)
    hbm = pl.BlockSpec(memory_space=pl.ANY)
    grid_spec = pltpu.PrefetchScalarGridSpec(
        num_scalar_prefetch=1,
        grid=(n_items,),
        in_specs=[pl.BlockSpec((MOE_TILE, ROW_W), row_idx), hbm, hbm, hbm],
        out_specs=pl.BlockSpec((MOE_TILE, d), row_idx),
        scratch_shapes=[pltpu.VMEM((2, ne, d, D_EXPERT), F32), pltpu.VMEM((2, ne, d, D_EXPERT), F32),
                        pltpu.VMEM((2, ne, D_EXPERT, d), F32), pltpu.SemaphoreType.DMA((2, 3)),
                        pltpu.VMEM((ne, d, D_EXPERT), BF16), pltpu.VMEM((ne, d, D_EXPERT), BF16),
                        pltpu.VMEM((ne, D_EXPERT, d), BF16), pltpu.SMEM((1,), I32)],
    )
    return pl.pallas_call(
        functools.partial(_moe_kernel, layer=layer), grid_spec=grid_spec,
        out_shape=jax.ShapeDtypeStruct((hs.shape[0], d), BF16),
        compiler_params=pltpu.CompilerParams(dimension_semantics=("arbitrary",),
                                             vmem_limit_bytes=VMEM_LIMIT),
        name="moe",
    )(items, hs, wg, wu, wd)


def kernel(x, c, ada_w, ada_b, norm_mix_g, norm_ffn_g, hyb_w_in, sgu_ln_g, sgu_ln_b, sgu_w, sgu_b,
           sconv_w, hyb_w_out, conf_w1, conf_b1, conf_dw_w, conf_dw_b, conf_ln_g, conf_ln_b,
           conf_w2, conf_b2, router_w, router_bias, moe_w_gate, moe_w_up, moe_w_down, final_norm_g):
    depth = ada_w.shape[0]
    bsz, seq, d = x.shape
    assert depth == 2 and seq % SEQ_TILE == 0
    mod = _ada_mod(c, ada_w, ada_b).reshape(depth, bsz, N_MOD, d)

    perm = jnp.arange(N_EXPERTS).reshape(N_GROUPS, EXPERTS_PER_GROUP).T.reshape(-1)
    rw_p = router_w[:, perm]
    rw_t = jnp.concatenate([rw_p, rw_p, jnp.zeros((d, LANES - 2 * N_EXPERTS), F32)], axis=1)
    rb_t = router_bias[perm].reshape(N_EXPERTS, 1)

    ne = EXPERTS_PER_GROUP
    wg = moe_w_gate.reshape(depth * N_GROUPS, ne, d, D_EXPERT)
    wu = moe_w_up.reshape(depth * N_GROUPS, ne, d, D_EXPERT)
    wd = moe_w_down.reshape(depth * N_GROUPS, ne, D_EXPERT, d)

    row = lambda a: a.reshape(1, -1)
    consts0 = [row(norm_mix_g[0]), row(norm_ffn_g[0]), row(sgu_ln_g[0]), row(sgu_ln_b[0]), sgu_w[0],
               sgu_b[0].reshape(A_HEADS, CHUNK, 1), sconv_w[0], rw_t, rb_t]
    x1, hs, q, segb, segn, items = _hybrid_call(x, mod, 0, hyb_w_in, hyb_w_out, consts0)
    ys = _moe_call(hs, items, 0, wg, wu, wd)

    consts1 = [row(norm_mix_g[1]), row(norm_ffn_g[1]), row(conf_b1[0]), conf_dw_w[0],
               row(conf_dw_b[0]), row(conf_ln_g[0]), row(conf_ln_b[0]), row(conf_b2[0]), rw_t, rb_t]
    x1, hs, q1, segb1, segn1, items = _conformer_call((x1, ys, q, segb, segn), mod, 1,
                                                      conf_w1, conf_w2, consts1)
    ys = _moe_call(hs, items, 1, wg, wu, wd)
    return _final_call((x1, ys, q1, segb1, segn1), mod, 1, final_norm_g)
```
